```python
import numpy as np
import jax
import jax.numpy as jnp
from jax import lax

D_MODEL = 1024
BATCH = 8
SEQ = 2048
DEPTH = 2

GRID_W = 64
CTX_LEN = 256
Q_BLOCK = 128
ROPE_BASE = 10000.0
EPS = 1e-6

MLA_HEADS = 8
MLA_NOPE = 64
MLA_ROPE = 32
MLA_V = 64
MLA_Q_LORA = 256
MLA_KV_LORA = 128
RET_HEADS = 4
RET_DK = 64
RET_DV = 128
RET_CHUNK = 128
LRU_WIDTH = 512
LRU_BLOCKS = 8
LRU_BLOCK_W = LRU_WIDTH // LRU_BLOCKS
LRU_C = 8.0
CONV_W = 4
CONV_PAD_LEFT = 2
GQA_HEADS = 8
GQA_KV_HEADS = 2
GQA_HEAD_DIM = 64
N_BRANCH = 4
BRANCH_W = 512
N_EXPERTS = 64
N_GROUPS = 8
EXPERTS_PER_GROUP = N_EXPERTS // N_GROUPS
TOP_K = 2
EXPERT_FF = 256
MOE_BLOCK = 128

IN_SPLITS = (MLA_Q_LORA, MLA_KV_LORA, MLA_ROPE,
             RET_HEADS * RET_DK, RET_HEADS * RET_DK, RET_HEADS * RET_DV, RET_HEADS * RET_DV,
             LRU_WIDTH, LRU_WIDTH,
             GQA_HEADS * GQA_HEAD_DIM, GQA_KV_HEADS * GQA_HEAD_DIM, GQA_KV_HEADS * GQA_HEAD_DIM)
D_IN = sum(IN_SPLITS)
IN_OFFSETS = tuple(int(o) for o in np.cumsum(IN_SPLITS)[:-1])

kernel_name = "hybrid_mla_retnet_rglru_gqa_grouped_moe_dit"


def rms_norm(x, g):
    xf = x.astype(jnp.float32)
    y = xf * lax.rsqrt(jnp.mean(xf * xf, axis=-1, keepdims=True) + EPS)
    return (y * g.astype(jnp.float32)).astype(x.dtype)


def modulate(h, shift, scale):
    return h * (1 + scale) + shift


def group_layer_norm(o, g):
    of = o.astype(jnp.float32)
    mu = jnp.mean(of, axis=-1, keepdims=True)
    var = jnp.mean(jnp.square(of - mu), axis=-1, keepdims=True)
    y = ((of - mu) * lax.rsqrt(var + EPS)).reshape(o.shape[0], o.shape[1], -1)
    return (y * g.astype(jnp.float32)).astype(o.dtype)


def axial_angles(rows, cols, rot_dim):
    half = rot_dim // 2
    inv = ROPE_BASE ** (-jnp.arange(0, half, 2, dtype=jnp.float32) / half)
    return rows.astype(jnp.float32)[:, None] * inv, cols.astype(jnp.float32)[:, None] * inv


def rope_1d(x, ang):
    x1, x2 = jnp.split(x, 2, axis=-1)
    cos = jnp.cos(ang)[:, None, :].astype(x.dtype)
    sin = jnp.sin(ang)[:, None, :].astype(x.dtype)
    return jnp.concatenate([x1 * cos - x2 * sin, x2 * cos + x1 * sin], axis=-1)


def rope_2d(x, ang_row, ang_col):
    half = x.shape[-1] // 2
    return jnp.concatenate([rope_1d(x[..., :half], ang_row), rope_1d(x[..., half:], ang_col)], axis=-1)


def merge_heads(o):
    B, H, L, d = o.shape
    return o.transpose(0, 2, 1, 3).reshape(B, L, H * d)


def attend_blocked(q, k, v, scale):
    B, HQ, L, DK = q.shape
    HK = k.shape[1]
    G = HQ // HK
    nb = L // Q_BLOCK
    qb = q.reshape(B, HK, G, nb, Q_BLOCK, DK).transpose(3, 0, 1, 2, 4, 5)

    def block(qi):
        s = jnp.einsum('bhgqd,bhkd->bhgqk', qi, k, preferred_element_type=jnp.float32) * scale
        p = jax.nn.softmax(s, axis=-1).astype(v.dtype)
        return jnp.einsum('bhgqk,bhkd->bhgqd', p, v)

    o = lax.map(block, qb)
    return o.transpose(1, 2, 3, 0, 4, 5).reshape(B, HQ, L, v.shape[-1])


def mla_qkv(cq, ckv, k_rope, q_norm, kv_norm, w_uq, w_ukv, ang):
    B, L, _ = cq.shape
    q = (rms_norm(cq, q_norm) @ w_uq).reshape(B, L, MLA_HEADS, MLA_NOPE + MLA_ROPE)
    kv = (rms_norm(ckv, kv_norm) @ w_ukv).reshape(B, L, MLA_HEADS, MLA_NOPE + MLA_V)
    q_nope, q_rot = q[..., :MLA_NOPE], q[..., MLA_NOPE:]
    k_nope, v = kv[..., :MLA_NOPE], kv[..., MLA_NOPE:]
    k_rot = k_rope[:, :, None, :]
    if ang is not None:
        q_rot = rope_2d(q_rot, *ang)
        k_rot = rope_2d(k_rot, *ang)
    k_rot = jnp.broadcast_to(k_rot, (B, L, MLA_HEADS, MLA_ROPE))
    q = jnp.concatenate([q_nope, q_rot], axis=-1)
    k = jnp.concatenate([k_nope, k_rot], axis=-1)
    return q.transpose(0, 2, 1, 3), k.transpose(0, 2, 1, 3), v.transpose(0, 2, 1, 3)


def mla_branch(zl, zc, ang, with_ctx_out, q_norm, kv_norm, w_uq, w_ukv):
    ql, kl, vl = mla_qkv(zl[0], zl[1], zl[2], q_norm, kv_norm, w_uq, w_ukv, ang)
    qc, kc, vc = mla_qkv(zc[0], zc[1], zc[2], q_norm, kv_norm, w_uq, w_ukv, None)
    scale = (MLA_NOPE + MLA_ROPE) ** -0.5
    y_lat = merge_heads(attend_blocked(ql, jnp.concatenate([kl, kc], axis=2),
                                       jnp.concatenate([vl, vc], axis=2), scale))
    y_ctx = merge_heads(attend_blocked(qc, kc, vc, scale)) if with_ctx_out else None
    return y_lat, y_ctx


def retention_chunked(q, k, v, log_gamma, s0, inclusive):
    B, H, L, DK = q.shape
    DV = v.shape[-1]
    nc = L // RET_CHUNK
    pos = jnp.arange(RET_CHUNK, dtype=jnp.float32)
    diff = pos[:, None] - pos[None, :]
    mask = diff >= 0 if inclusive else diff > 0
    lg = log_gamma.astype(jnp.float32)
    intra = jnp.where(mask, jnp.exp(lg[:, None, None] * jnp.where(mask, diff, 0.0)), 0.0)
    q_dec = jnp.exp(lg[:, None] * (pos + 1.0))[None, :, :, None]
    k_dec = jnp.exp(lg[:, None] * (RET_CHUNK - 1.0 - pos))[None, :, :, None]
    c_dec = jnp.exp(lg * RET_CHUNK)[None, :, None, None]

    def to_chunks(t):
        return t.astype(jnp.float32).reshape(B, H, nc, RET_CHUNK, t.shape[-1]).transpose(2, 0, 1, 3, 4)

    def step(s, qkv):
        qi, ki, vi = qkv
        scores = jnp.einsum('bhid,bhjd->bhij', qi, ki) * intra
        o = jnp.einsum('bhij,bhje->bhie', scores, vi) + jnp.einsum('bhid,bhde->bhie', qi * q_dec, s)
        s = s * c_dec + jnp.einsum('bhjd,bhje->bhde', ki * k_dec, vi)
        return s, o

    s_fin, o = lax.scan(step, s0, (to_chunks(q), to_chunks(k), to_chunks(v)))
    o = o.transpose(1, 2, 0, 3, 4).reshape(B, H, L, DV)
    return s_fin, o.astype(v.dtype)


def retention_heads(q, k, v, ang):
    B, L, _ = q.shape
    q = q.reshape(B, L, RET_HEADS, RET_DK)
    k = k.reshape(B, L, RET_HEADS, RET_DK) * (RET_DK ** -0.5)
    v = v.reshape(B, L, RET_HEADS, RET_DV)
    if ang is not None:
        q = rope_2d(q, *ang)
        k = rope_2d(k, *ang)
    return q.transpose(0, 2, 1, 3), k.transpose(0, 2, 1, 3), v.transpose(0, 2, 1, 3)


def retention_branch(zl, zc, ang, with_ctx_out, ret_decay, ret_norm):
    ql, kl, vl = retention_heads(zl[3], zl[4], zl[5], ang)
    qc, kc, vc = retention_heads(zc[3], zc[4], zc[5], None)
    log_g = -jax.nn.softplus(-ret_decay.astype(jnp.float32))
    s0 = jnp.zeros((ql.shape[0], RET_HEADS, RET_DK, RET_DV), jnp.float32)

    def flip(t):
        return t[:, :, ::-1]

    s_cf, o_cf = retention_chunked(qc, kc, vc, log_g[0], s0, True)
    s_cb, o_cb = retention_chunked(flip(qc), flip(kc), flip(vc), log_g[1], s0, False)
    _, o_lf = retention_chunked(ql, kl, vl, log_g[0], s_cf, True)
    _, o_lb = retention_chunked(flip(ql), flip(kl), flip(vl), log_g[1], s_cb, False)

    def finish(o, gate):
        return jax.nn.silu(gate) * group_layer_norm(o.transpose(0, 2, 1, 3), ret_norm).astype(gate.dtype)

    y_lat = finish(o_lf + flip(o_lb), zl[6])
    y_ctx = finish(o_cf + flip(o_cb), zc[6]) if with_ctx_out else None
    return y_lat, y_ctx


def short_conv(u, w, b):
    L = u.shape[1]
    up = jnp.pad(u, ((0, 0), (CONV_PAD_LEFT, CONV_W - 1 - CONV_PAD_LEFT), (0, 0)))
    out = b
    for j in range(CONV_W):
        out = out + up[:, j:j + L] * w[j]
    return out


def rglru_coeffs(u, w_a, b_a, w_x, b_x, lam):
    B, L, W = u.shape
    ub = u.reshape(B, L, LRU_BLOCKS, LRU_BLOCK_W)
    r = jax.nn.sigmoid(jnp.einsum('blnc,ncd->blnd', ub, w_a).reshape(B, L, W) + b_a)
    i = jax.nn.sigmoid(jnp.einsum('blnc,ncd->blnd', ub, w_x).reshape(B, L, W) + b_x)
    log_a = (-LRU_C * r.astype(jnp.float32)) * jax.nn.softplus(-lam.astype(jnp.float32))
    a = jnp.exp(log_a)
    b = jnp.sqrt(-jnp.expm1(2.0 * log_a)) * (i * u).astype(jnp.float32)
    return a, b


def linear_recurrence(a, b, h0):
    b = b.at[:, 0].add(a[:, 0] * h0)

    def combine(left, right):
        a_l, b_l = left
        a_r, b_r = right
        return a_l * a_r, a_r * b_l + b_r

    _, h = lax.associative_scan(combine, (a, b), axis=1)
    return h


def rglru_branch(zl, zc, with_ctx_out, conv_w, conv_b, w_a, b_a, w_x, b_x, lam):
    u_lat = short_conv(zl[7], conv_w, conv_b)
    u_ctx = short_conv(zc[7], conv_w, conv_b)
    h_lat_dirs, h_ctx_dirs = [], []
    for d in range(2):
        def f(t, d=d):
            return t[:, ::-1] if d == 1 else t
        a_c, b_c = rglru_coeffs(f(u_ctx), w_a[d], b_a[d], w_x[d], b_x[d], lam[d])
        h_c = linear_recurrence(a_c, b_c, jnp.zeros_like(b_c[:, 0]))
        a_l, b_l = rglru_coeffs(f(u_lat), w_a[d], b_a[d], w_x[d], b_x[d], lam[d])
        h_l = linear_recurrence(a_l, b_l, h_c[:, -1])
        h_lat_dirs.append(f(h_l))
        h_ctx_dirs.append(f(h_c))
    y_lat = (h_lat_dirs[0] + h_lat_dirs[1]).astype(zl[8].dtype) * jax.nn.gelu(zl[8])
    y_ctx = ((h_ctx_dirs[0] + h_ctx_dirs[1]).astype(zc[8].dtype) * jax.nn.gelu(zc[8])) if with_ctx_out else None
    return y_lat, y_ctx


def gqa_qkv(q, k, v, q_norm, k_norm, ang):
    B, L, _ = q.shape
    q = rms_norm(q.reshape(B, L, GQA_HEADS, GQA_HEAD_DIM), q_norm)
    k = rms_norm(k.reshape(B, L, GQA_KV_HEADS, GQA_HEAD_DIM), k_norm)
    v = v.reshape(B, L, GQA_KV_HEADS, GQA_HEAD_DIM)
    if ang is not None:
        q = rope_2d(q, *ang)
        k = rope_2d(k, *ang)
    return q.transpose(0, 2, 1, 3), k.transpose(0, 2, 1, 3), v.transpose(0, 2, 1, 3)


def gqa_branch(zl, zc, ang, with_ctx_out, q_norm, k_norm):
    ql, kl, vl = gqa_qkv(zl[9], zl[10], zl[11], q_norm, k_norm, ang)
    qc, kc, vc = gqa_qkv(zc[9], zc[10], zc[11], q_norm, k_norm, None)
    scale = GQA_HEAD_DIM ** -0.5
    y_lat = merge_heads(attend_blocked(ql, jnp.concatenate([kl, kc], axis=2),
                                       jnp.concatenate([vl, vc], axis=2), scale))
    y_ctx = merge_heads(attend_blocked(qc, kc, vc, scale)) if with_ctx_out else None
    return y_lat, y_ctx


def merge_branches(h, branches, w_branch, w_merge, b_merge, w_out):
    stack = jnp.stack(branches, axis=2)
    proj = jnp.einsum('blnm,nmd->blnd', stack, w_branch)
    gates = jax.nn.sigmoid(h @ w_merge + b_merge).reshape(h.shape[0], h.shape[1], N_BRANCH, D_MODEL)
    return jnp.sum(gates * proj, axis=2) @ w_out


def token_mixers(h_lat, h_ctx, ang_mla, ang_head, with_ctx_out, w_in,
                 mla_q_norm, mla_kv_norm, mla_w_uq, mla_w_ukv, ret_decay, ret_norm,
                 lru_conv_w, lru_conv_b, lru_w_a, lru_b_a, lru_w_x, lru_b_x, lru_lambda,
                 gqa_q_norm, gqa_k_norm, w_branch, w_merge, b_merge, w_out):
    zl = jnp.split(h_lat @ w_in, list(IN_OFFSETS), axis=-1)
    zc = jnp.split(h_ctx @ w_in, list(IN_OFFSETS), axis=-1)
    ya = mla_branch(zl, zc, ang_mla, with_ctx_out, mla_q_norm, mla_kv_norm, mla_w_uq, mla_w_ukv)
    yb = retention_branch(zl, zc, ang_head, with_ctx_out, ret_decay, ret_norm)
    yc = rglru_branch(zl, zc, with_ctx_out, lru_conv_w, lru_conv_b, lru_w_a, lru_b_a, lru_w_x, lru_b_x, lru_lambda)
    yd = gqa_branch(zl, zc, ang_head, with_ctx_out, gqa_q_norm, gqa_k_norm)
    out_lat = merge_branches(h_lat, [ya[0], yb[0], yc[0], yd[0]], w_branch, w_merge, b_merge, w_out)
    if not with_ctx_out:
        return out_lat, None
    out_ctx = merge_branches(h_ctx, [ya[1], yb[1], yc[1], yd[1]], w_branch, w_merge, b_merge, w_out)
    return out_lat, out_ctx


def moe_ffn(h, router_w, router_bias, w1, w3, w2):
    T, D = h.shape
    scores = jax.nn.sigmoid(jnp.dot(h, router_w, preferred_element_type=jnp.float32))
    biased = (scores + router_bias.astype(jnp.float32)).reshape(T, N_GROUPS, EXPERTS_PER_GROUP)
    group_score = jnp.sum(lax.top_k(biased, 2)[0], axis=-1)
    grp = jnp.argmax(group_score, axis=-1)
    in_grp = jnp.take_along_axis(biased, grp[:, None, None], axis=1)[:, 0]
    _, local = lax.top_k(in_grp, TOP_K)
    expert = grp[:, None] * EXPERTS_PER_GROUP + local
    w = jnp.take_along_axis(scores, expert, axis=1)
    w = w / jnp.sum(w, axis=-1, keepdims=True)

    TK = T * TOP_K
    flat_e = expert.reshape(-1)
    flat_tok = jnp.arange(TK, dtype=jnp.int32) // TOP_K
    order = jnp.argsort(flat_e)
    e_sorted = flat_e[order]
    tok_sorted = flat_tok[order]
    w_sorted = w.reshape(-1)[order]
    counts = jnp.bincount(flat_e, length=N_EXPERTS)
    padded = (counts + MOE_BLOCK - 1) // MOE_BLOCK * MOE_BLOCK
    pad_end = jnp.cumsum(padded)
    pad_start = pad_end - padded
    start = jnp.cumsum(counts) - counts
    slot = pad_start[e_sorted] + (jnp.arange(TK, dtype=jnp.int32) - start[e_sorted])
    n_blocks = -(-TK // MOE_BLOCK) + N_EXPERTS
    slot_tok = jnp.full((n_blocks * MOE_BLOCK,), T, jnp.int32).at[slot].set(tok_sorted)
    block_expert = jnp.minimum(
        jnp.searchsorted(pad_end, jnp.arange(n_blocks, dtype=jnp.int32) * MOE_BLOCK, side='right'),
        N_EXPERTS - 1)
    h_pad = jnp.concatenate([h, jnp.zeros((1, D), h.dtype)], axis=0)
    xb = h_pad[slot_tok].reshape(n_blocks, MOE_BLOCK, D)

    def expert_block(args):
        xi, e = args
        return (jax.nn.silu(xi @ w1[e]) * (xi @ w3[e])) @ w2[e]

    yb = lax.map(expert_block, (xb, block_expert)).reshape(-1, D)
    y_assign = yb[slot] * w_sorted[:, None].astype(yb.dtype)
    return jax.ops.segment_sum(y_assign, tok_sorted, num_segments=T)


def setup_inputs(seed: int = 0) -> dict:
    key = jax.random.key(seed)
    ks = iter(jax.random.split(key, 48))
    f32 = jnp.float32

    def nrm(shape, fan_in, gain=1.0):
        return gain * fan_in ** -0.5 * jax.random.normal(next(ks), shape, f32)

    def gain_vec(shape):
        return 1.0 + 0.05 * jax.random.normal(next(ks), shape, f32)

    def bias(shape):
        return 0.01 * jax.random.normal(next(ks), shape, f32)

    ret_base = jnp.log(2.0 ** (5.0 + jnp.arange(RET_HEADS, dtype=f32)) - 1.0)
    lru_u = jax.random.uniform(next(ks), (DEPTH, 2, LRU_WIDTH), f32, 0.9, 0.999)
    lru_a0 = lru_u ** (1.0 / LRU_C)
    return {
        "x": jax.random.normal(next(ks), (BATCH, SEQ, D_MODEL), f32),
        "c": jax.random.normal(next(ks), (BATCH, D_MODEL), f32),
        "ctx": jax.random.normal(next(ks), (BATCH, CTX_LEN, D_MODEL), f32),
        "c_ctx": jax.random.normal(next(ks), (D_MODEL,), f32),
        "ada_w": nrm((DEPTH, D_MODEL, 6 * D_MODEL), D_MODEL, 0.5),
        "ada_b": bias((DEPTH, 6 * D_MODEL)),
        "norm1_g": gain_vec((DEPTH, D_MODEL)),
        "norm2_g": gain_vec((DEPTH, D_MODEL)),
        "w_in": nrm((DEPTH, D_MODEL, D_IN), D_MODEL),
        "mla_q_norm": gain_vec((DEPTH, MLA_Q_LORA)),
        "mla_kv_norm": gain_vec((DEPTH, MLA_KV_LORA)),
        "mla_w_uq": nrm((DEPTH, MLA_Q_LORA, MLA_HEADS * (MLA_NOPE + MLA_ROPE)), MLA_Q_LORA),
        "mla_w_ukv": nrm((DEPTH, MLA_KV_LORA, MLA_HEADS * (MLA_NOPE + MLA_V)), MLA_KV_LORA),
        "ret_decay": ret_base + 0.1 * jax.random.normal(next(ks), (DEPTH, 2, RET_HEADS), f32),
        "ret_norm": gain_vec((DEPTH, RET_HEADS * RET_DV)),
        "lru_conv_w": nrm((DEPTH, CONV_W, LRU_WIDTH), CONV_W),
        "lru_conv_b": bias((DEPTH, LRU_WIDTH)),
        "lru_w_a": nrm((DEPTH, 2, LRU_BLOCKS, LRU_BLOCK_W, LRU_BLOCK_W), LRU_BLOCK_W),
        "lru_b_a": bias((DEPTH, 2, LRU_WIDTH)),
        "lru_w_x": nrm((DEPTH, 2, LRU_BLOCKS, LRU_BLOCK_W, LRU_BLOCK_W), LRU_BLOCK_W),
        "lru_b_x": bias((DEPTH, 2, LRU_WIDTH)),
        "lru_lambda": jnp.log(lru_a0) - jnp.log1p(-lru_a0),
        "gqa_q_norm": gain_vec((DEPTH, GQA_HEAD_DIM)),
        "gqa_k_norm": gain_vec((DEPTH, GQA_HEAD_DIM)),
        "w_branch": nrm((DEPTH, N_BRANCH, BRANCH_W, D_MODEL), BRANCH_W),
        "w_merge": nrm((DEPTH, D_MODEL, N_BRANCH * D_MODEL), D_MODEL),
        "b_merge": bias((DEPTH, N_BRANCH * D_MODEL)),
        "w_out": nrm((DEPTH, D_MODEL, D_MODEL), D_MODEL),
        "router_w": nrm((D_MODEL, N_EXPERTS), D_MODEL),
        "router_bias": bias((N_EXPERTS,)),
        "moe_w1": nrm((DEPTH, N_EXPERTS, D_MODEL, EXPERT_FF), D_MODEL),
        "moe_w3": nrm((DEPTH, N_EXPERTS, D_MODEL, EXPERT_FF), D_MODEL),
        "moe_w2": nrm((DEPTH, N_EXPERTS, EXPERT_FF, D_MODEL), EXPERT_FF),
        "final_norm": gain_vec((D_MODEL,)),
    }


def reference(x, c, ctx, c_ctx, ada_w, ada_b, norm1_g, norm2_g, w_in,
              mla_q_norm, mla_kv_norm, mla_w_uq, mla_w_ukv, ret_decay, ret_norm,
              lru_conv_w, lru_conv_b, lru_w_a, lru_b_a, lru_w_x, lru_b_x, lru_lambda,
              gqa_q_norm, gqa_k_norm, w_branch, w_merge, b_merge, w_out,
              router_w, router_bias, moe_w1, moe_w3, moe_w2, final_norm):
    B, L, D = x.shape
    n_rows = L // GRID_W
    rows = jnp.repeat(jnp.arange(n_rows, dtype=jnp.int32), GRID_W)
    cols = jnp.tile(jnp.arange(GRID_W, dtype=jnp.int32), n_rows)
    ang_mla = axial_angles(rows, cols, MLA_ROPE)
    ang_head = axial_angles(rows, cols, GQA_HEAD_DIM)

    x_lat, x_ctx = x, ctx
    for l in range(DEPTH):
        last = l == DEPTH - 1
        m_lat = jnp.split((jax.nn.silu(c) @ ada_w[l] + ada_b[l])[:, None, :], 6, axis=-1)
        m_ctx = jnp.split((jax.nn.silu(c_ctx) @ ada_w[l] + ada_b[l])[None, None, :], 6, axis=-1)
        h_lat = modulate(rms_norm(x_lat, norm1_g[l]), m_lat[0], m_lat[1])
        h_ctx = modulate(rms_norm(x_ctx, norm1_g[l]), m_ctx[0], m_ctx[1])
        y_lat, y_ctx = token_mixers(h_lat, h_ctx, ang_mla, ang_head, not last, w_in[l],
                                    mla_q_norm[l], mla_kv_norm[l], mla_w_uq[l], mla_w_ukv[l],
                                    ret_decay[l], ret_norm[l],
                                    lru_conv_w[l], lru_conv_b[l], lru_w_a[l], lru_b_a[l],
                                    lru_w_x[l], lru_b_x[l], lru_lambda[l],
                                    gqa_q_norm[l], gqa_k_norm[l],
                                    w_branch[l], w_merge[l], b_merge[l], w_out[l])
        x_lat = x_lat + m_lat[2] * y_lat
        h2_lat = modulate(rms_norm(x_lat, norm2_g[l]), m_lat[3], m_lat[4])
        if last:
            f_lat = moe_ffn(h2_lat.reshape(-1, D), router_w, router_bias,
                            moe_w1[l], moe_w3[l], moe_w2[l]).reshape(B, L, D)
        else:
            x_ctx = x_ctx + m_ctx[2] * y_ctx
            h2_ctx = modulate(rms_norm(x_ctx, norm2_g[l]), m_ctx[3], m_ctx[4])
            f_all = moe_ffn(jnp.concatenate([h2_lat.reshape(-1, D), h2_ctx.reshape(-1, D)], axis=0),
                            router_w, router_bias, moe_w1[l], moe_w3[l], moe_w2[l])
            f_lat = f_all[:B * L].reshape(B, L, D)
            x_ctx = x_ctx + m_ctx[5] * f_all[B * L:].reshape(x_ctx.shape)
        x_lat = x_lat + m_lat[5] * f_lat
    return rms_norm(x_lat, final_norm)
```

```python
import functools

import numpy as np
import jax
import jax.numpy as jnp
from jax import lax
from jax.experimental import pallas as pl
from jax.experimental.pallas import tpu as pltpu

F32 = jnp.float32
BF16 = jnp.bfloat16

LANE = 128
SUBLANE = 8
VMEM_LIMIT = 56 * 1024 * 1024

GRID_W = 64
ROPE_BASE = 10000.0
EPS = 1e-6
MLA_HEADS, MLA_NOPE, MLA_ROPE, MLA_V = 8, 64, 32, 64
MLA_Q_LORA, MLA_KV_LORA = 256, 128
RET_HEADS, RET_DK, RET_DV, RET_CHUNK = 4, 64, 128, 128
LRU_WIDTH, LRU_BLOCKS, LRU_C, CONV_W, CONV_PAD_LEFT = 512, 8, 8.0, 4, 2
LRU_BLOCK_W = LRU_WIDTH // LRU_BLOCKS
GQA_HEADS, GQA_KV_HEADS, GQA_HEAD_DIM = 8, 2, 64
N_BRANCH, BRANCH_W = 4, 512
N_EXPERTS, N_GROUPS, TOP_K, EXPERT_FF, MOE_BLOCK = 64, 8, 2, 256, 128
EXPERTS_PER_GROUP = N_EXPERTS // N_GROUPS

IN_SPLITS = (MLA_Q_LORA, MLA_KV_LORA, MLA_ROPE,
             RET_HEADS * RET_DK, RET_HEADS * RET_DK, RET_HEADS * RET_DV, RET_HEADS * RET_DV,
             LRU_WIDTH, LRU_WIDTH,
             GQA_HEADS * GQA_HEAD_DIM, GQA_KV_HEADS * GQA_HEAD_DIM, GQA_KV_HEADS * GQA_HEAD_DIM)
IN_OFF = tuple(int(o) for o in np.cumsum((0,) + IN_SPLITS))
D_IN = IN_OFF[-1]

TM = 256
HEAD_SLOT = LANE
ONE_LANE = 64

ZP_CQ, ZP_CKV, ZP_KR = 0, 256, 384
ZP_RQ, ZP_RK, ZP_RV, ZP_RG = 512, 1024, 1536, 2048
ZP_LU, ZP_LG = 2560, 3072
ZP_GQ, ZP_GK, ZP_GV = 3584, 4608, 4864
ZP_W = 5120


def _cparams(sem):
    return pltpu.CompilerParams(dimension_semantics=sem, vmem_limit_bytes=VMEM_LIMIT)


def _const_spec(shape):
    nd = len(shape)
    return pl.BlockSpec(shape, lambda *_: (0,) * nd, pipeline_mode=pl.Buffered(1))


def _in_proj_columns():
    idx = np.full((ZP_W,), D_IN, np.int64)
    o = IN_OFF
    idx[ZP_CQ:ZP_CQ + 256] = o[0] + np.arange(256)
    idx[ZP_CKV:ZP_CKV + 128] = o[1] + np.arange(128)
    idx[ZP_KR + MLA_NOPE:ZP_KR + MLA_NOPE + MLA_ROPE] = o[2] + np.arange(MLA_ROPE)
    for h in range(RET_HEADS):
        idx[ZP_RQ + h * 128:ZP_RQ + h * 128 + 64] = o[3] + h * 64 + np.arange(64)
        idx[ZP_RK + h * 128:ZP_RK + h * 128 + 64] = o[4] + h * 64 + np.arange(64)
    idx[ZP_RV:ZP_RV + 512] = o[5] + np.arange(512)
    idx[ZP_RG:ZP_RG + 512] = o[6] + np.arange(512)
    idx[ZP_LU:ZP_LU + 512] = o[7] + np.arange(512)
    idx[ZP_LG:ZP_LG + 512] = o[8] + np.arange(512)
    for h in range(GQA_HEADS):
        idx[ZP_GQ + h * 128:ZP_GQ + h * 128 + 64] = o[9] + h * 64 + np.arange(64)
    for h in range(GQA_KV_HEADS):
        idx[ZP_GK + h * 128:ZP_GK + h * 128 + 64] = o[10] + h * 64 + np.arange(64)
        idx[ZP_GV + h * 128:ZP_GV + h * 128 + 64] = o[11] + h * 64 + np.arange(64)
    return idx


def _mla_up_columns():
    dq = MLA_NOPE + MLA_ROPE
    dkv = MLA_NOPE + MLA_V
    qi = np.full((MLA_HEADS * 128,), MLA_HEADS * dq, np.int64)
    ki = np.full((MLA_HEADS * 128,), MLA_HEADS * dkv, np.int64)
    vi = np.full((MLA_HEADS * 128,), MLA_HEADS * dkv, np.int64)
    for h in range(MLA_HEADS):
        qi[h * 128:h * 128 + dq] = h * dq + np.arange(dq)
        ki[h * 128:h * 128 + MLA_NOPE] = h * dkv + np.arange(MLA_NOPE)
        vi[h * 128:h * 128 + MLA_V] = h * dkv + MLA_NOPE + np.arange(MLA_V)
    return qi, np.concatenate([ki, vi])


def _take_cols(w, idx):
    wz = jnp.concatenate([w, jnp.zeros((w.shape[0], 1), w.dtype)], axis=1)
    return jnp.take(wz, jnp.asarray(idx, jnp.int32), axis=1)


def _rope_slot_tables(n_ctx, seq, rot_dim, lane0):
    half = rot_dim // 2
    q = half // 2
    pos = jnp.arange(seq, dtype=jnp.int32)
    rows = (pos // GRID_W).astype(F32)
    cols = (pos % GRID_W).astype(F32)
    inv = ROPE_BASE ** (-jnp.arange(0, half, 2, dtype=F32) / half)
    ar = rows[:, None] * inv
    ac = cols[:, None] * inv
    cos = jnp.concatenate([jnp.cos(ar), jnp.cos(ar), jnp.cos(ac), jnp.cos(ac)], axis=1)
    sin = jnp.concatenate([-jnp.sin(ar), jnp.sin(ar), -jnp.sin(ac), jnp.sin(ac)], axis=1)
    assert cos.shape[1] == rot_dim and q * 4 == rot_dim
    cos_t = jnp.ones((n_ctx + seq, LANE), F32).at[n_ctx:, lane0:lane0 + rot_dim].set(cos)
    sin_t = jnp.zeros((n_ctx + seq, LANE), F32).at[n_ctx:, lane0:lane0 + rot_dim].set(sin)
    return cos_t, sin_t


def _swap_lanes(x, blk):
    n = x.shape[-1]
    lane = lax.broadcasted_iota(jnp.int32, x.shape, x.ndim - 1)
    up = pltpu.roll(x, n - blk, x.ndim - 1)
    dn = pltpu.roll(x, blk, x.ndim - 1)
    return jnp.where((lane % (2 * blk)) < blk, up, dn)


def _rms(x, g):
    return x * lax.rsqrt(jnp.mean(x * x, axis=-1, keepdims=True) + EPS) * g


def _ada_body(c_ref, w_ref, b_ref, o_ref):
    c = c_ref[...]
    s = (c * jax.nn.sigmoid(c)).astype(BF16)
    o_ref[...] = jnp.dot(s, w_ref[...].astype(BF16), preferred_element_type=F32) + b_ref[...]


def _ada_mods(cc, ada_w, ada_b):
    depth, d, n = ada_w.shape
    r = cc.shape[0]
    tn = 1536
    return pl.pallas_call(
        _ada_body,
        grid=(depth, n // tn),
        in_specs=[pl.BlockSpec((r, d), lambda l, j: (0, 0)),
                  pl.BlockSpec((None, d, tn), lambda l, j: (l, 0, j)),
                  pl.BlockSpec((None, 1, tn), lambda l, j: (l, 0, j))],
        out_specs=pl.BlockSpec((None, r, tn), lambda l, j: (l, 0, j)),
        out_shape=jax.ShapeDtypeStruct((depth, r, n), F32),
        compiler_params=_cparams(("arbitrary", "arbitrary")),
        name="ada_mods",
    )(cc, ada_w, ada_b.reshape(depth, 1, n))


def _inproj_body(x_ref, mod_ref, g1_ref, tab_ref, win_ref, wuq_ref, wukv_ref, gq_ref, gkv_ref,
                 mq_ref, mk_ref, mv_ref, rq_ref, rk_ref, rv_ref, rg_ref, lu_ref, lg_ref,
                 gq_out, gk_out, gv_out):
    x = x_ref[...]
    shift = mod_ref[0:1, :]
    scale = mod_ref[1:2, :]
    h = _rms(x, g1_ref[...]) * (1.0 + scale) + shift
    hb = h.astype(BF16)

    def proj(c0, c1):
        return jnp.dot(hb, win_ref[:, c0:c1], preferred_element_type=F32)

    lane = lax.broadcasted_iota(jnp.int32, (x.shape[0], LANE), 1)
    one_col = jnp.where(lane == ONE_LANE, 1.0, 0.0).astype(F32)

    def rope(v, ci, blk):
        return v * tab_ref[ci] + _swap_lanes(v, blk) * tab_ref[ci + 1]

    cq = proj(ZP_CQ, ZP_CQ + 256)
    qn = _rms(cq, gq_ref[...]).astype(BF16)
    q = jnp.dot(qn, wuq_ref[...], preferred_element_type=F32)
    for hh in range(MLA_HEADS):
        sl = slice(hh * 128, (hh + 1) * 128)
        mq_ref[:, sl] = rope(q[:, sl], 0, MLA_ROPE // 4).astype(BF16)
    ckv = proj(ZP_CKV, ZP_CKV + 128)
    kvn = _rms(ckv, gkv_ref[...]).astype(BF16)
    kv = jnp.dot(kvn, wukv_ref[...], preferred_element_type=F32)
    kr = rope(proj(ZP_KR, ZP_KR + 128), 2, MLA_ROPE // 4)
    for hh in range(MLA_HEADS):
        sl = slice(hh * 128, (hh + 1) * 128)
        mk_ref[:, sl] = (kv[:, sl] + kr).astype(BF16)
        mv_ref[:, sl] = (kv[:, MLA_HEADS * 128 + hh * 128:MLA_HEADS * 128 + (hh + 1) * 128] + one_col).astype(BF16)

    rq = proj(ZP_RQ, ZP_RQ + 512)
    rk = proj(ZP_RK, ZP_RK + 512)
    for hh in range(RET_HEADS):
        sl = slice(hh * 128, (hh + 1) * 128)
        rq_ref[:, sl] = rope(rq[:, sl], 4, RET_DK // 4).astype(BF16)
        rk_ref[:, sl] = rope(rk[:, sl], 6, RET_DK // 4).astype(BF16)
    rv_ref[...] = proj(ZP_RV, ZP_RV + 512).astype(BF16)
    rg_ref[...] = proj(ZP_RG, ZP_RG + 512).astype(BF16)

    lu_ref[...] = proj(ZP_LU, ZP_LU + 512).astype(BF16)
    lg_ref[...] = proj(ZP_LG, ZP_LG + 512).astype(BF16)

    gq = proj(ZP_GQ, ZP_GQ + 1024)
    for hh in range(GQA_HEADS):
        sl = slice(hh * 128, (hh + 1) * 128)
        v = gq[:, sl]
        v = v * lax.rsqrt(jnp.sum(v * v, axis=-1, keepdims=True) * (1.0 / GQA_HEAD_DIM) + EPS)
        gq_out[:, sl] = rope(v, 8, GQA_HEAD_DIM // 4).astype(BF16)
    gk = proj(ZP_GK, ZP_GK + 256)
    gv = proj(ZP_GV, ZP_GV + 256)
    for hh in range(GQA_KV_HEADS):
        sl = slice(hh * 128, (hh + 1) * 128)
        v = gk[:, sl]
        v = v * lax.rsqrt(jnp.sum(v * v, axis=-1, keepdims=True) * (1.0 / GQA_HEAD_DIM) + EPS)
        gk_out[:, sl] = rope(v, 10, GQA_HEAD_DIM // 4).astype(BF16)
        gv_out[:, sl] = (gv[:, sl] + one_col).astype(BF16)


def _inproj(x_all, mods, g1, tabs, win_p, wuq_p, wukv_p, gq, gkv):
    b, s, d = x_all.shape
    nt = s // TM
    widths = (1024, 1024, 1024, 512, 512, 512, 512, 512, 512, 1024, 256, 256)
    tile = lambda w: pl.BlockSpec((None, TM, w), lambda i, bb: (bb, i, 0))
    return pl.pallas_call(
        _inproj_body,
        grid=(nt, b),
        in_specs=[tile(d),
                  pl.BlockSpec((None, None, 6, d), lambda i, bb: (bb, jnp.minimum(i, 1), 0, 0)),
                  _const_spec((1, d)),
                  pl.BlockSpec((12, TM, LANE), lambda i, bb: (0, i, 0)),
                  _const_spec(win_p.shape), _const_spec(wuq_p.shape), _const_spec(wukv_p.shape),
                  _const_spec((1, MLA_Q_LORA)), _const_spec((1, MLA_KV_LORA))],
        out_specs=[tile(w) for w in widths],
        out_shape=[jax.ShapeDtypeStruct((b, s, w), BF16) for w in widths],
        compiler_params=_cparams(("arbitrary", "arbitrary")),
        name="in_proj",
    )(x_all, mods, g1, tabs, win_p, wuq_p, wukv_p, gq, gkv)


def _attn_body(q_ref, k_ref, v_ref, o_ref, *, heads, kv_heads, n_ctx, tile0):
    i = pl.program_id(1) + tile0
    grp = heads // kv_heads

    def run(klen):
        for hp in range(heads // 2):
            outs = []
            for h in (2 * hp, 2 * hp + 1):
                g = h // grp
                q = q_ref[:, h * 128:(h + 1) * 128]
                k = k_ref[0:klen, g * 128:(g + 1) * 128]
                v = v_ref[0:klen, g * 128:(g + 1) * 128]
                s = lax.dot_general(q, k, (((1,), (1,)), ((), ())), preferred_element_type=F32)
                m = jnp.max(s, axis=-1, keepdims=True)
                p = jnp.exp(s - m).astype(BF16)
                o = jnp.dot(p, v, preferred_element_type=F32)
                outs.append(o[:, :64] / o[:, ONE_LANE:ONE_LANE + 1])
            o_ref[:, hp * 128:(hp + 1) * 128] = jnp.concatenate(outs, axis=1).astype(o_ref.dtype)

    if tile0 == 0:
        @pl.when(i == 0)
        def _():
            run(n_ctx)

        @pl.when(i > 0)
        def _():
            run(k_ref.shape[0])
    else:
        run(k_ref.shape[0])


def _attention(q, k, v, heads, kv_heads, n_ctx, with_ctx):
    b, s, _ = q.shape
    tile0 = 0 if with_ctx else n_ctx // TM
    nq = s // TM - tile0
    body = functools.partial(_attn_body, heads=heads, kv_heads=kv_heads, n_ctx=n_ctx, tile0=tile0)
    return pl.pallas_call(
        body,
        grid=(b, nq),
        in_specs=[pl.BlockSpec((None, TM, heads * 128), lambda bb, i: (bb, i + tile0, 0)),
                  pl.BlockSpec((None, s, kv_heads * 128), lambda bb, i: (bb, 0, 0)),
                  pl.BlockSpec((None, s, kv_heads * 128), lambda bb, i: (bb, 0, 0))],
        out_specs=pl.BlockSpec((None, TM, heads * 64), lambda bb, i: (bb, i + tile0, 0)),
        out_shape=jax.ShapeDtypeStruct((b, s, heads * 64), BF16),
        compiler_params=_cparams(("arbitrary", "arbitrary")),
        name="attention_h%d_kv%d" % (heads, kv_heads),
    )(q, k, v)


def _retention_body(lg_ref, q_ref, k_ref, v_ref, g_ref, gn_ref, o_ref, kv_scr, st_scr, m_scr, *, n_ctx):
    c = RET_CHUNK
    s = q_ref.shape[0]
    nc = s // c
    nctx = n_ctx // c
    back_order = list(range(nctx - 1, -1, -1)) + list(range(nc - 1, nctx - 1, -1))
    pos = lax.broadcasted_iota(jnp.int32, (c, LANE), 0).astype(F32)
    ri = lax.broadcasted_iota(jnp.int32, (c, c), 0)
    ci = lax.broadcasted_iota(jnp.int32, (c, c), 1)
    diff = (ri - ci).astype(F32)

    for h in range(RET_HEADS):
        sl = slice(h * 128, (h + 1) * 128)
        lgf = lg_ref[0, h]
        lgb = lg_ref[1, h]
        kdf = jnp.exp(lgf * (c - 1.0 - pos))
        kdb = jnp.exp(lgb * pos)
        qdf = jnp.exp(lgf * (pos + 1.0))
        qdb = jnp.exp(lgb * (c - pos))
        m_scr[...] = jnp.where(diff >= 0, jnp.exp(lgf * jnp.maximum(diff, 0.0)),
                               jnp.exp(lgb * jnp.maximum(-diff, 0.0)))

        def kv_step(j, carry):
            r0 = pl.multiple_of(j * c, c)
            kc = k_ref[pl.ds(r0, c), sl].astype(F32)
            vc = v_ref[pl.ds(r0, c), sl]
            kk = jnp.concatenate([(kc * kdf).astype(BF16), (kc * kdb).astype(BF16)], axis=1)
            kv_scr[j] = lax.dot_general(kk, vc, (((0,), (0,)), ((), ())), preferred_element_type=F32)
            return carry

        lax.fori_loop(0, nc, kv_step, 0)

        gcf = jnp.exp(lgf * c)
        gcb = jnp.exp(lgb * c)
        sf = jnp.zeros((128, 128), F32)
        for j in range(nc):
            st_scr[j, 0:128, :] = sf.astype(BF16)
            sf = sf * gcf + kv_scr[j, 0:128, :]
        sb = jnp.zeros((128, 128), F32)
        for j in back_order:
            st_scr[j, 128:256, :] = sb.astype(BF16)
            sb = sb * gcb + kv_scr[j, 128:256, :]

        def out_step(j, carry):
            r0 = pl.multiple_of(j * c, c)
            qb = q_ref[pl.ds(r0, c), sl]
            kb = k_ref[pl.ds(r0, c), sl]
            vc = v_ref[pl.ds(r0, c), sl]
            sc = lax.dot_general(qb, kb, (((1,), (1,)), ((), ())), preferred_element_type=F32) * m_scr[...]
            o = jnp.dot(sc.astype(BF16), vc, preferred_element_type=F32)
            qf = qb.astype(F32)
            qd = jnp.concatenate([(qf * qdf).astype(BF16), (qf * qdb).astype(BF16)], axis=1)
            o = o + jnp.dot(qd, st_scr[j], preferred_element_type=F32)
            mu = jnp.mean(o, axis=-1, keepdims=True)
            oc = o - mu
            var = jnp.mean(oc * oc, axis=-1, keepdims=True)
            y = oc * lax.rsqrt(var + EPS) * gn_ref[:, sl]
            gate = g_ref[pl.ds(r0, c), sl].astype(F32)
            o_ref[pl.ds(r0, c), sl] = (gate * jax.nn.sigmoid(gate) * y).astype(o_ref.dtype)
            return carry

        lax.fori_loop(0, nc, out_step, 0)


def _retention(log_g, q, k, v, g, gn, n_ctx):
    b, s, w = v.shape
    nc = s // RET_CHUNK
    blk = lambda ww: pl.BlockSpec((None, s, ww), lambda bb: (bb, 0, 0))
    return pl.pallas_call(
        functools.partial(_retention_body, n_ctx=n_ctx),
        grid=(b,),
        in_specs=[pl.BlockSpec(memory_space=pltpu.SMEM),
                  blk(512), blk(512), blk(512), blk(512), _const_spec((1, 512))],
        out_specs=blk(512),
        out_shape=jax.ShapeDtypeStruct((b, s, 512), BF16),
        scratch_shapes=[pltpu.VMEM((nc, 256, 128), F32),
                        pltpu.VMEM((nc, 256, 128), BF16),
                        pltpu.VMEM((RET_CHUNK, RET_CHUNK), F32)],
        compiler_params=_cparams(("arbitrary",)),
        name="retention",
    )(log_g, q, k, v, g, gn)


LRU_ROWS = 256
LRU_HALO = 16


def _tile_scan(a, bv, carry, reverse):
    row = lax.broadcasted_iota(jnp.int32, a.shape, 0)
    for dlt in (1, 2, 4):
        if reverse:
            a_s = pltpu.roll(a, SUBLANE - dlt, 0)
            b_s = pltpu.roll(bv, SUBLANE - dlt, 0)
            ok = row < SUBLANE - dlt
        else:
            a_s = pltpu.roll(a, dlt, 0)
            b_s = pltpu.roll(bv, dlt, 0)
            ok = row >= dlt
        bv = jnp.where(ok, a * b_s + bv, bv)
        a = jnp.where(ok, a * a_s, a)
    h = a * carry + bv
    new_carry = h[0:1, :] if reverse else h[SUBLANE - 1:SUBLANE, :]
    return h, new_carry


def _rglru_body(u_ref, g_ref, wc_ref, bc_ref, cw_ref, cb_ref, lam_ref, o_ref, a_scr, b_scr, *, n_ctx):
    s = u_ref.shape[0]
    w = LRU_WIDTH
    r = LRU_ROWS
    cdec = [-LRU_C * jax.nn.softplus(-lam_ref[d:d + 1, :]) for d in range(2)]
    zeros = jnp.zeros((LRU_HALO, w), F32)

    for ch in range(s // r):
        r0 = ch * r
        seg_start = r0 == 0 or r0 == n_ctx
        seg_end = r0 + r == n_ctx or r0 + r == s
        lo = r0 if seg_start else r0 - LRU_HALO
        hi = r0 + r if seg_end else r0 + r + LRU_HALO
        parts = [u_ref[lo:hi, :].astype(F32)]
        if seg_start:
            parts = [zeros] + parts
        if seg_end:
            parts = parts + [zeros]
        ext = jnp.concatenate(parts, axis=0) if len(parts) > 1 else parts[0]
        n = ext.shape[0]
        u = cb_ref[...]
        for j in range(CONV_W):
            sh = (CONV_PAD_LEFT - j) % n
            tap = ext if sh == 0 else pltpu.roll(ext, sh, 0)
            u = u + tap[LRU_HALO:LRU_HALO + r, :] * cw_ref[j:j + 1, :]
        gates = jnp.dot(u.astype(BF16), wc_ref[...], preferred_element_type=F32) + bc_ref[...]
        for d in range(2):
            rg = jax.nn.sigmoid(gates[:, (2 * d) * w:(2 * d + 1) * w])
            ig = jax.nn.sigmoid(gates[:, (2 * d + 1) * w:(2 * d + 2) * w])
            a = jnp.exp(rg * cdec[d])
            a_scr[d, r0:r0 + r, :] = a
            b_scr[d, r0:r0 + r, :] = jnp.sqrt(1.0 - a * a) * (ig * u)

    def fwd(t, carry):
        r0 = pl.multiple_of(t * SUBLANE, SUBLANE)
        h, carry = _tile_scan(a_scr[0, pl.ds(r0, SUBLANE), :], b_scr[0, pl.ds(r0, SUBLANE), :], carry, False)
        b_scr[0, pl.ds(r0, SUBLANE), :] = h
        return carry

    lax.fori_loop(0, s // SUBLANE, fwd, jnp.zeros((1, w), F32), unroll=4)

    def bwd(t0):
        def step(t, carry):
            r0 = pl.multiple_of((t0 - t) * SUBLANE, SUBLANE)
            h, carry = _tile_scan(a_scr[1, pl.ds(r0, SUBLANE), :], b_scr[1, pl.ds(r0, SUBLANE), :], carry, True)
            b_scr[1, pl.ds(r0, SUBLANE), :] = h
            return carry
        return step

    nct = n_ctx // SUBLANE
    carry = lax.fori_loop(0, nct, bwd(nct - 1), jnp.zeros((1, w), F32), unroll=4)
    lax.fori_loop(0, s // SUBLANE - nct, bwd(s // SUBLANE - 1), carry, unroll=4)

    def fin(j, carry):
        r0 = pl.multiple_of(j * r, r)
        hsum = b_scr[0, pl.ds(r0, r), :] + b_scr[1, pl.ds(r0, r), :]
        gate = g_ref[pl.ds(r0, r), :].astype(F32)
        o_ref[pl.ds(r0, r), :] = (hsum * jax.nn.gelu(gate)).astype(o_ref.dtype)
        return carry

    lax.fori_loop(0, s // r, fin, 0)


def _rglru(u, g, wcat, bcat, conv_w, conv_b, lam, n_ctx):
    b, s, w = u.shape
    blk = pl.BlockSpec((None, s, w), lambda bb: (bb, 0, 0))
    return pl.pallas_call(
        functools.partial(_rglru_body, n_ctx=n_ctx),
        grid=(b,),
        in_specs=[blk, blk, _const_spec(wcat.shape), _const_spec(bcat.shape),
                  _const_spec(conv_w.shape), _const_spec(conv_b.shape), _const_spec(lam.shape)],
        out_specs=blk,
        out_shape=jax.ShapeDtypeStruct((b, s, w), BF16),
        scratch_shapes=[pltpu.VMEM((2, s, w), F32), pltpu.VMEM((2, s, w), F32)],
        compiler_params=_cparams(("arbitrary",)),
        name="rglru",
    )(u, g, wcat, bcat, conv_w, conv_b, lam)


def _merge_body(x_ref, mod_ref, g1_ref, g2_ref, ya_ref, yb_ref, yc_ref, yd_ref,
                wm_ref, bm_ref, wb_ref, wo_ref, rw_ref, xo_ref, h2_ref, sc_ref):
    d = x_ref.shape[1]
    x = x_ref[...]
    h = (_rms(x, g1_ref[...]) * (1.0 + mod_ref[1:2, :]) + mod_ref[0:1, :]).astype(BF16)
    acc = jnp.zeros(x.shape, F32)
    for n, y_ref in enumerate((ya_ref, yb_ref, yc_ref, yd_ref)):
        gate = jax.nn.sigmoid(jnp.dot(h, wm_ref[:, n * d:(n + 1) * d], preferred_element_type=F32)
                              + bm_ref[:, n * d:(n + 1) * d])
        acc = acc + gate * jnp.dot(y_ref[...], wb_ref[n], preferred_element_type=F32)
    y = jnp.dot(acc.astype(BF16), wo_ref[...], preferred_element_type=F32)
    xn = x + mod_ref[2:3, :] * y
    xo_ref[...] = xn
    h2 = _rms(xn, g2_ref[...]) * (1.0 + mod_ref[4:5, :]) + mod_ref[3:4, :]
    h2_ref[...] = h2.astype(BF16)
    sc_ref[...] = jax.nn.sigmoid(jnp.dot(h2, rw_ref[...], preferred_element_type=F32,
                                         precision=lax.Precision.HIGHEST))


def _merge(x_all, mods, g1, g2, ys, wm, bm, wb, wo, rw, n_ctx, with_ctx):
    b, s, d = x_all.shape
    tile0 = 0 if with_ctx else n_ctx // TM
    nt = s // TM - tile0
    tile = lambda w: pl.BlockSpec((None, TM, w), lambda bb, i: (bb, i + tile0, 0))
    return pl.pallas_call(
        _merge_body,
        grid=(b, nt),
        in_specs=[tile(d),
                  pl.BlockSpec((None, None, 6, d), lambda bb, i: (bb, jnp.minimum(i + tile0, 1), 0, 0)),
                  _const_spec((1, d)), _const_spec((1, d)),
                  tile(BRANCH_W), tile(BRANCH_W), tile(BRANCH_W), tile(BRANCH_W),
                  _const_spec(wm.shape), _const_spec(bm.shape), _const_spec(wb.shape),
                  _const_spec(wo.shape), _const_spec(rw.shape)],
        out_specs=[tile(d), tile(d), tile(LANE)],
        out_shape=[jax.ShapeDtypeStruct((b, s, d), F32),
                   jax.ShapeDtypeStruct((b, s, d), BF16),
                   jax.ShapeDtypeStruct((b, s, LANE), F32)],
        compiler_params=_cparams(("arbitrary", "arbitrary")),
        name="merge",
    )(x_all, mods, g1, g2, *ys, wm, bm, wb, wo, rw)


def _expert_body(be_ref, nu_ref, x_ref, w1_ref, w3_ref, w2_ref, o_ref):
    j = pl.program_id(0)

    @pl.when(j < nu_ref[0])
    def _():
        x = x_ref[...]
        a = jnp.dot(x, w1_ref[...].astype(BF16), preferred_element_type=F32)
        g = jnp.dot(x, w3_ref[...].astype(BF16), preferred_element_type=F32)
        hmid = (a * jax.nn.sigmoid(a) * g).astype(BF16)
        o_ref[...] = jnp.dot(hmid, w2_ref[...].astype(BF16), preferred_element_type=F32).astype(o_ref.dtype)

    @pl.when(j >= nu_ref[0])
    def _():
        o_ref[...] = jnp.zeros(o_ref.shape, o_ref.dtype)


def _expert_ffn(block_expert, n_used, xb, w1, w3, w2):
    n_rows, d = xb.shape
    nb = n_rows // MOE_BLOCK
    ff = w1.shape[-1]
    grid_spec = pltpu.PrefetchScalarGridSpec(
        num_scalar_prefetch=2,
        grid=(nb,),
        in_specs=[pl.BlockSpec((MOE_BLOCK, d), lambda j, be, nu: (j, 0)),
                  pl.BlockSpec((None, d, ff), lambda j, be, nu: (be[j], 0, 0)),
                  pl.BlockSpec((None, d, ff), lambda j, be, nu: (be[j], 0, 0)),
                  pl.BlockSpec((None, ff, d), lambda j, be, nu: (be[j], 0, 0))],
        out_specs=pl.BlockSpec((MOE_BLOCK, d), lambda j, be, nu: (j, 0)),
    )
    return pl.pallas_call(
        _expert_body,
        grid_spec=grid_spec,
        out_shape=jax.ShapeDtypeStruct((n_rows, d), BF16),
        compiler_params=_cparams(("arbitrary",)),
        name="expert_ffn",
    )(block_expert, n_used, xb, w1, w3, w2)


def _route(scores, router_bias):
    t = scores.shape[0]
    biased = (scores + router_bias.astype(F32)).reshape(t, N_GROUPS, EXPERTS_PER_GROUP)
    group_score = jnp.sum(lax.top_k(biased, 2)[0], axis=-1)
    grp = jnp.argmax(group_score, axis=-1)
    in_grp = jnp.take_along_axis(biased, grp[:, None, None], axis=1)[:, 0]
    _, local = lax.top_k(in_grp, TOP_K)
    expert = (grp[:, None] * EXPERTS_PER_GROUP + local).astype(jnp.int32)
    w = jnp.take_along_axis(scores, expert, axis=1)
    w = w / jnp.sum(w, axis=-1, keepdims=True)

    tk = t * TOP_K
    flat_e = expert.reshape(-1)
    onehot = (flat_e[:, None] == jnp.arange(N_EXPERTS, dtype=jnp.int32)[None, :]).astype(jnp.int32)
    ranks = jnp.cumsum(onehot, axis=0) - onehot
    rank = jnp.take_along_axis(ranks, flat_e[:, None], axis=1)[:, 0]
    counts = jnp.sum(onehot, axis=0)
    padded = (counts + MOE_BLOCK - 1) // MOE_BLOCK * MOE_BLOCK
    pad_end = jnp.cumsum(padded)
    pad_start = pad_end - padded
    slot = (pad_start[flat_e] + rank).astype(jnp.int32)
    n_blocks = -(-tk // MOE_BLOCK) + N_EXPERTS
    flat_tok = jnp.arange(tk, dtype=jnp.int32) // TOP_K
    slot_tok = jnp.full((n_blocks * MOE_BLOCK,), t, jnp.int32).at[slot].set(flat_tok)
    block_expert = jnp.minimum(
        jnp.searchsorted(pad_end, jnp.arange(n_blocks, dtype=jnp.int32) * MOE_BLOCK, side='right'),
        N_EXPERTS - 1).astype(jnp.int32)
    n_used = (pad_end[-1] // MOE_BLOCK).astype(jnp.int32).reshape(1)
    return w, slot.reshape(t, TOP_K), slot_tok, block_expert, n_used


def _resid_body(x_ref, mod_ref, y0_ref, y1_ref, w_ref, gf_ref, o_ref, *, final):
    w = w_ref[...]
    f = w[:, 0:1] * y0_ref[...].astype(F32) + w[:, 1:2] * y1_ref[...].astype(F32)
    xn = x_ref[...] + mod_ref[5:6, :] * f
    if final:
        xn = _rms(xn, gf_ref[...])
    o_ref[...] = xn


def _moe_residual(x_all, mods, y0, y1, w, gf, n_ctx, final):
    b, s, d = x_all.shape
    tile0 = n_ctx // TM if final else 0
    nt = s // TM - tile0
    so = s - tile0 * TM
    tin = lambda ww: pl.BlockSpec((None, TM, ww), lambda bb, i: (bb, i + tile0, 0))
    tsub = lambda ww: pl.BlockSpec((None, TM, ww), lambda bb, i: (bb, i, 0))
    return pl.pallas_call(
        functools.partial(_resid_body, final=final),
        grid=(b, nt),
        in_specs=[tin(d),
                  pl.BlockSpec((None, None, 6, d), lambda bb, i: (bb, jnp.minimum(i + tile0, 1), 0, 0)),
                  tsub(d), tsub(d), tsub(TOP_K), _const_spec((1, d))],
        out_specs=tsub(d),
        out_shape=jax.ShapeDtypeStruct((b, so, d), F32),
        compiler_params=_cparams(("arbitrary", "arbitrary")),
        name="moe_residual",
    )(x_all, mods, y0, y1, w, gf)


def kernel(x, c, ctx, c_ctx, ada_w, ada_b, norm1_g, norm2_g, w_in, mla_q_norm, mla_kv_norm, mla_w_uq,
           mla_w_ukv, ret_decay, ret_norm, lru_conv_w, lru_conv_b, lru_w_a, lru_b_a, lru_w_x, lru_b_x,
           lru_lambda, gqa_q_norm, gqa_k_norm, w_branch, w_merge, b_merge, w_out, router_w, router_bias,
           moe_w1, moe_w3, moe_w2, final_norm):
    b, seq, d = x.shape
    n_ctx = ctx.shape[1]
    depth = ada_w.shape[0]
    s = n_ctx + seq
    assert n_ctx == TM and seq % TM == 0 and seq % GRID_W == 0

    r_pad = -(-(b + 1) // SUBLANE) * SUBLANE
    cc = jnp.zeros((r_pad, d), F32).at[:b].set(c).at[b].set(c_ctx)
    mods_all = _ada_mods(cc, ada_w, ada_b)

    mla_cos, mla_sin = _rope_slot_tables(n_ctx, seq, MLA_ROPE, MLA_NOPE)
    hd_cos, hd_sin = _rope_slot_tables(n_ctx, seq, GQA_HEAD_DIM, 0)

    in_cols = _in_proj_columns()
    uq_cols, ukv_cols = _mla_up_columns()
    rw = jnp.concatenate([router_w, jnp.zeros((d, LANE - N_EXPERTS), F32)], axis=1)

    x_all = jnp.concatenate([ctx, x], axis=1)
    out = None
    for l in range(depth):
        last = l == depth - 1
        m = mods_all[l].reshape(r_pad, 6, d)
        mods = jnp.stack([jnp.broadcast_to(m[b], (b, 6, d)), m[:b]], axis=1)

        mla_scale = (MLA_NOPE + MLA_ROPE) ** -0.5
        gqa_scale = GQA_HEAD_DIM ** -0.5

        def swap_gain(g):
            g4 = g.reshape(2, 2, GQA_HEAD_DIM // 4)
            return g4[:, ::-1, :].reshape(GQA_HEAD_DIM)

        def gain_slot(g):
            return jnp.concatenate([g, jnp.zeros((LANE - GQA_HEAD_DIM,), F32)])

        gqc, gqs = gain_slot(gqa_q_norm[l]), gain_slot(swap_gain(gqa_q_norm[l]))
        gkc, gks = gain_slot(gqa_k_norm[l]), gain_slot(swap_gain(gqa_k_norm[l]))
        tabs = jnp.stack([
            mla_cos * mla_scale, mla_sin * mla_scale, mla_cos, mla_sin,
            hd_cos, hd_sin, hd_cos * RET_DK ** -0.5, hd_sin * RET_DK ** -0.5,
            hd_cos * gqc * gqa_scale, hd_sin * gqs * gqa_scale, hd_cos * gkc, hd_sin * gks])

        win_p = _take_cols(w_in[l], in_cols).astype(BF16)
        wuq_p = _take_cols(mla_w_uq[l], uq_cols).astype(BF16)
        wukv_p = _take_cols(mla_w_ukv[l], ukv_cols).astype(BF16)

        (mq, mk, mv, rq, rk, rv, rg, lu, lg, gq, gk, gv) = _inproj(
            x_all, mods, norm1_g[l][None], tabs, win_p, wuq_p, wukv_p,
            mla_q_norm[l][None], mla_kv_norm[l][None])

        ya = _attention(mq, mk, mv, MLA_HEADS, MLA_HEADS, n_ctx, not last)
        yd = _attention(gq, gk, gv, GQA_HEADS, GQA_KV_HEADS, n_ctx, not last)

        log_g = -jax.nn.softplus(-ret_decay[l].astype(F32))
        yb = _retention(log_g, rq, rk, rv, rg, ret_norm[l][None], n_ctx)

        eye = jnp.eye(LRU_BLOCKS, dtype=F32)

        def block_diag(wblk):
            return jnp.einsum('ncd,nm->ncmd', wblk, eye).reshape(LRU_WIDTH, LRU_WIDTH)

        wcat = jnp.concatenate([block_diag(lru_w_a[l, 0]), block_diag(lru_w_x[l, 0]),
                                block_diag(lru_w_a[l, 1]), block_diag(lru_w_x[l, 1])], axis=1).astype(BF16)
        bcat = jnp.concatenate([lru_b_a[l, 0], lru_b_x[l, 0], lru_b_a[l, 1], lru_b_x[l, 1]])[None]
        yc = _rglru(lu, lg, wcat, bcat, lru_conv_w[l], lru_conv_b[l][None], lru_lambda[l], n_ctx)

        x_all, h2, scores = _merge(
            x_all, mods, norm1_g[l][None], norm2_g[l][None], (ya, yb, yc, yd),
            w_merge[l].astype(BF16), b_merge[l][None], w_branch[l].astype(BF16), w_out[l].astype(BF16),
            rw, n_ctx, not last)

        r0 = n_ctx if last else 0
        h2f = h2[:, r0:].reshape(-1, d)
        scf = scores[:, r0:, :N_EXPERTS].reshape(-1, N_EXPERTS)
        t = h2f.shape[0]
        wts, slot, slot_tok, block_expert, n_used = _route(scf, router_bias)
        h_pad = jnp.concatenate([h2f, jnp.zeros((1, d), h2f.dtype)], axis=0)
        xb = jnp.take(h_pad, slot_tok, axis=0)
        yblk = _expert_ffn(block_expert, n_used, xb, moe_w1[l], moe_w3[l], moe_w2[l])
        y0 = jnp.take(yblk, slot[:, 0], axis=0).reshape(b, s - r0, d)
        y1 = jnp.take(yblk, slot[:, 1], axis=0).reshape(b, s - r0, d)
        out = _moe_residual(x_all, mods, y0, y1, wts.reshape(b, s - r0, TOP_K), final_norm[None], n_ctx, last)
        x_all = out
    return out
```

```python
import functools

import numpy as np
import jax
import jax.numpy as jnp
from jax import lax
from jax.experimental import pallas as pl
from jax.experimental.pallas import tpu as pltpu

F32 = jnp.float32
BF16 = jnp.bfloat16

LANE = 128
SUBLANE = 8
VMEM_LIMIT = 56 * 1024 * 1024

GRID_W = 64
ROPE_BASE = 10000.0
EPS = 1e-6
MLA_HEADS, MLA_NOPE, MLA_ROPE, MLA_V = 8, 64, 32, 64
MLA_Q_LORA, MLA_KV_LORA = 256, 128
RET_HEADS, RET_DK, RET_DV, RET_CHUNK = 4, 64, 128, 128
LRU_WIDTH, LRU_BLOCKS, LRU_C, CONV_W, CONV_PAD_LEFT = 512, 8, 8.0, 4, 2
LRU_BLOCK_W = LRU_WIDTH // LRU_BLOCKS
GQA_HEADS, GQA_KV_HEADS, GQA_HEAD_DIM = 8, 2, 64
N_BRANCH, BRANCH_W = 4, 512
N_EXPERTS, N_GROUPS, TOP_K, EXPERT_FF, MOE_BLOCK = 64, 8, 2, 256, 128
EXPERTS_PER_GROUP = N_EXPERTS // N_GROUPS

IN_SPLITS = (MLA_Q_LORA, MLA_KV_LORA, MLA_ROPE,
             RET_HEADS * RET_DK, RET_HEADS * RET_DK, RET_HEADS * RET_DV, RET_HEADS * RET_DV,
             LRU_WIDTH, LRU_WIDTH,
             GQA_HEADS * GQA_HEAD_DIM, GQA_KV_HEADS * GQA_HEAD_DIM, GQA_KV_HEADS * GQA_HEAD_DIM)
IN_OFF = tuple(int(o) for o in np.cumsum((0,) + IN_SPLITS))
D_IN = IN_OFF[-1]

TM = 256
EXP_ROWS = 256
HEAD_SLOT = LANE
ONE_LANE = 64

ZP_CQ, ZP_CKV, ZP_KR = 0, 256, 384
ZP_RQ, ZP_RK, ZP_RV, ZP_RG = 512, 1024, 1536, 2048
ZP_LU, ZP_LG = 2560, 3072
ZP_GQ, ZP_GK, ZP_GV = 3584, 4608, 4864
ZP_W = 5120


def _cparams(sem):
    return pltpu.CompilerParams(dimension_semantics=sem, vmem_limit_bytes=VMEM_LIMIT)


def _const_spec(shape):
    nd = len(shape)
    return pl.BlockSpec(shape, lambda *_: (0,) * nd, pipeline_mode=pl.Buffered(1))


def _in_proj_columns():
    idx = np.full((ZP_W,), D_IN, np.int64)
    o = IN_OFF
    idx[ZP_CQ:ZP_CQ + 256] = o[0] + np.arange(256)
    idx[ZP_CKV:ZP_CKV + 128] = o[1] + np.arange(128)
    idx[ZP_KR + MLA_NOPE:ZP_KR + MLA_NOPE + MLA_ROPE] = o[2] + np.arange(MLA_ROPE)
    for h in range(RET_HEADS):
        idx[ZP_RQ + h * 128:ZP_RQ + h * 128 + 64] = o[3] + h * 64 + np.arange(64)
        idx[ZP_RK + h * 128:ZP_RK + h * 128 + 64] = o[4] + h * 64 + np.arange(64)
    idx[ZP_RV:ZP_RV + 512] = o[5] + np.arange(512)
    idx[ZP_RG:ZP_RG + 512] = o[6] + np.arange(512)
    idx[ZP_LU:ZP_LU + 512] = o[7] + np.arange(512)
    idx[ZP_LG:ZP_LG + 512] = o[8] + np.arange(512)
    for h in range(GQA_HEADS):
        idx[ZP_GQ + h * 128:ZP_GQ + h * 128 + 64] = o[9] + h * 64 + np.arange(64)
    for h in range(GQA_KV_HEADS):
        idx[ZP_GK + h * 128:ZP_GK + h * 128 + 64] = o[10] + h * 64 + np.arange(64)
        idx[ZP_GV + h * 128:ZP_GV + h * 128 + 64] = o[11] + h * 64 + np.arange(64)
    return idx


def _mla_up_columns():
    dq = MLA_NOPE + MLA_ROPE
    dkv = MLA_NOPE + MLA_V
    qi = np.full((MLA_HEADS * 128,), MLA_HEADS * dq, np.int64)
    ki = np.full((MLA_HEADS * 128,), MLA_HEADS * dkv, np.int64)
    vi = np.full((MLA_HEADS * 128,), MLA_HEADS * dkv, np.int64)
    for h in range(MLA_HEADS):
        qi[h * 128:h * 128 + dq] = h * dq + np.arange(dq)
        ki[h * 128:h * 128 + MLA_NOPE] = h * dkv + np.arange(MLA_NOPE)
        vi[h * 128:h * 128 + MLA_V] = h * dkv + MLA_NOPE + np.arange(MLA_V)
    return qi, np.concatenate([ki, vi])


def _take_cols(w, idx):
    wz = jnp.concatenate([w, jnp.zeros((w.shape[0], 1), w.dtype)], axis=1)
    return jnp.take(wz, jnp.asarray(idx, jnp.int32), axis=1)


def _rope_slot_tables(n_ctx, seq, rot_dim, lane0):
    half = rot_dim // 2
    q = half // 2
    pos = jnp.arange(seq, dtype=jnp.int32)
    rows = (pos // GRID_W).astype(F32)
    cols = (pos % GRID_W).astype(F32)
    inv = ROPE_BASE ** (-jnp.arange(0, half, 2, dtype=F32) / half)
    ar = rows[:, None] * inv
    ac = cols[:, None] * inv
    cos = jnp.concatenate([jnp.cos(ar), jnp.cos(ar), jnp.cos(ac), jnp.cos(ac)], axis=1)
    sin = jnp.concatenate([-jnp.sin(ar), jnp.sin(ar), -jnp.sin(ac), jnp.sin(ac)], axis=1)
    assert cos.shape[1] == rot_dim and q * 4 == rot_dim
    cos_t = jnp.ones((n_ctx + seq, LANE), F32).at[n_ctx:, lane0:lane0 + rot_dim].set(cos)
    sin_t = jnp.zeros((n_ctx + seq, LANE), F32).at[n_ctx:, lane0:lane0 + rot_dim].set(sin)
    return cos_t, sin_t


def _swap_lanes(x, blk):
    n = x.shape[-1]
    lane = lax.broadcasted_iota(jnp.int32, x.shape, x.ndim - 1)
    up = pltpu.roll(x, n - blk, x.ndim - 1)
    dn = pltpu.roll(x, blk, x.ndim - 1)
    return jnp.where((lane % (2 * blk)) < blk, up, dn)


def _rms(x, g):
    return x * lax.rsqrt(jnp.mean(x * x, axis=-1, keepdims=True) + EPS) * g


def _ada_body(c_ref, w_ref, b_ref, o_ref):
    c = c_ref[...]
    s = (c * jax.nn.sigmoid(c)).astype(BF16)
    o_ref[...] = jnp.dot(s, w_ref[...].astype(BF16), preferred_element_type=F32) + b_ref[...]


def _ada_mods(cc, ada_w, ada_b):
    depth, d, n = ada_w.shape
    r = cc.shape[0]
    tn = 1536
    return pl.pallas_call(
        _ada_body,
        grid=(depth, n // tn),
        in_specs=[pl.BlockSpec((r, d), lambda l, j: (0, 0)),
                  pl.BlockSpec((None, d, tn), lambda l, j: (l, 0, j)),
                  pl.BlockSpec((None, 1, tn), lambda l, j: (l, 0, j))],
        out_specs=pl.BlockSpec((None, r, tn), lambda l, j: (l, 0, j)),
        out_shape=jax.ShapeDtypeStruct((depth, r, n), F32),
        compiler_params=_cparams(("arbitrary", "arbitrary")),
        name="ada_mods",
    )(cc, ada_w, ada_b.reshape(depth, 1, n))


def _inproj_body(x_ref, mod_ref, g1_ref, tab_ref, win_ref, wuq_ref, wukv_ref, gq_ref, gkv_ref,
                 mq_ref, mk_ref, mv_ref, rq_ref, rk_ref, rv_ref, rg_ref, lu_ref, lg_ref,
                 gq_out, gk_out, gv_out):
    x = x_ref[...]
    shift = mod_ref[0:1, :]
    scale = mod_ref[1:2, :]
    h = _rms(x, g1_ref[...]) * (1.0 + scale) + shift
    hb = h.astype(BF16)

    def proj(c0, c1):
        return jnp.dot(hb, win_ref[:, c0:c1], preferred_element_type=F32)

    lane = lax.broadcasted_iota(jnp.int32, (x.shape[0], LANE), 1)
    one_col = jnp.where(lane == ONE_LANE, 1.0, 0.0).astype(F32)

    def rope(v, ci, blk):
        return v * tab_ref[ci] + _swap_lanes(v, blk) * tab_ref[ci + 1]

    cq = proj(ZP_CQ, ZP_CQ + 256)
    qn = _rms(cq, gq_ref[...]).astype(BF16)
    q = jnp.dot(qn, wuq_ref[...], preferred_element_type=F32)
    for hh in range(MLA_HEADS):
        sl = slice(hh * 128, (hh + 1) * 128)
        mq_ref[:, sl] = rope(q[:, sl], 0, MLA_ROPE // 4).astype(BF16)
    ckv = proj(ZP_CKV, ZP_CKV + 128)
    kvn = _rms(ckv, gkv_ref[...]).astype(BF16)
    kv = jnp.dot(kvn, wukv_ref[...], preferred_element_type=F32)
    kr = rope(proj(ZP_KR, ZP_KR + 128), 2, MLA_ROPE // 4)
    for hh in range(MLA_HEADS):
        sl = slice(hh * 128, (hh + 1) * 128)
        mk_ref[:, sl] = (kv[:, sl] + kr).astype(BF16)
        mv_ref[:, sl] = (kv[:, MLA_HEADS * 128 + hh * 128:MLA_HEADS * 128 + (hh + 1) * 128] + one_col).astype(BF16)

    rq = proj(ZP_RQ, ZP_RQ + 512)
    rk = proj(ZP_RK, ZP_RK + 512)
    for hh in range(RET_HEADS):
        sl = slice(hh * 128, (hh + 1) * 128)
        rq_ref[:, sl] = rope(rq[:, sl], 4, RET_DK // 4).astype(BF16)
        rk_ref[:, sl] = rope(rk[:, sl], 6, RET_DK // 4).astype(BF16)
    rv_ref[...] = proj(ZP_RV, ZP_RV + 512).astype(BF16)
    rg_ref[...] = proj(ZP_RG, ZP_RG + 512).astype(BF16)

    lu_ref[...] = proj(ZP_LU, ZP_LU + 512).astype(BF16)
    lg_ref[...] = proj(ZP_LG, ZP_LG + 512).astype(BF16)

    gq = proj(ZP_GQ, ZP_GQ + 1024)
    for hh in range(GQA_HEADS):
        sl = slice(hh * 128, (hh + 1) * 128)
        v = gq[:, sl]
        v = v * lax.rsqrt(jnp.sum(v * v, axis=-1, keepdims=True) * (1.0 / GQA_HEAD_DIM) + EPS)
        gq_out[:, sl] = rope(v, 8, GQA_HEAD_DIM // 4).astype(BF16)
    gk = proj(ZP_GK, ZP_GK + 256)
    gv = proj(ZP_GV, ZP_GV + 256)
    for hh in range(GQA_KV_HEADS):
        sl = slice(hh * 128, (hh + 1) * 128)
        v = gk[:, sl]
        v = v * lax.rsqrt(jnp.sum(v * v, axis=-1, keepdims=True) * (1.0 / GQA_HEAD_DIM) + EPS)
        gk_out[:, sl] = rope(v, 10, GQA_HEAD_DIM // 4).astype(BF16)
        gv_out[:, sl] = (gv[:, sl] + one_col).astype(BF16)


def _inproj(x_all, mods, g1, tabs, win_p, wuq_p, wukv_p, gq, gkv):
    b, s, d = x_all.shape
    nt = s // TM
    widths = (1024, 1024, 1024, 512, 512, 512, 512, 512, 512, 1024, 256, 256)
    tile = lambda w: pl.BlockSpec((None, TM, w), lambda i, bb: (bb, i, 0))
    return pl.pallas_call(
        _inproj_body,
        grid=(nt, b),
        in_specs=[tile(d),
                  pl.BlockSpec((None, None, 6, d), lambda i, bb: (bb, jnp.minimum(i, 1), 0, 0)),
                  _const_spec((1, d)),
                  pl.BlockSpec((12, TM, LANE), lambda i, bb: (0, i, 0)),
                  _const_spec(win_p.shape), _const_spec(wuq_p.shape), _const_spec(wukv_p.shape),
                  _const_spec((1, MLA_Q_LORA)), _const_spec((1, MLA_KV_LORA))],
        out_specs=[tile(w) for w in widths],
        out_shape=[jax.ShapeDtypeStruct((b, s, w), BF16) for w in widths],
        compiler_params=_cparams(("arbitrary", "arbitrary")),
        name="in_proj",
    )(x_all, mods, g1, tabs, win_p, wuq_p, wukv_p, gq, gkv)


def _attn_body(q_ref, k_ref, v_ref, o_ref, *, heads, kv_heads, n_ctx, tile0):
    i = pl.program_id(1) + tile0
    grp = heads // kv_heads

    def run(klen):
        for hp in range(heads // 2):
            outs = []
            for h in (2 * hp, 2 * hp + 1):
                g = h // grp
                q = q_ref[:, h * 128:(h + 1) * 128]
                k = k_ref[0:klen, g * 128:(g + 1) * 128]
                v = v_ref[0:klen, g * 128:(g + 1) * 128]
                s = lax.dot_general(q, k, (((1,), (1,)), ((), ())), preferred_element_type=F32)
                m = jnp.max(s, axis=-1, keepdims=True)
                p = jnp.exp(s - m).astype(BF16)
                o = jnp.dot(p, v, preferred_element_type=F32)
                outs.append(o[:, :64] / o[:, ONE_LANE:ONE_LANE + 1])
            o_ref[:, hp * 128:(hp + 1) * 128] = jnp.concatenate(outs, axis=1).astype(o_ref.dtype)

    if tile0 == 0:
        @pl.when(i == 0)
        def _():
            run(n_ctx)

        @pl.when(i > 0)
        def _():
            run(k_ref.shape[0])
    else:
        run(k_ref.shape[0])


def _attention(q, k, v, heads, kv_heads, n_ctx, with_ctx):
    b, s, _ = q.shape
    tile0 = 0 if with_ctx else n_ctx // TM
    nq = s // TM - tile0
    body = functools.partial(_attn_body, heads=heads, kv_heads=kv_heads, n_ctx=n_ctx, tile0=tile0)
    return pl.pallas_call(
        body,
        grid=(b, nq),
        in_specs=[pl.BlockSpec((None, TM, heads * 128), lambda bb, i: (bb, i + tile0, 0)),
                  pl.BlockSpec((None, s, kv_heads * 128), lambda bb, i: (bb, 0, 0)),
                  pl.BlockSpec((None, s, kv_heads * 128), lambda bb, i: (bb, 0, 0))],
        out_specs=pl.BlockSpec((None, TM, heads * 64), lambda bb, i: (bb, i, 0)),
        out_shape=jax.ShapeDtypeStruct((b, nq * TM, heads * 64), BF16),
        compiler_params=_cparams(("arbitrary", "arbitrary")),
        name="attention_h%d_kv%d" % (heads, kv_heads),
    )(q, k, v)


def _retention_body(lg_ref, q_ref, k_ref, v_ref, g_ref, gn_ref, o_ref, kv_scr, st_scr, m_scr, *, n_ctx):
    c = RET_CHUNK
    s = q_ref.shape[0]
    nc = s // c
    nctx = n_ctx // c
    back_order = list(range(nctx - 1, -1, -1)) + list(range(nc - 1, nctx - 1, -1))
    pos = lax.broadcasted_iota(jnp.int32, (c, LANE), 0).astype(F32)
    ri = lax.broadcasted_iota(jnp.int32, (c, c), 0)
    ci = lax.broadcasted_iota(jnp.int32, (c, c), 1)
    diff = (ri - ci).astype(F32)

    for h in range(RET_HEADS):
        sl = slice(h * 128, (h + 1) * 128)
        lgf = lg_ref[0, h]
        lgb = lg_ref[1, h]
        kdf = jnp.exp(lgf * (c - 1.0 - pos))
        kdb = jnp.exp(lgb * pos)
        qdf = jnp.exp(lgf * (pos + 1.0))
        qdb = jnp.exp(lgb * (c - pos))
        m_scr[...] = jnp.where(diff >= 0, jnp.exp(lgf * jnp.maximum(diff, 0.0)),
                               jnp.exp(lgb * jnp.maximum(-diff, 0.0)))

        def kv_step(j, carry):
            r0 = pl.multiple_of(j * c, c)
            kc = k_ref[pl.ds(r0, c), sl].astype(F32)
            vc = v_ref[pl.ds(r0, c), sl]
            kk = jnp.concatenate([(kc * kdf).astype(BF16), (kc * kdb).astype(BF16)], axis=1)
            kv_scr[j] = lax.dot_general(kk, vc, (((0,), (0,)), ((), ())), preferred_element_type=F32)
            return carry

        lax.fori_loop(0, nc, kv_step, 0)

        gcf = jnp.exp(lgf * c)
        gcb = jnp.exp(lgb * c)
        sf = jnp.zeros((128, 128), F32)
        for j in range(nc):
            st_scr[j, 0:128, :] = sf.astype(BF16)
            sf = sf * gcf + kv_scr[j, 0:128, :]
        sb = jnp.zeros((128, 128), F32)
        for j in back_order:
            st_scr[j, 128:256, :] = sb.astype(BF16)
            sb = sb * gcb + kv_scr[j, 128:256, :]

        def out_step(j, carry):
            r0 = pl.multiple_of(j * c, c)
            qb = q_ref[pl.ds(r0, c), sl]
            kb = k_ref[pl.ds(r0, c), sl]
            vc = v_ref[pl.ds(r0, c), sl]
            sc = lax.dot_general(qb, kb, (((1,), (1,)), ((), ())), preferred_element_type=F32) * m_scr[...]
            o = jnp.dot(sc.astype(BF16), vc, preferred_element_type=F32)
            qf = qb.astype(F32)
            qd = jnp.concatenate([(qf * qdf).astype(BF16), (qf * qdb).astype(BF16)], axis=1)
            o = o + jnp.dot(qd, st_scr[j], preferred_element_type=F32)
            mu = jnp.mean(o, axis=-1, keepdims=True)
            oc = o - mu
            var = jnp.mean(oc * oc, axis=-1, keepdims=True)
            y = oc * lax.rsqrt(var + EPS) * gn_ref[:, sl]
            gate = g_ref[pl.ds(r0, c), sl].astype(F32)
            o_ref[pl.ds(r0, c), sl] = (gate * jax.nn.sigmoid(gate) * y).astype(o_ref.dtype)
            return carry

        lax.fori_loop(0, nc, out_step, 0)


def _retention(log_g, q, k, v, g, gn, n_ctx):
    b, s, w = v.shape
    nc = s // RET_CHUNK
    blk = lambda ww: pl.BlockSpec((None, s, ww), lambda bb: (bb, 0, 0))
    return pl.pallas_call(
        functools.partial(_retention_body, n_ctx=n_ctx),
        grid=(b,),
        in_specs=[pl.BlockSpec(memory_space=pltpu.SMEM),
                  blk(512), blk(512), blk(512), blk(512), _const_spec((1, 512))],
        out_specs=blk(512),
        out_shape=jax.ShapeDtypeStruct((b, s, 512), BF16),
        scratch_shapes=[pltpu.VMEM((nc, 256, 128), F32),
                        pltpu.VMEM((nc, 256, 128), BF16),
                        pltpu.VMEM((RET_CHUNK, RET_CHUNK), F32)],
        compiler_params=_cparams(("arbitrary",)),
        name="retention",
    )(log_g, q, k, v, g, gn)


LRU_ROWS = 256
LRU_HALO = 16


def _tile_scan(a, bv, carry, reverse):
    row = lax.broadcasted_iota(jnp.int32, a.shape, 0)
    for dlt in (1, 2, 4):
        if reverse:
            a_s = pltpu.roll(a, SUBLANE - dlt, 0)
            b_s = pltpu.roll(bv, SUBLANE - dlt, 0)
            ok = row < SUBLANE - dlt
        else:
            a_s = pltpu.roll(a, dlt, 0)
            b_s = pltpu.roll(bv, dlt, 0)
            ok = row >= dlt
        bv = jnp.where(ok, a * b_s + bv, bv)
        a = jnp.where(ok, a * a_s, a)
    h = a * carry + bv
    new_carry = h[0:1, :] if reverse else h[SUBLANE - 1:SUBLANE, :]
    return h, new_carry


def _rglru_body(u_ref, g_ref, wc_ref, bc_ref, cw_ref, cb_ref, lam_ref, o_ref, a_scr, b_scr, *, n_ctx):
    s = u_ref.shape[0]
    w = LRU_WIDTH
    r = LRU_ROWS
    cdec = [-LRU_C * jax.nn.softplus(-lam_ref[d:d + 1, :]) for d in range(2)]
    zeros = jnp.zeros((LRU_HALO, w), F32)

    for ch in range(s // r):
        r0 = ch * r
        seg_start = r0 == 0 or r0 == n_ctx
        seg_end = r0 + r == n_ctx or r0 + r == s
        lo = r0 if seg_start else r0 - LRU_HALO
        hi = r0 + r if seg_end else r0 + r + LRU_HALO
        parts = [u_ref[lo:hi, :].astype(F32)]
        if seg_start:
            parts = [zeros] + parts
        if seg_end:
            parts = parts + [zeros]
        ext = jnp.concatenate(parts, axis=0) if len(parts) > 1 else parts[0]
        n = ext.shape[0]
        u = cb_ref[...]
        for j in range(CONV_W):
            sh = (CONV_PAD_LEFT - j) % n
            tap = ext if sh == 0 else pltpu.roll(ext, sh, 0)
            u = u + tap[LRU_HALO:LRU_HALO + r, :] * cw_ref[j:j + 1, :]
        gates = jnp.dot(u.astype(BF16), wc_ref[...], preferred_element_type=F32) + bc_ref[...]
        for d in range(2):
            rg = jax.nn.sigmoid(gates[:, (2 * d) * w:(2 * d + 1) * w])
            ig = jax.nn.sigmoid(gates[:, (2 * d + 1) * w:(2 * d + 2) * w])
            a = jnp.exp(rg * cdec[d])
            a_scr[d, r0:r0 + r, :] = a
            b_scr[d, r0:r0 + r, :] = jnp.sqrt(1.0 - a * a) * (ig * u)

    def fwd(t, carry):
        r0 = pl.multiple_of(t * SUBLANE, SUBLANE)
        h, carry = _tile_scan(a_scr[0, pl.ds(r0, SUBLANE), :], b_scr[0, pl.ds(r0, SUBLANE), :], carry, False)
        b_scr[0, pl.ds(r0, SUBLANE), :] = h
        return carry

    lax.fori_loop(0, s // SUBLANE, fwd, jnp.zeros((1, w), F32), unroll=4)

    def bwd(t0):
        def step(t, carry):
            r0 = pl.multiple_of((t0 - t) * SUBLANE, SUBLANE)
            h, carry = _tile_scan(a_scr[1, pl.ds(r0, SUBLANE), :], b_scr[1, pl.ds(r0, SUBLANE), :], carry, True)
            b_scr[1, pl.ds(r0, SUBLANE), :] = h
            return carry
        return step

    nct = n_ctx // SUBLANE
    carry = lax.fori_loop(0, nct, bwd(nct - 1), jnp.zeros((1, w), F32), unroll=4)
    lax.fori_loop(0, s // SUBLANE - nct, bwd(s // SUBLANE - 1), carry, unroll=4)

    def fin(j, carry):
        r0 = pl.multiple_of(j * r, r)
        hsum = b_scr[0, pl.ds(r0, r), :] + b_scr[1, pl.ds(r0, r), :]
        gate = g_ref[pl.ds(r0, r), :].astype(F32)
        o_ref[pl.ds(r0, r), :] = (hsum * jax.nn.gelu(gate)).astype(o_ref.dtype)
        return carry

    lax.fori_loop(0, s // r, fin, 0)


def _rglru(u, g, wcat, bcat, conv_w, conv_b, lam, n_ctx):
    b, s, w = u.shape
    blk = pl.BlockSpec((None, s, w), lambda bb: (bb, 0, 0))
    return pl.pallas_call(
        functools.partial(_rglru_body, n_ctx=n_ctx),
        grid=(b,),
        in_specs=[blk, blk, _const_spec(wcat.shape), _const_spec(bcat.shape),
                  _const_spec(conv_w.shape), _const_spec(conv_b.shape), _const_spec(lam.shape)],
        out_specs=blk,
        out_shape=jax.ShapeDtypeStruct((b, s, w), BF16),
        scratch_shapes=[pltpu.VMEM((2, s, w), F32), pltpu.VMEM((2, s, w), F32)],
        compiler_params=_cparams(("arbitrary",)),
        name="rglru",
    )(u, g, wcat, bcat, conv_w, conv_b, lam)


def _group_all(v, op):
    blk = 1
    while blk < EXPERTS_PER_GROUP:
        v = op(v, _swap_lanes(v, blk))
        blk *= 2
    return v


def _route_tile(scores, bias):
    neg = jnp.float32(-3.0e38)
    far = jnp.float32(1.0e9)
    lane = lax.broadcasted_iota(jnp.int32, scores.shape, 1)
    lane_f = lane.astype(F32)
    valid = lane < N_EXPERTS
    x = jnp.where(valid, scores + bias, neg)
    m1 = _group_all(x, jnp.maximum)
    i1 = _group_all(jnp.where(x == m1, lane_f, far), jnp.minimum)
    x2 = jnp.where(lane_f == i1, neg, x)
    m2 = _group_all(x2, jnp.maximum)
    i2 = _group_all(jnp.where(x2 == m2, lane_f, far), jnp.minimum)
    gs = jnp.where(valid, m1 + m2, neg)
    gmax = jnp.max(gs, axis=-1, keepdims=True)
    g0 = jnp.min(jnp.where(gs == gmax, lane_f, far), axis=-1, keepdims=True)
    in_best = jnp.where(lane_f >= g0, jnp.where(lane_f < g0 + EXPERTS_PER_GROUP, 1.0, 0.0), 0.0)
    sel1 = jnp.where(lane_f == i1, in_best, 0.0)
    sel2 = jnp.where(lane_f == i2, in_best, 0.0)
    both = sel1 + sel2
    wsum = jnp.sum(both * scores, axis=-1, keepdims=True)
    wmat = both * scores / wsum
    return wmat + pltpu.roll(sel1 + 2.0 * sel2, N_EXPERTS, 1)


def _merge_body(x_ref, mod_ref, g1_ref, g2_ref, ya_ref, yb_ref, yc_ref, yd_ref,
                wm_ref, bm_ref, wb_ref, wo_ref, rw_ref, rb_ref, xo_ref, h2_ref, sc_ref):
    d = x_ref.shape[1]
    x = x_ref[...]
    h = (_rms(x, g1_ref[...]) * (1.0 + mod_ref[1:2, :]) + mod_ref[0:1, :]).astype(BF16)
    acc = jnp.zeros(x.shape, F32)
    for n, y_ref in enumerate((ya_ref, yb_ref, yc_ref, yd_ref)):
        gate = jax.nn.sigmoid(jnp.dot(h, wm_ref[:, n * d:(n + 1) * d], preferred_element_type=F32)
                              + bm_ref[:, n * d:(n + 1) * d])
        acc = acc + gate * jnp.dot(y_ref[...], wb_ref[n], preferred_element_type=F32)
    y = jnp.dot(acc.astype(BF16), wo_ref[...], preferred_element_type=F32)
    xn = x + mod_ref[2:3, :] * y
    xo_ref[...] = xn
    h2 = _rms(xn, g2_ref[...]) * (1.0 + mod_ref[4:5, :]) + mod_ref[3:4, :]
    h2_ref[...] = h2
    scores = jax.nn.sigmoid(jnp.dot(h2, rw_ref[...], preferred_element_type=F32,
                                    precision=lax.Precision.HIGHEST))
    sc_ref[...] = _route_tile(scores, rb_ref[...])


def _merge(x_all, mods, g1, g2, ys, wm, bm, wb, wo, rw, rb, n_ctx, with_ctx):
    b, s, d = x_all.shape
    tile0 = 0 if with_ctx else n_ctx // TM
    nt = s // TM - tile0
    tile = lambda w: pl.BlockSpec((None, TM, w), lambda bb, i: (bb, i + tile0, 0))
    sub = lambda w: pl.BlockSpec((None, TM, w), lambda bb, i: (bb, i, 0))
    so = nt * TM
    return pl.pallas_call(
        _merge_body,
        grid=(b, nt),
        in_specs=[tile(d),
                  pl.BlockSpec((None, None, 6, d), lambda bb, i: (bb, jnp.minimum(i + tile0, 1), 0, 0)),
                  _const_spec((1, d)), _const_spec((1, d)),
                  sub(BRANCH_W), tile(BRANCH_W), tile(BRANCH_W), sub(BRANCH_W),
                  _const_spec(wm.shape), _const_spec(bm.shape), _const_spec(wb.shape),
                  _const_spec(wo.shape), _const_spec(rw.shape), _const_spec(rb.shape)],
        out_specs=[sub(d), sub(d), sub(LANE)],
        out_shape=[jax.ShapeDtypeStruct((b, so, d), F32),
                   jax.ShapeDtypeStruct((b, so, d), F32),
                   jax.ShapeDtypeStruct((b, so, LANE), F32)],
        compiler_params=_cparams(("arbitrary", "arbitrary")),
        name="merge",
    )(x_all, mods, g1, g2, *ys, wm, bm, wb, wo, rw, rb)


def _expert_body(be_ref, nu_ref, x_ref, w1_ref, w3_ref, w2_ref, o_ref):
    j = pl.program_id(0)

    @pl.when(j < nu_ref[0])
    def _():
        x = x_ref[...].astype(BF16)
        a = jnp.dot(x, w1_ref[...].astype(BF16), preferred_element_type=F32)
        g = jnp.dot(x, w3_ref[...].astype(BF16), preferred_element_type=F32)
        hmid = (a * jax.nn.sigmoid(a) * g).astype(BF16)
        o_ref[...] = jnp.dot(hmid, w2_ref[...].astype(BF16), preferred_element_type=F32).astype(o_ref.dtype)

    @pl.when(j >= nu_ref[0])
    def _():
        o_ref[...] = jnp.zeros(o_ref.shape, o_ref.dtype)


def _expert_ffn(block_expert, n_used, xb, w1, w3, w2):
    n_rows, d = xb.shape
    nb = n_rows // EXP_ROWS
    ff = w1.shape[-1]
    grid_spec = pltpu.PrefetchScalarGridSpec(
        num_scalar_prefetch=2,
        grid=(nb,),
        in_specs=[pl.BlockSpec((EXP_ROWS, d), lambda j, be, nu: (j, 0)),
                  pl.BlockSpec((None, d, ff), lambda j, be, nu: (be[j], 0, 0)),
                  pl.BlockSpec((None, d, ff), lambda j, be, nu: (be[j], 0, 0)),
                  pl.BlockSpec((None, ff, d), lambda j, be, nu: (be[j], 0, 0))],
        out_specs=pl.BlockSpec((EXP_ROWS, d), lambda j, be, nu: (j, 0)),
    )
    return pl.pallas_call(
        _expert_body,
        grid_spec=grid_spec,
        out_shape=jax.ShapeDtypeStruct((n_rows, d), F32),
        compiler_params=_cparams(("arbitrary",)),
        name="expert_ffn",
    )(block_expert, n_used, xb, w1, w3, w2)


def _markers(route):
    lane = lax.broadcasted_iota(jnp.int32, route.shape, 1)
    return jnp.where(lane < N_EXPERTS, pltpu.roll(route, N_EXPERTS, 1), 0.0)


def _plan_body(rt_ref, slot_ref, cnt_ref, run_scr, start_scr):
    ph = pl.program_id(0)
    first = jnp.logical_and(pl.program_id(1) == 0, pl.program_id(2) == 0)
    mk = _markers(rt_ref[...])
    p = jnp.where(mk > 0.0, 1.0, 0.0)

    @pl.when(jnp.logical_and(ph == 0, first))
    def _():
        run_scr[...] = jnp.zeros(run_scr.shape, F32)

    @pl.when(ph == 0)
    def _():
        run_scr[...] += jnp.sum(p, axis=0, keepdims=True)

    @pl.when(jnp.logical_and(ph == 1, first))
    def _():
        cnt = run_scr[...]
        cnt_ref[...] = cnt
        padded = jnp.floor((cnt + (EXP_ROWS - 1.0)) * (1.0 / EXP_ROWS)) * EXP_ROWS
        lane = lax.broadcasted_iota(jnp.int32, cnt.shape, 1)
        end = padded
        sh = 1
        while sh < LANE:
            end = end + jnp.where(lane >= sh, pltpu.roll(end, sh, 1), 0.0)
            sh *= 2
        start_scr[...] = end - padded
        run_scr[...] = jnp.zeros(run_scr.shape, F32)

    @pl.when(ph == 1)
    def _():
        tm = p.shape[0]
        ri = lax.broadcasted_iota(jnp.int32, (tm, tm), 0)
        ci = lax.broadcasted_iota(jnp.int32, (tm, tm), 1)
        lower = jnp.where(ri > ci, 1.0, 0.0).astype(BF16)
        before = jnp.dot(lower, p.astype(BF16), preferred_element_type=F32)
        pos = before + run_scr[0:1, :] + start_scr[0:1, :]
        s1 = jnp.sum(jnp.where(mk == 1.0, pos, 0.0), axis=-1, keepdims=True)
        s2 = jnp.sum(jnp.where(mk == 2.0, pos, 0.0), axis=-1, keepdims=True)
        which = lax.broadcasted_iota(jnp.int32, (tm, TOP_K), 1)
        slot_ref[...] = jnp.where(which == 0, s1, s2).astype(jnp.int32)
        run_scr[...] += jnp.sum(p, axis=0, keepdims=True)


def _plan(route):
    b, s, _ = route.shape
    nt = s // TM
    return pl.pallas_call(
        _plan_body,
        grid=(2, b, nt),
        in_specs=[pl.BlockSpec((None, TM, LANE), lambda ph, bb, i: (bb, i, 0))],
        out_specs=[pl.BlockSpec((None, TM, TOP_K), lambda ph, bb, i: (bb * ph, i * ph, 0)),
                   pl.BlockSpec((SUBLANE, LANE), lambda ph, bb, i: (0, 0))],
        out_shape=[jax.ShapeDtypeStruct((b, s, TOP_K), jnp.int32),
                   jax.ShapeDtypeStruct((SUBLANE, LANE), F32)],
        scratch_shapes=[pltpu.VMEM((SUBLANE, LANE), F32), pltpu.VMEM((SUBLANE, LANE), F32)],
        compiler_params=_cparams(("arbitrary", "arbitrary", "arbitrary")),
        name="dispatch_plan",
    )(route)


def _block_experts(counts, n_blocks):
    cnt = counts[0, :N_EXPERTS].astype(jnp.int32)
    padded = (cnt + EXP_ROWS - 1) // EXP_ROWS * EXP_ROWS
    pad_end = jnp.cumsum(padded)
    block_expert = jnp.minimum(
        jnp.searchsorted(pad_end, jnp.arange(n_blocks, dtype=jnp.int32) * EXP_ROWS, side='right'),
        N_EXPERTS - 1).astype(jnp.int32)
    n_used = (pad_end[-1] // EXP_ROWS).astype(jnp.int32).reshape(1)
    return block_expert, n_used


def _row_copy(src, src_row, dst, dst_row, sem):
    return pltpu.make_async_copy(src.at[pl.ds(src_row, 1)], dst.at[pl.ds(dst_row, 1)], sem)


def _dispatch_body(slot_ref, h_ref, xb_in, xb_ref, sem):
    del xb_in
    tm = h_ref.shape[0]

    def issue(t, carry):
        for k in range(TOP_K):
            _row_copy(h_ref, t, xb_ref, slot_ref[0, TOP_K * t + k], sem).start()
        return carry

    lax.fori_loop(0, tm, issue, 0, unroll=8)
    for k in range(TOP_K):
        pltpu.make_async_copy(h_ref, xb_ref.at[pl.ds(0, tm)], sem).wait()


def _dispatch(slot_tiles, h2, xb0):
    b, s, d = h2.shape
    nt = b * s // TM
    return pl.pallas_call(
        _dispatch_body,
        grid=(nt,),
        in_specs=[pl.BlockSpec((None, 1, TOP_K * TM), lambda i: (i, 0, 0), memory_space=pltpu.SMEM),
                  pl.BlockSpec((TM, d), lambda i: (i, 0)),
                  pl.BlockSpec(memory_space=pl.ANY)],
        out_specs=pl.BlockSpec(memory_space=pl.ANY),
        out_shape=jax.ShapeDtypeStruct(xb0.shape, xb0.dtype),
        scratch_shapes=[pltpu.SemaphoreType.DMA(())],
        input_output_aliases={2: 0},
        compiler_params=_cparams(("arbitrary",)),
        name="dispatch",
    )(slot_tiles, h2.reshape(b * s, d), xb0)


def _resid_body(slot_ref, x_ref, mod_ref, rt_ref, gf_ref, y_hbm, o_ref, ybuf0, ybuf1, sem, *, final):
    tm = x_ref.shape[0]
    ybuf = (ybuf0, ybuf1)

    def issue(t, carry):
        for k in range(TOP_K):
            _row_copy(y_hbm, slot_ref[0, TOP_K * t + k], ybuf[k], t, sem).start()
        return carry

    lax.fori_loop(0, tm, issue, 0, unroll=8)
    rt = rt_ref[...]
    mk = _markers(rt)
    w1 = jnp.sum(jnp.where(mk == 1.0, rt, 0.0), axis=-1, keepdims=True)
    w2 = jnp.sum(jnp.where(mk == 2.0, rt, 0.0), axis=-1, keepdims=True)
    for k in range(TOP_K):
        pltpu.make_async_copy(y_hbm.at[pl.ds(0, tm)], ybuf[k], sem).wait()
    f = w1 * ybuf0[...] + w2 * ybuf1[...]
    xn = x_ref[...] + mod_ref[5:6, :] * f
    if final:
        xn = _rms(xn, gf_ref[...])
    o_ref[...] = xn


def _moe_residual(slot_tiles, x_all, mods, route, gf, yblk, has_ctx, final):
    b, s, d = x_all.shape
    nt = s // TM
    tin = lambda ww: pl.BlockSpec((None, TM, ww), lambda bb, i: (bb, i, 0))
    mod_row = (lambda i: jnp.minimum(i, 1)) if has_ctx else (lambda i: 1)
    return pl.pallas_call(
        functools.partial(_resid_body, final=final),
        grid=(b, nt),
        in_specs=[pl.BlockSpec((None, 1, TOP_K * TM), lambda bb, i: (bb * nt + i, 0, 0),
                               memory_space=pltpu.SMEM),
                  tin(d),
                  pl.BlockSpec((None, None, 6, d), lambda bb, i: (bb, mod_row(i), 0, 0)),
                  tin(LANE), _const_spec((1, d)),
                  pl.BlockSpec(memory_space=pl.ANY)],
        out_specs=tin(d),
        out_shape=jax.ShapeDtypeStruct((b, s, d), F32),
        scratch_shapes=[pltpu.VMEM((TM, d), F32), pltpu.VMEM((TM, d), F32), pltpu.SemaphoreType.DMA(())],
        compiler_params=_cparams(("arbitrary", "arbitrary")),
        name="moe_residual",
    )(slot_tiles, x_all, mods, route, gf, yblk)


def kernel(x, c, ctx, c_ctx, ada_w, ada_b, norm1_g, norm2_g, w_in, mla_q_norm, mla_kv_norm, mla_w_uq,
           mla_w_ukv, ret_decay, ret_norm, lru_conv_w, lru_conv_b, lru_w_a, lru_b_a, lru_w_x, lru_b_x,
           lru_lambda, gqa_q_norm, gqa_k_norm, w_branch, w_merge, b_merge, w_out, router_w, router_bias,
           moe_w1, moe_w3, moe_w2, final_norm):
    b, seq, d = x.shape
    n_ctx = ctx.shape[1]
    depth = ada_w.shape[0]
    s = n_ctx + seq
    assert n_ctx == TM and seq % TM == 0 and seq % GRID_W == 0

    r_pad = -(-(b + 1) // SUBLANE) * SUBLANE
    cc = jnp.zeros((r_pad, d), F32).at[:b].set(c).at[b].set(c_ctx)
    mods_all = _ada_mods(cc, ada_w, ada_b)

    mla_cos, mla_sin = _rope_slot_tables(n_ctx, seq, MLA_ROPE, MLA_NOPE)
    hd_cos, hd_sin = _rope_slot_tables(n_ctx, seq, GQA_HEAD_DIM, 0)

    in_cols = _in_proj_columns()
    uq_cols, ukv_cols = _mla_up_columns()
    rw = jnp.concatenate([router_w, jnp.zeros((d, LANE - N_EXPERTS), F32)], axis=1)
    rb = jnp.concatenate([router_bias.astype(F32), jnp.zeros((LANE - N_EXPERTS,), F32)])[None]

    x_all = jnp.concatenate([ctx, x], axis=1)
    out = None
    for l in range(depth):
        last = l == depth - 1
        m = mods_all[l].reshape(r_pad, 6, d)
        mods = jnp.stack([jnp.broadcast_to(m[b], (b, 6, d)), m[:b]], axis=1)

        mla_scale = (MLA_NOPE + MLA_ROPE) ** -0.5
        gqa_scale = GQA_HEAD_DIM ** -0.5

        def swap_gain(g):
            g4 = g.reshape(2, 2, GQA_HEAD_DIM // 4)
            return g4[:, ::-1, :].reshape(GQA_HEAD_DIM)

        def gain_slot(g):
            return jnp.concatenate([g, jnp.zeros((LANE - GQA_HEAD_DIM,), F32)])

        gqc, gqs = gain_slot(gqa_q_norm[l]), gain_slot(swap_gain(gqa_q_norm[l]))
        gkc, gks = gain_slot(gqa_k_norm[l]), gain_slot(swap_gain(gqa_k_norm[l]))
        tabs = jnp.stack([
            mla_cos * mla_scale, mla_sin * mla_scale, mla_cos, mla_sin,
            hd_cos, hd_sin, hd_cos * RET_DK ** -0.5, hd_sin * RET_DK ** -0.5,
            hd_cos * gqc * gqa_scale, hd_sin * gqs * gqa_scale, hd_cos * gkc, hd_sin * gks])

        win_p = _take_cols(w_in[l], in_cols).astype(BF16)
        wuq_p = _take_cols(mla_w_uq[l], uq_cols).astype(BF16)
        wukv_p = _take_cols(mla_w_ukv[l], ukv_cols).astype(BF16)

        (mq, mk, mv, rq, rk, rv, rg, lu, lg, gq, gk, gv) = _inproj(
            x_all, mods, norm1_g[l][None], tabs, win_p, wuq_p, wukv_p,
            mla_q_norm[l][None], mla_kv_norm[l][None])

        ya = _attention(mq, mk, mv, MLA_HEADS, MLA_HEADS, n_ctx, not last)
        yd = _attention(gq, gk, gv, GQA_HEADS, GQA_KV_HEADS, n_ctx, not last)

        log_g = -jax.nn.softplus(-ret_decay[l].astype(F32))
        yb = _retention(log_g, rq, rk, rv, rg, ret_norm[l][None], n_ctx)

        eye = jnp.eye(LRU_BLOCKS, dtype=F32)

        def block_diag(wblk):
            return jnp.einsum('ncd,nm->ncmd', wblk, eye).reshape(LRU_WIDTH, LRU_WIDTH)

        wcat = jnp.concatenate([block_diag(lru_w_a[l, 0]), block_diag(lru_w_x[l, 0]),
                                block_diag(lru_w_a[l, 1]), block_diag(lru_w_x[l, 1])], axis=1).astype(BF16)
        bcat = jnp.concatenate([lru_b_a[l, 0], lru_b_x[l, 0], lru_b_a[l, 1], lru_b_x[l, 1]])[None]
        yc = _rglru(lu, lg, wcat, bcat, lru_conv_w[l], lru_conv_b[l][None], lru_lambda[l], n_ctx)

        x_all, h2, route = _merge(
            x_all, mods, norm1_g[l][None], norm2_g[l][None], (ya, yb, yc, yd),
            w_merge[l].astype(BF16), b_merge[l][None], w_branch[l].astype(BF16), w_out[l].astype(BF16),
            rw, rb, n_ctx, not last)

        t = b * (seq if last else s)
        n_blocks = -(-t * TOP_K // EXP_ROWS) + N_EXPERTS
        slot, counts = _plan(route)
        slot_tiles = slot.reshape(t // TM, 1, TOP_K * TM)
        block_expert, n_used = _block_experts(counts, n_blocks)
        xb = _dispatch(slot_tiles, h2, jnp.zeros((n_blocks * EXP_ROWS, d), F32))
        yblk = _expert_ffn(block_expert, n_used, xb, moe_w1[l], moe_w3[l], moe_w2[l])
        out = _moe_residual(slot_tiles, x_all, mods, route, final_norm[None], yblk, not last, last)
        x_all = out
    return out
```

```python
import functools

import numpy as np
import jax
import jax.numpy as jnp
from jax import lax
from jax.experimental import pallas as pl
from jax.experimental.pallas import tpu as pltpu

F32 = jnp.float32
BF16 = jnp.bfloat16

LANE = 128
SUBLANE = 8
VMEM_LIMIT = 56 * 1024 * 1024

GRID_W = 64
ROPE_BASE = 10000.0
EPS = 1e-6
MLA_HEADS, MLA_NOPE, MLA_ROPE, MLA_V = 8, 64, 32, 64
MLA_Q_LORA, MLA_KV_LORA = 256, 128
RET_HEADS, RET_DK, RET_DV, RET_CHUNK = 4, 64, 128, 128
LRU_WIDTH, LRU_BLOCKS, LRU_C, CONV_W, CONV_PAD_LEFT = 512, 8, 8.0, 4, 2
LRU_BLOCK_W = LRU_WIDTH // LRU_BLOCKS
GQA_HEADS, GQA_KV_HEADS, GQA_HEAD_DIM = 8, 2, 64
N_BRANCH, BRANCH_W = 4, 512
N_EXPERTS, N_GROUPS, TOP_K, EXPERT_FF, MOE_BLOCK = 64, 8, 2, 256, 128
EXPERTS_PER_GROUP = N_EXPERTS // N_GROUPS

IN_SPLITS = (MLA_Q_LORA, MLA_KV_LORA, MLA_ROPE,
             RET_HEADS * RET_DK, RET_HEADS * RET_DK, RET_HEADS * RET_DV, RET_HEADS * RET_DV,
             LRU_WIDTH, LRU_WIDTH,
             GQA_HEADS * GQA_HEAD_DIM, GQA_KV_HEADS * GQA_HEAD_DIM, GQA_KV_HEADS * GQA_HEAD_DIM)
IN_OFF = tuple(int(o) for o in np.cumsum((0,) + IN_SPLITS))
D_IN = IN_OFF[-1]

TM = 256
EXP_ROWS = 256
HEAD_SLOT = LANE
ONE_LANE = 64

ZP_CQ, ZP_CKV, ZP_KR = 0, 256, 384
ZP_RQ, ZP_RK, ZP_RV, ZP_RG = 512, 1024, 1536, 2048
ZP_LU, ZP_LG = 2560, 3072
ZP_GQ, ZP_GK, ZP_GV = 3584, 4608, 4864
ZP_W = 5120


def _cparams(sem):
    return pltpu.CompilerParams(dimension_semantics=sem, vmem_limit_bytes=VMEM_LIMIT)


def _const_spec(shape):
    nd = len(shape)
    return pl.BlockSpec(shape, lambda *_: (0,) * nd, pipeline_mode=pl.Buffered(1))


def _in_proj_columns():
    idx = np.full((ZP_W,), D_IN, np.int64)
    o = IN_OFF
    idx[ZP_CQ:ZP_CQ + 256] = o[0] + np.arange(256)
    idx[ZP_CKV:ZP_CKV + 128] = o[1] + np.arange(128)
    idx[ZP_KR + MLA_NOPE:ZP_KR + MLA_NOPE + MLA_ROPE] = o[2] + np.arange(MLA_ROPE)
    for h in range(RET_HEADS):
        idx[ZP_RQ + h * 128:ZP_RQ + h * 128 + 64] = o[3] + h * 64 + np.arange(64)
        idx[ZP_RK + h * 128:ZP_RK + h * 128 + 64] = o[4] + h * 64 + np.arange(64)
    idx[ZP_RV:ZP_RV + 512] = o[5] + np.arange(512)
    idx[ZP_RG:ZP_RG + 512] = o[6] + np.arange(512)
    idx[ZP_LU:ZP_LU + 512] = o[7] + np.arange(512)
    idx[ZP_LG:ZP_LG + 512] = o[8] + np.arange(512)
    for h in range(GQA_HEADS):
        idx[ZP_GQ + h * 128:ZP_GQ + h * 128 + 64] = o[9] + h * 64 + np.arange(64)
    for h in range(GQA_KV_HEADS):
        idx[ZP_GK + h * 128:ZP_GK + h * 128 + 64] = o[10] + h * 64 + np.arange(64)
        idx[ZP_GV + h * 128:ZP_GV + h * 128 + 64] = o[11] + h * 64 + np.arange(64)
    return idx


def _mla_up_columns():
    dq = MLA_NOPE + MLA_ROPE
    dkv = MLA_NOPE + MLA_V
    qi = np.full((MLA_HEADS * 128,), MLA_HEADS * dq, np.int64)
    ki = np.full((MLA_HEADS * 128,), MLA_HEADS * dkv, np.int64)
    vi = np.full((MLA_HEADS * 128,), MLA_HEADS * dkv, np.int64)
    for h in range(MLA_HEADS):
        qi[h * 128:h * 128 + dq] = h * dq + np.arange(dq)
        ki[h * 128:h * 128 + MLA_NOPE] = h * dkv + np.arange(MLA_NOPE)
        vi[h * 128:h * 128 + MLA_V] = h * dkv + MLA_NOPE + np.arange(MLA_V)
    return qi, np.concatenate([ki, vi])


def _take_cols(w, idx):
    wz = jnp.concatenate([w, jnp.zeros((w.shape[0], 1), w.dtype)], axis=1)
    return jnp.take(wz, jnp.asarray(idx, jnp.int32), axis=1)


def _rope_slot_tables(n_ctx, seq, rot_dim, lane0):
    half = rot_dim // 2
    q = half // 2
    pos = jnp.arange(seq, dtype=jnp.int32)
    rows = (pos // GRID_W).astype(F32)
    cols = (pos % GRID_W).astype(F32)
    inv = ROPE_BASE ** (-jnp.arange(0, half, 2, dtype=F32) / half)
    ar = rows[:, None] * inv
    ac = cols[:, None] * inv
    cos = jnp.concatenate([jnp.cos(ar), jnp.cos(ar), jnp.cos(ac), jnp.cos(ac)], axis=1)
    sin = jnp.concatenate([-jnp.sin(ar), jnp.sin(ar), -jnp.sin(ac), jnp.sin(ac)], axis=1)
    assert cos.shape[1] == rot_dim and q * 4 == rot_dim
    cos_t = jnp.ones((n_ctx + seq, LANE), F32).at[n_ctx:, lane0:lane0 + rot_dim].set(cos)
    sin_t = jnp.zeros((n_ctx + seq, LANE), F32).at[n_ctx:, lane0:lane0 + rot_dim].set(sin)
    return cos_t, sin_t


def _swap_lanes(x, blk):
    n = x.shape[-1]
    lane = lax.broadcasted_iota(jnp.int32, x.shape, x.ndim - 1)
    up = pltpu.roll(x, n - blk, x.ndim - 1)
    dn = pltpu.roll(x, blk, x.ndim - 1)
    return jnp.where((lane % (2 * blk)) < blk, up, dn)


def _rms(x, g):
    return x * lax.rsqrt(jnp.mean(x * x, axis=-1, keepdims=True) + EPS) * g


def _ada_body(c_ref, w_ref, b_ref, o_ref):
    c = c_ref[...]
    s = (c * jax.nn.sigmoid(c)).astype(BF16)
    o_ref[...] = jnp.dot(s, w_ref[...].astype(BF16), preferred_element_type=F32) + b_ref[...]


def _ada_mods(cc, ada_w, ada_b):
    depth, d, n = ada_w.shape
    r = cc.shape[0]
    tn = 1536
    return pl.pallas_call(
        _ada_body,
        grid=(depth, n // tn),
        in_specs=[pl.BlockSpec((r, d), lambda l, j: (0, 0)),
                  pl.BlockSpec((None, d, tn), lambda l, j: (l, 0, j)),
                  pl.BlockSpec((None, 1, tn), lambda l, j: (l, 0, j))],
        out_specs=pl.BlockSpec((None, r, tn), lambda l, j: (l, 0, j)),
        out_shape=jax.ShapeDtypeStruct((depth, r, n), F32),
        compiler_params=_cparams(("arbitrary", "arbitrary")),
        name="ada_mods",
    )(cc, ada_w, ada_b.reshape(depth, 1, n))


def _inproj_body(x_ref, mod_ref, g1_ref, tab_ref, win_ref, wuq_ref, wukv_ref, gq_ref, gkv_ref,
                 mq_ref, mk_ref, mv_ref, rq_ref, rk_ref, rv_ref, rg_ref, lu_ref, lg_ref,
                 gq_out, gk_out, gv_out):
    x = x_ref[...]
    shift = mod_ref[0:1, :]
    scale = mod_ref[1:2, :]
    h = _rms(x, g1_ref[...]) * (1.0 + scale) + shift
    hb = h.astype(BF16)

    def proj(c0, c1):
        return jnp.dot(hb, win_ref[:, c0:c1], preferred_element_type=F32)

    lane = lax.broadcasted_iota(jnp.int32, (x.shape[0], LANE), 1)
    one_col = jnp.where(lane == ONE_LANE, 1.0, 0.0).astype(F32)

    def rope(v, ci, blk):
        return v * tab_ref[ci] + _swap_lanes(v, blk) * tab_ref[ci + 1]

    cq = proj(ZP_CQ, ZP_CQ + 256)
    qn = _rms(cq, gq_ref[...]).astype(BF16)
    q = jnp.dot(qn, wuq_ref[...], preferred_element_type=F32)
    for hh in range(MLA_HEADS):
        sl = slice(hh * 128, (hh + 1) * 128)
        mq_ref[:, sl] = rope(q[:, sl], 0, MLA_ROPE // 4).astype(BF16)
    ckv = proj(ZP_CKV, ZP_CKV + 128)
    kvn = _rms(ckv, gkv_ref[...]).astype(BF16)
    kv = jnp.dot(kvn, wukv_ref[...], preferred_element_type=F32)
    kr = rope(proj(ZP_KR, ZP_KR + 128), 2, MLA_ROPE // 4)
    for hh in range(MLA_HEADS):
        sl = slice(hh * 128, (hh + 1) * 128)
        mk_ref[:, sl] = (kv[:, sl] + kr).astype(BF16)
        mv_ref[:, sl] = (kv[:, MLA_HEADS * 128 + hh * 128:MLA_HEADS * 128 + (hh + 1) * 128] + one_col).astype(BF16)

    rq = proj(ZP_RQ, ZP_RQ + 512)
    rk = proj(ZP_RK, ZP_RK + 512)
    for hh in range(RET_HEADS):
        sl = slice(hh * 128, (hh + 1) * 128)
        rq_ref[:, sl] = rope(rq[:, sl], 4, RET_DK // 4).astype(BF16)
        rk_ref[:, sl] = rope(rk[:, sl], 6, RET_DK // 4).astype(BF16)
    rv_ref[...] = proj(ZP_RV, ZP_RV + 512).astype(BF16)
    rg_ref[...] = proj(ZP_RG, ZP_RG + 512).astype(BF16)

    lu_ref[...] = proj(ZP_LU, ZP_LU + 512).astype(BF16)
    lg_ref[...] = proj(ZP_LG, ZP_LG + 512).astype(BF16)

    gq = proj(ZP_GQ, ZP_GQ + 1024)
    for hh in range(GQA_HEADS):
        sl = slice(hh * 128, (hh + 1) * 128)
        v = gq[:, sl]
        v = v * lax.rsqrt(jnp.sum(v * v, axis=-1, keepdims=True) * (1.0 / GQA_HEAD_DIM) + EPS)
        gq_out[:, sl] = rope(v, 8, GQA_HEAD_DIM // 4).astype(BF16)
    gk = proj(ZP_GK, ZP_GK + 256)
    gv = proj(ZP_GV, ZP_GV + 256)
    for hh in range(GQA_KV_HEADS):
        sl = slice(hh * 128, (hh + 1) * 128)
        v = gk[:, sl]
        v = v * lax.rsqrt(jnp.sum(v * v, axis=-1, keepdims=True) * (1.0 / GQA_HEAD_DIM) + EPS)
        gk_out[:, sl] = rope(v, 10, GQA_HEAD_DIM // 4).astype(BF16)
        gv_out[:, sl] = (gv[:, sl] + one_col).astype(BF16)


def _inproj(x_all, mods, g1, tabs, win_p, wuq_p, wukv_p, gq, gkv):
    b, s, d = x_all.shape
    nt = s // TM
    widths = (1024, 1024, 1024, 512, 512, 512, 512, 512, 512, 1024, 256, 256)
    tile = lambda w: pl.BlockSpec((None, TM, w), lambda i, bb: (bb, i, 0))
    return pl.pallas_call(
        _inproj_body,
        grid=(nt, b),
        in_specs=[tile(d),
                  pl.BlockSpec((None, None, 6, d), lambda i, bb: (bb, jnp.minimum(i, 1), 0, 0)),
                  _const_spec((1, d)),
                  pl.BlockSpec((12, TM, LANE), lambda i, bb: (0, i, 0)),
                  _const_spec(win_p.shape), _const_spec(wuq_p.shape), _const_spec(wukv_p.shape),
                  _const_spec((1, MLA_Q_LORA)), _const_spec((1, MLA_KV_LORA))],
        out_specs=[tile(w) for w in widths],
        out_shape=[jax.ShapeDtypeStruct((b, s, w), BF16) for w in widths],
        compiler_params=_cparams(("arbitrary", "arbitrary")),
        name="in_proj",
    )(x_all, mods, g1, tabs, win_p, wuq_p, wukv_p, gq, gkv)


def _attn_body(q_ref, k_ref, v_ref, o_ref, *, heads, kv_heads, n_ctx, tile0):
    i = pl.program_id(1) + tile0
    grp = heads // kv_heads

    def run(klen):
        for hp in range(heads // 2):
            outs = []
            for h in (2 * hp, 2 * hp + 1):
                g = h // grp
                q = q_ref[:, h * 128:(h + 1) * 128]
                k = k_ref[0:klen, g * 128:(g + 1) * 128]
                v = v_ref[0:klen, g * 128:(g + 1) * 128]
                s = lax.dot_general(q, k, (((1,), (1,)), ((), ())), preferred_element_type=F32)
                m = jnp.max(s, axis=-1, keepdims=True)
                p = jnp.exp((s - m).astype(BF16))
                o = jnp.dot(p, v, preferred_element_type=F32)
                outs.append(o[:, :64] / o[:, ONE_LANE:ONE_LANE + 1])
            o_ref[:, hp * 128:(hp + 1) * 128] = jnp.concatenate(outs, axis=1).astype(o_ref.dtype)

    if tile0 == 0:
        @pl.when(i == 0)
        def _():
            run(n_ctx)

        @pl.when(i > 0)
        def _():
            run(k_ref.shape[0])
    else:
        run(k_ref.shape[0])


def _attention(q, k, v, heads, kv_heads, n_ctx, with_ctx):
    b, s, _ = q.shape
    tile0 = 0 if with_ctx else n_ctx // TM
    nq = s // TM - tile0
    body = functools.partial(_attn_body, heads=heads, kv_heads=kv_heads, n_ctx=n_ctx, tile0=tile0)
    return pl.pallas_call(
        body,
        grid=(b, nq),
        in_specs=[pl.BlockSpec((None, TM, heads * 128), lambda bb, i: (bb, i + tile0, 0)),
                  pl.BlockSpec((None, s, kv_heads * 128), lambda bb, i: (bb, 0, 0)),
                  pl.BlockSpec((None, s, kv_heads * 128), lambda bb, i: (bb, 0, 0))],
        out_specs=pl.BlockSpec((None, TM, heads * 64), lambda bb, i: (bb, i, 0)),
        out_shape=jax.ShapeDtypeStruct((b, nq * TM, heads * 64), BF16),
        compiler_params=_cparams(("arbitrary", "arbitrary")),
        name="attention_h%d_kv%d" % (heads, kv_heads),
    )(q, k, v)


RET_ROWS = 256


def _retention_body(lg_ref, q_ref, k_ref, v_ref, g_ref, gn_ref, o_ref,
                    kv_scr, st_scr, dec_scr, m_scr, *, n_ctx):
    c = RET_ROWS
    s = q_ref.shape[0]
    nc = s // c
    nctx = n_ctx // c
    back_order = list(range(nctx - 1, -1, -1)) + list(range(nc - 1, nctx - 1, -1))
    pos = lax.broadcasted_iota(jnp.int32, (c, LANE), 0).astype(F32)
    ri = lax.broadcasted_iota(jnp.int32, (c, c), 0)
    ci = lax.broadcasted_iota(jnp.int32, (c, c), 1)
    diff = (ri - ci).astype(F32)
    heads = [(h, slice(h * 128, (h + 1) * 128)) for h in range(RET_HEADS)]

    for h, _ in heads:
        lgf = lg_ref[0, h]
        lgb = lg_ref[1, h]
        dec_scr[h, 0] = jnp.exp(lgf * (c - 1.0 - pos))
        dec_scr[h, 1] = jnp.exp(lgb * pos)
        dec_scr[h, 2] = jnp.exp(lgf * (pos + 1.0))
        dec_scr[h, 3] = jnp.exp(lgb * (c - pos))
        m_scr[h] = jnp.where(diff >= 0, jnp.exp(lgf * jnp.maximum(diff, 0.0)),
                             jnp.exp(lgb * jnp.maximum(-diff, 0.0)))

    def kv_step(j, carry):
        r0 = pl.multiple_of(j * c, c)
        for h, sl in heads:
            kc = k_ref[pl.ds(r0, c), sl].astype(F32)
            vc = v_ref[pl.ds(r0, c), sl]
            kk = jnp.concatenate([(kc * dec_scr[h, 0]).astype(BF16), (kc * dec_scr[h, 1]).astype(BF16)], axis=1)
            kv_scr[h, j] = lax.dot_general(kk, vc, (((0,), (0,)), ((), ())), preferred_element_type=F32)
        return carry

    lax.fori_loop(0, nc, kv_step, 0, unroll=3)

    for h, _ in heads:
        gcf = jnp.exp(lg_ref[0, h] * c)
        gcb = jnp.exp(lg_ref[1, h] * c)
        sf = jnp.zeros((128, 128), F32)
        for j in range(nc):
            st_scr[h, j, 0:128, :] = sf.astype(BF16)
            sf = sf * gcf + kv_scr[h, j, 0:128, :]
        sb = jnp.zeros((128, 128), F32)
        for j in back_order:
            st_scr[h, j, 128:256, :] = sb.astype(BF16)
            sb = sb * gcb + kv_scr[h, j, 128:256, :]

    def out_step(j, carry):
        r0 = pl.multiple_of(j * c, c)
        for h, sl in heads:
            qb = q_ref[pl.ds(r0, c), sl]
            kb = k_ref[pl.ds(r0, c), sl]
            vc = v_ref[pl.ds(r0, c), sl]
            sc = lax.dot_general(qb, kb, (((1,), (1,)), ((), ())), preferred_element_type=F32) * m_scr[h]
            o = jnp.dot(sc.astype(BF16), vc, preferred_element_type=F32)
            qf = qb.astype(F32)
            qd = jnp.concatenate([(qf * dec_scr[h, 2]).astype(BF16), (qf * dec_scr[h, 3]).astype(BF16)], axis=1)
            o = o + jnp.dot(qd, st_scr[h, j], preferred_element_type=F32)
            mu = jnp.mean(o, axis=-1, keepdims=True)
            oc = o - mu
            var = jnp.mean(oc * oc, axis=-1, keepdims=True)
            y = oc * lax.rsqrt(var + EPS) * gn_ref[:, sl]
            gate = g_ref[pl.ds(r0, c), sl].astype(F32)
            o_ref[pl.ds(r0, c), sl] = (gate * jax.nn.sigmoid(gate) * y).astype(o_ref.dtype)
        return carry

    lax.fori_loop(0, nc, out_step, 0, unroll=3)


def _retention(log_g, q, k, v, g, gn, n_ctx):
    b, s, w = v.shape
    nc = s // RET_ROWS
    blk = lambda ww: pl.BlockSpec((None, s, ww), lambda bb: (bb, 0, 0))
    return pl.pallas_call(
        functools.partial(_retention_body, n_ctx=n_ctx),
        grid=(b,),
        in_specs=[pl.BlockSpec(memory_space=pltpu.SMEM),
                  blk(512), blk(512), blk(512), blk(512), _const_spec((1, 512))],
        out_specs=blk(512),
        out_shape=jax.ShapeDtypeStruct((b, s, 512), BF16),
        scratch_shapes=[pltpu.VMEM((RET_HEADS, nc, 256, 128), F32),
                        pltpu.VMEM((RET_HEADS, nc, 256, 128), BF16),
                        pltpu.VMEM((RET_HEADS, 4, RET_ROWS, LANE), F32),
                        pltpu.VMEM((RET_HEADS, RET_ROWS, RET_ROWS), F32)],
        compiler_params=_cparams(("arbitrary",)),
        name="retention",
    )(log_g, q, k, v, g, gn)


LRU_ROWS = 256
LRU_HALO = 16


def _tile_scan(a, bv, carry, reverse):
    row = lax.broadcasted_iota(jnp.int32, a.shape, 0)
    for dlt in (1, 2, 4):
        if reverse:
            a_s = pltpu.roll(a, SUBLANE - dlt, 0)
            b_s = pltpu.roll(bv, SUBLANE - dlt, 0)
            ok = row < SUBLANE - dlt
        else:
            a_s = pltpu.roll(a, dlt, 0)
            b_s = pltpu.roll(bv, dlt, 0)
            ok = row >= dlt
        bv = jnp.where(ok, a * b_s + bv, bv)
        a = jnp.where(ok, a * a_s, a)
    h = a * carry + bv
    new_carry = h[0:1, :] if reverse else h[SUBLANE - 1:SUBLANE, :]
    return h, new_carry


def _rglru_body(u_ref, g_ref, wc_ref, bc_ref, cw_ref, cb_ref, lam_ref, o_ref, a_scr, b_scr, *, n_ctx):
    s = u_ref.shape[0]
    w = LRU_WIDTH
    r = LRU_ROWS
    cdec = [-LRU_C * jax.nn.softplus(-lam_ref[d:d + 1, :]) for d in range(2)]
    zeros = jnp.zeros((LRU_HALO, w), F32)

    for ch in range(s // r):
        r0 = ch * r
        seg_start = r0 == 0 or r0 == n_ctx
        seg_end = r0 + r == n_ctx or r0 + r == s
        lo = r0 if seg_start else r0 - LRU_HALO
        hi = r0 + r if seg_end else r0 + r + LRU_HALO
        parts = [u_ref[lo:hi, :].astype(F32)]
        if seg_start:
            parts = [zeros] + parts
        if seg_end:
            parts = parts + [zeros]
        ext = jnp.concatenate(parts, axis=0) if len(parts) > 1 else parts[0]
        n = ext.shape[0]
        u = cb_ref[...]
        for j in range(CONV_W):
            sh = (CONV_PAD_LEFT - j) % n
            tap = ext if sh == 0 else pltpu.roll(ext, sh, 0)
            u = u + tap[LRU_HALO:LRU_HALO + r, :] * cw_ref[j:j + 1, :]
        gates = jnp.dot(u.astype(BF16), wc_ref[...], preferred_element_type=F32) + bc_ref[...]
        for d in range(2):
            rg = jax.nn.sigmoid(gates[:, (2 * d) * w:(2 * d + 1) * w])
            ig = jax.nn.sigmoid(gates[:, (2 * d + 1) * w:(2 * d + 2) * w])
            a = jnp.exp(rg * cdec[d])
            a_scr[d, r0:r0 + r, :] = a
            b_scr[d, r0:r0 + r, :] = jnp.sqrt(1.0 - a * a) * (ig * u)

    def fwd(t, carry):
        r0 = pl.multiple_of(t * SUBLANE, SUBLANE)
        h, carry = _tile_scan(a_scr[0, pl.ds(r0, SUBLANE), :], b_scr[0, pl.ds(r0, SUBLANE), :], carry, False)
        b_scr[0, pl.ds(r0, SUBLANE), :] = h
        return carry

    lax.fori_loop(0, s // SUBLANE, fwd, jnp.zeros((1, w), F32), unroll=4)

    def bwd(t0):
        def step(t, carry):
            r0 = pl.multiple_of((t0 - t) * SUBLANE, SUBLANE)
            h, carry = _tile_scan(a_scr[1, pl.ds(r0, SUBLANE), :], b_scr[1, pl.ds(r0, SUBLANE), :], carry, True)
            b_scr[1, pl.ds(r0, SUBLANE), :] = h
            return carry
        return step

    nct = n_ctx // SUBLANE
    carry = lax.fori_loop(0, nct, bwd(nct - 1), jnp.zeros((1, w), F32), unroll=4)
    lax.fori_loop(0, s // SUBLANE - nct, bwd(s // SUBLANE - 1), carry, unroll=4)

    def fin(j, carry):
        r0 = pl.multiple_of(j * r, r)
        hsum = b_scr[0, pl.ds(r0, r), :] + b_scr[1, pl.ds(r0, r), :]
        gate = g_ref[pl.ds(r0, r), :].astype(F32)
        o_ref[pl.ds(r0, r), :] = (hsum * jax.nn.gelu(gate)).astype(o_ref.dtype)
        return carry

    lax.fori_loop(0, s // r, fin, 0)


def _rglru(u, g, wcat, bcat, conv_w, conv_b, lam, n_ctx):
    b, s, w = u.shape
    blk = pl.BlockSpec((None, s, w), lambda bb: (bb, 0, 0))
    return pl.pallas_call(
        functools.partial(_rglru_body, n_ctx=n_ctx),
        grid=(b,),
        in_specs=[blk, blk, _const_spec(wcat.shape), _const_spec(bcat.shape),
                  _const_spec(conv_w.shape), _const_spec(conv_b.shape), _const_spec(lam.shape)],
        out_specs=blk,
        out_shape=jax.ShapeDtypeStruct((b, s, w), BF16),
        scratch_shapes=[pltpu.VMEM((2, s, w), F32), pltpu.VMEM((2, s, w), F32)],
        compiler_params=_cparams(("arbitrary",)),
        name="rglru",
    )(u, g, wcat, bcat, conv_w, conv_b, lam)


def _group_all(v, op):
    blk = 1
    while blk < EXPERTS_PER_GROUP:
        v = op(v, _swap_lanes(v, blk))
        blk *= 2
    return v


def _route_tile(scores, bias):
    neg = jnp.float32(-3.0e38)
    far = jnp.float32(1.0e9)
    lane = lax.broadcasted_iota(jnp.int32, scores.shape, 1)
    lane_f = lane.astype(F32)
    valid = lane < N_EXPERTS
    x = jnp.where(valid, scores + bias, neg)
    m1 = _group_all(x, jnp.maximum)
    i1 = _group_all(jnp.where(x == m1, lane_f, far), jnp.minimum)
    x2 = jnp.where(lane_f == i1, neg, x)
    m2 = _group_all(x2, jnp.maximum)
    i2 = _group_all(jnp.where(x2 == m2, lane_f, far), jnp.minimum)
    gs = jnp.where(valid, m1 + m2, neg)
    gmax = jnp.max(gs, axis=-1, keepdims=True)
    g0 = jnp.min(jnp.where(gs == gmax, lane_f, far), axis=-1, keepdims=True)
    in_best = jnp.where(lane_f >= g0, jnp.where(lane_f < g0 + EXPERTS_PER_GROUP, 1.0, 0.0), 0.0)
    sel1 = jnp.where(lane_f == i1, in_best, 0.0)
    sel2 = jnp.where(lane_f == i2, in_best, 0.0)
    both = sel1 + sel2
    wsum = jnp.sum(both * scores, axis=-1, keepdims=True)
    wmat = both * scores / wsum
    return wmat + pltpu.roll(sel1 + 2.0 * sel2, N_EXPERTS, 1)


def _pack_rows(v):
    w = v.shape[1] // 2
    bits = lax.bitcast_convert_type(v.astype(BF16).astype(F32), jnp.uint32)
    return bits[:, :w] | (bits[:, w:] >> 16)


def _unpack_rows(p):
    hi = lax.bitcast_convert_type(p & jnp.uint32(0xFFFF0000), F32)
    lo = lax.bitcast_convert_type(p << 16, F32)
    return hi, lo


def _merge_body(x_ref, mod_ref, g1_ref, g2_ref, ya_ref, yb_ref, yc_ref, yd_ref,
                wm_ref, bm_ref, wb_ref, wo_ref, rwh_ref, rwl_ref, rb_ref, xo_ref, h2_ref, sc_ref):
    d = x_ref.shape[1]
    x = x_ref[...]
    h = (_rms(x, g1_ref[...]) * (1.0 + mod_ref[1:2, :]) + mod_ref[0:1, :]).astype(BF16)
    acc = jnp.zeros(x.shape, F32)
    for n, y_ref in enumerate((ya_ref, yb_ref, yc_ref, yd_ref)):
        gate = jax.nn.sigmoid(jnp.dot(h, wm_ref[:, n * d:(n + 1) * d], preferred_element_type=F32)
                              + bm_ref[:, n * d:(n + 1) * d])
        acc = acc + gate * jnp.dot(y_ref[...], wb_ref[n], preferred_element_type=F32)
    y = jnp.dot(acc.astype(BF16), wo_ref[...], preferred_element_type=F32)
    xn = x + mod_ref[2:3, :] * y
    xo_ref[...] = xn
    h2 = _rms(xn, g2_ref[...]) * (1.0 + mod_ref[4:5, :]) + mod_ref[3:4, :]
    h2_ref[...] = _pack_rows(h2)
    h2_hi = h2.astype(BF16)
    h2_lo = (h2 - h2_hi.astype(F32)).astype(BF16)
    logits = (jnp.dot(h2_hi, rwh_ref[...], preferred_element_type=F32)
              + jnp.dot(h2_lo, rwh_ref[...], preferred_element_type=F32)
              + jnp.dot(h2_hi, rwl_ref[...], preferred_element_type=F32))
    sc_ref[...] = _route_tile(jax.nn.sigmoid(logits), rb_ref[...])


def _merge(x_all, mods, g1, g2, ys, wm, bm, wb, wo, rwh, rwl, rb, n_ctx, with_ctx):
    b, s, d = x_all.shape
    tile0 = 0 if with_ctx else n_ctx // TM
    nt = s // TM - tile0
    tile = lambda w: pl.BlockSpec((None, TM, w), lambda bb, i: (bb, i + tile0, 0))
    sub = lambda w: pl.BlockSpec((None, TM, w), lambda bb, i: (bb, i, 0))
    so = nt * TM
    return pl.pallas_call(
        _merge_body,
        grid=(b, nt),
        in_specs=[tile(d),
                  pl.BlockSpec((None, None, 6, d), lambda bb, i: (bb, jnp.minimum(i + tile0, 1), 0, 0)),
                  _const_spec((1, d)), _const_spec((1, d)),
                  sub(BRANCH_W), tile(BRANCH_W), tile(BRANCH_W), sub(BRANCH_W),
                  _const_spec(wm.shape), _const_spec(bm.shape), _const_spec(wb.shape),
                  _const_spec(wo.shape), _const_spec(rwh.shape), _const_spec(rwl.shape),
                  _const_spec(rb.shape)],
        out_specs=[sub(d), sub(d // 2), sub(LANE)],
        out_shape=[jax.ShapeDtypeStruct((b, so, d), F32),
                   jax.ShapeDtypeStruct((b, so, d // 2), jnp.uint32),
                   jax.ShapeDtypeStruct((b, so, LANE), F32)],
        compiler_params=_cparams(("arbitrary", "arbitrary")),
        name="merge",
    )(x_all, mods, g1, g2, *ys, wm, bm, wb, wo, rwh, rwl, rb)


def _expert_body(be_ref, nu_ref, x_ref, w1_ref, w3_ref, w2_ref, o_ref):
    j = pl.program_id(0)

    @pl.when(j < nu_ref[0])
    def _():
        half = x_ref.shape[1]
        x_hi, x_lo = _unpack_rows(x_ref[...])
        x_hi = x_hi.astype(BF16)
        x_lo = x_lo.astype(BF16)

        def up(w_ref):
            return (jnp.dot(x_hi, w_ref[0:half, :].astype(BF16), preferred_element_type=F32)
                    + jnp.dot(x_lo, w_ref[half:, :].astype(BF16), preferred_element_type=F32))

        a = up(w1_ref)
        g = up(w3_ref)
        hmid = (a * jax.nn.sigmoid(a) * g).astype(BF16)
        o_ref[...] = _pack_rows(jnp.dot(hmid, w2_ref[...].astype(BF16), preferred_element_type=F32))

    @pl.when(j >= nu_ref[0])
    def _():
        o_ref[...] = jnp.zeros(o_ref.shape, o_ref.dtype)


def _expert_ffn(block_expert, n_used, xb, w1, w3, w2, layer):
    n_rows, dh = xb.shape
    nb = n_rows // EXP_ROWS
    d, ff = w1.shape[-2:]
    wspec = lambda r, c: pl.BlockSpec((None, None, r, c), lambda j, be, nu: (layer, be[j], 0, 0))
    grid_spec = pltpu.PrefetchScalarGridSpec(
        num_scalar_prefetch=2,
        grid=(nb,),
        in_specs=[pl.BlockSpec((EXP_ROWS, dh), lambda j, be, nu: (jnp.minimum(j, nu[0] - 1), 0)),
                  wspec(d, ff), wspec(d, ff), wspec(ff, d)],
        out_specs=pl.BlockSpec((EXP_ROWS, dh), lambda j, be, nu: (j, 0)),
    )
    return pl.pallas_call(
        _expert_body,
        grid_spec=grid_spec,
        out_shape=jax.ShapeDtypeStruct((n_rows, dh), jnp.uint32),
        compiler_params=_cparams(("arbitrary",)),
        name="expert_ffn",
    )(block_expert, n_used, xb, w1, w3, w2)


def _markers(route):
    lane = lax.broadcasted_iota(jnp.int32, route.shape, 1)
    return jnp.where(lane < N_EXPERTS, pltpu.roll(route, N_EXPERTS, 1), 0.0)


def _plan_body(rt_ref, slot_ref, cnt_ref, run_scr, start_scr):
    ph = pl.program_id(0)
    first = jnp.logical_and(pl.program_id(1) == 0, pl.program_id(2) == 0)
    mk = _markers(rt_ref[...])
    p = jnp.where(mk > 0.0, 1.0, 0.0)

    @pl.when(jnp.logical_and(ph == 0, first))
    def _():
        run_scr[...] = jnp.zeros(run_scr.shape, F32)

    @pl.when(ph == 0)
    def _():
        run_scr[...] += jnp.sum(p, axis=0, keepdims=True)

    @pl.when(jnp.logical_and(ph == 1, first))
    def _():
        cnt = run_scr[...]
        cnt_ref[...] = cnt
        padded = jnp.floor((cnt + (EXP_ROWS - 1.0)) * (1.0 / EXP_ROWS)) * EXP_ROWS
        lane = lax.broadcasted_iota(jnp.int32, cnt.shape, 1)
        end = padded
        sh = 1
        while sh < LANE:
            end = end + jnp.where(lane >= sh, pltpu.roll(end, sh, 1), 0.0)
            sh *= 2
        start_scr[...] = end - padded
        run_scr[...] = jnp.zeros(run_scr.shape, F32)

    @pl.when(ph == 1)
    def _():
        tm = p.shape[0]
        ri = lax.broadcasted_iota(jnp.int32, (tm, tm), 0)
        ci = lax.broadcasted_iota(jnp.int32, (tm, tm), 1)
        lower = jnp.where(ri > ci, 1.0, 0.0).astype(BF16)
        before = jnp.dot(lower, p.astype(BF16), preferred_element_type=F32)
        pos = before + run_scr[0:1, :] + start_scr[0:1, :]
        s1 = jnp.sum(jnp.where(mk == 1.0, pos, 0.0), axis=-1, keepdims=True)
        s2 = jnp.sum(jnp.where(mk == 2.0, pos, 0.0), axis=-1, keepdims=True)
        which = lax.broadcasted_iota(jnp.int32, (tm, TOP_K), 1)
        slot_ref[...] = jnp.where(which == 0, s1, s2).astype(jnp.int32)
        run_scr[...] += jnp.sum(p, axis=0, keepdims=True)


def _plan(route):
    b, s, _ = route.shape
    nt = s // TM
    return pl.pallas_call(
        _plan_body,
        grid=(2, b, nt),
        in_specs=[pl.BlockSpec((None, TM, LANE), lambda ph, bb, i: (bb, i, 0))],
        out_specs=[pl.BlockSpec((None, TM, TOP_K), lambda ph, bb, i: (bb * ph, i * ph, 0)),
                   pl.BlockSpec((SUBLANE, LANE), lambda ph, bb, i: (0, 0))],
        out_shape=[jax.ShapeDtypeStruct((b, s, TOP_K), jnp.int32),
                   jax.ShapeDtypeStruct((SUBLANE, LANE), F32)],
        scratch_shapes=[pltpu.VMEM((SUBLANE, LANE), F32), pltpu.VMEM((SUBLANE, LANE), F32)],
        compiler_params=_cparams(("arbitrary", "arbitrary", "arbitrary")),
        name="dispatch_plan",
    )(route)


def _block_experts(counts, n_blocks):
    cnt = counts[0, :N_EXPERTS].astype(jnp.int32)
    padded = (cnt + EXP_ROWS - 1) // EXP_ROWS * EXP_ROWS
    pad_end = jnp.cumsum(padded)
    block_expert = jnp.minimum(
        jnp.searchsorted(pad_end, jnp.arange(n_blocks, dtype=jnp.int32) * EXP_ROWS, side='right'),
        N_EXPERTS - 1).astype(jnp.int32)
    n_used = (pad_end[-1] // EXP_ROWS).astype(jnp.int32).reshape(1)
    return block_expert, n_used, cnt, pad_end.astype(jnp.int32)


def _row_copy(src, src_row, dst, dst_row, sem):
    return pltpu.make_async_copy(src.at[pl.ds(src_row, 1)], dst.at[pl.ds(dst_row, 1)], sem)


def _dispatch_body(slot_ref, cnt_ref, end_ref, h_ref, xb_ref, zero_scr, sem, zsem):
    tm = h_ref.shape[0]

    @pl.when(pl.program_id(0) == 0)
    def _():
        zero_scr[...] = jnp.zeros(zero_scr.shape, zero_scr.dtype)

        def fill(e, carry):
            @pl.when(cnt_ref[e] > 0)
            def _():
                r0 = pl.multiple_of(end_ref[e] - EXP_ROWS, EXP_ROWS)
                pltpu.make_async_copy(zero_scr, xb_ref.at[pl.ds(r0, EXP_ROWS)], zsem).start()
            return carry

        def drain(e, carry):
            @pl.when(cnt_ref[e] > 0)
            def _():
                pltpu.make_async_copy(zero_scr, xb_ref.at[pl.ds(0, EXP_ROWS)], zsem).wait()
            return carry

        lax.fori_loop(0, N_EXPERTS, fill, 0)
        lax.fori_loop(0, N_EXPERTS, drain, 0)

    def issue(t, carry):
        for k in range(TOP_K):
            _row_copy(h_ref, t, xb_ref, slot_ref[0, TOP_K * t + k], sem).start()
        return carry

    lax.fori_loop(0, tm, issue, 0, unroll=8)
    for k in range(TOP_K):
        pltpu.make_async_copy(h_ref, xb_ref.at[pl.ds(0, tm)], sem).wait()


def _dispatch(slot_tiles, cnt, pad_end, h2, n_rows):
    b, s, dh = h2.shape
    nt = b * s // TM
    return pl.pallas_call(
        _dispatch_body,
        grid=(nt,),
        in_specs=[pl.BlockSpec((None, 1, TOP_K * TM), lambda i: (i, 0, 0), memory_space=pltpu.SMEM),
                  pl.BlockSpec(memory_space=pltpu.SMEM), pl.BlockSpec(memory_space=pltpu.SMEM),
                  pl.BlockSpec((TM, dh), lambda i: (i, 0))],
        out_specs=pl.BlockSpec(memory_space=pl.ANY),
        out_shape=jax.ShapeDtypeStruct((n_rows, dh), h2.dtype),
        scratch_shapes=[pltpu.VMEM((EXP_ROWS, dh), h2.dtype),
                        pltpu.SemaphoreType.DMA(()), pltpu.SemaphoreType.DMA(())],
        compiler_params=_cparams(("arbitrary",)),
        name="dispatch",
    )(slot_tiles, cnt, pad_end, h2.reshape(b * s, dh))


def _resid_body(slot_ref, x_ref, mod_ref, rt_ref, gf_ref, y_hbm, o_ref, ybuf0, ybuf1, sem, *, final):
    tm = x_ref.shape[0]
    ybuf = (ybuf0, ybuf1)

    def issue(t, carry):
        for k in range(TOP_K):
            _row_copy(y_hbm, slot_ref[0, TOP_K * t + k], ybuf[k], t, sem).start()
        return carry

    lax.fori_loop(0, tm, issue, 0, unroll=8)
    rt = rt_ref[...]
    mk = _markers(rt)
    w1 = jnp.sum(jnp.where(mk == 1.0, rt, 0.0), axis=-1, keepdims=True)
    w2 = jnp.sum(jnp.where(mk == 2.0, rt, 0.0), axis=-1, keepdims=True)
    for k in range(TOP_K):
        pltpu.make_async_copy(y_hbm.at[pl.ds(0, tm)], ybuf[k], sem).wait()
    a_hi, a_lo = _unpack_rows(ybuf0[...])
    b_hi, b_lo = _unpack_rows(ybuf1[...])
    f = jnp.concatenate([w1 * a_hi + w2 * b_hi, w1 * a_lo + w2 * b_lo], axis=1)
    xn = x_ref[...] + mod_ref[5:6, :] * f
    if final:
        xn = _rms(xn, gf_ref[...])
    o_ref[...] = xn


def _moe_residual(slot_tiles, x_all, mods, route, gf, yblk, has_ctx, final):
    b, s, d = x_all.shape
    nt = s // TM
    tin = lambda ww: pl.BlockSpec((None, TM, ww), lambda bb, i: (bb, i, 0))
    mod_row = (lambda i: jnp.minimum(i, 1)) if has_ctx else (lambda i: 1)
    return pl.pallas_call(
        functools.partial(_resid_body, final=final),
        grid=(b, nt),
        in_specs=[pl.BlockSpec((None, 1, TOP_K * TM), lambda bb, i: (bb * nt + i, 0, 0),
                               memory_space=pltpu.SMEM),
                  tin(d),
                  pl.BlockSpec((None, None, 6, d), lambda bb, i: (bb, mod_row(i), 0, 0)),
                  tin(LANE), _const_spec((1, d)),
                  pl.BlockSpec(memory_space=pl.ANY)],
        out_specs=tin(d),
        out_shape=jax.ShapeDtypeStruct((b, s, d), F32),
        scratch_shapes=[pltpu.VMEM((TM, d // 2), jnp.uint32), pltpu.VMEM((TM, d // 2), jnp.uint32),
                        pltpu.SemaphoreType.DMA(())],
        compiler_params=_cparams(("arbitrary", "arbitrary")),
        name="moe_residual",
    )(slot_tiles, x_all, mods, route, gf, yblk)


def kernel(x, c, ctx, c_ctx, ada_w, ada_b, norm1_g, norm2_g, w_in, mla_q_norm, mla_kv_norm, mla_w_uq,
           mla_w_ukv, ret_decay, ret_norm, lru_conv_w, lru_conv_b, lru_w_a, lru_b_a, lru_w_x, lru_b_x,
           lru_lambda, gqa_q_norm, gqa_k_norm, w_branch, w_merge, b_merge, w_out, router_w, router_bias,
           moe_w1, moe_w3, moe_w2, final_norm):
    b, seq, d = x.shape
    n_ctx = ctx.shape[1]
    depth = ada_w.shape[0]
    s = n_ctx + seq
    assert n_ctx == TM and seq % TM == 0 and seq % GRID_W == 0

    r_pad = -(-(b + 1) // SUBLANE) * SUBLANE
    cc = jnp.zeros((r_pad, d), F32).at[:b].set(c).at[b].set(c_ctx)
    mods_all = _ada_mods(cc, ada_w, ada_b)

    mla_cos, mla_sin = _rope_slot_tables(n_ctx, seq, MLA_ROPE, MLA_NOPE)
    hd_cos, hd_sin = _rope_slot_tables(n_ctx, seq, GQA_HEAD_DIM, 0)

    in_cols = _in_proj_columns()
    uq_cols, ukv_cols = _mla_up_columns()
    rw = jnp.concatenate([router_w.astype(F32), jnp.zeros((d, LANE - N_EXPERTS), F32)], axis=1)
    rw_hi = rw.astype(BF16)
    rw_lo = (rw - rw_hi.astype(F32)).astype(BF16)
    rb = jnp.concatenate([router_bias.astype(F32), jnp.zeros((LANE - N_EXPERTS,), F32)])[None]

    x_all = jnp.concatenate([ctx, x], axis=1)
    out = None
    for l in range(depth):
        last = l == depth - 1
        m = mods_all[l].reshape(r_pad, 6, d)
        mods = jnp.stack([jnp.broadcast_to(m[b], (b, 6, d)), m[:b]], axis=1)

        mla_scale = (MLA_NOPE + MLA_ROPE) ** -0.5
        gqa_scale = GQA_HEAD_DIM ** -0.5

        def swap_gain(g):
            g4 = g.reshape(2, 2, GQA_HEAD_DIM // 4)
            return g4[:, ::-1, :].reshape(GQA_HEAD_DIM)

        def gain_slot(g):
            return jnp.concatenate([g, jnp.zeros((LANE - GQA_HEAD_DIM,), F32)])

        gqc, gqs = gain_slot(gqa_q_norm[l]), gain_slot(swap_gain(gqa_q_norm[l]))
        gkc, gks = gain_slot(gqa_k_norm[l]), gain_slot(swap_gain(gqa_k_norm[l]))
        tabs = jnp.stack([
            mla_cos * mla_scale, mla_sin * mla_scale, mla_cos, mla_sin,
            hd_cos, hd_sin, hd_cos * RET_DK ** -0.5, hd_sin * RET_DK ** -0.5,
            hd_cos * gqc * gqa_scale, hd_sin * gqs * gqa_scale, hd_cos * gkc, hd_sin * gks])

        win_p = _take_cols(w_in[l], in_cols).astype(BF16)
        wuq_p = _take_cols(mla_w_uq[l], uq_cols).astype(BF16)
        wukv_p = _take_cols(mla_w_ukv[l], ukv_cols).astype(BF16)

        (mq, mk, mv, rq, rk, rv, rg, lu, lg, gq, gk, gv) = _inproj(
            x_all, mods, norm1_g[l][None], tabs, win_p, wuq_p, wukv_p,
            mla_q_norm[l][None], mla_kv_norm[l][None])

        ya = _attention(mq, mk, mv, MLA_HEADS, MLA_HEADS, n_ctx, not last)
        yd = _attention(gq, gk, gv, GQA_HEADS, GQA_KV_HEADS, n_ctx, not last)

        log_g = -jax.nn.softplus(-ret_decay[l].astype(F32))
        yb = _retention(log_g, rq, rk, rv, rg, ret_norm[l][None], n_ctx)

        eye = jnp.eye(LRU_BLOCKS, dtype=F32)

        def block_diag(wblk):
            return jnp.einsum('ncd,nm->ncmd', wblk, eye).reshape(LRU_WIDTH, LRU_WIDTH)

        wcat = jnp.concatenate([block_diag(lru_w_a[l, 0]), block_diag(lru_w_x[l, 0]),
                                block_diag(lru_w_a[l, 1]), block_diag(lru_w_x[l, 1])], axis=1).astype(BF16)
        bcat = jnp.concatenate([lru_b_a[l, 0], lru_b_x[l, 0], lru_b_a[l, 1], lru_b_x[l, 1]])[None]
        yc = _rglru(lu, lg, wcat, bcat, lru_conv_w[l], lru_conv_b[l][None], lru_lambda[l], n_ctx)

        x_all, h2, route = _merge(
            x_all, mods, norm1_g[l][None], norm2_g[l][None], (ya, yb, yc, yd),
            w_merge[l].astype(BF16), b_merge[l][None], w_branch[l].astype(BF16), w_out[l].astype(BF16),
            rw_hi, rw_lo, rb, n_ctx, not last)

        t = b * (seq if last else s)
        n_blocks = -(-t * TOP_K // EXP_ROWS) + N_EXPERTS
        slot, counts = _plan(route)
        slot_tiles = slot.reshape(t // TM, 1, TOP_K * TM)
        block_expert, n_used, cnt, pad_end = _block_experts(counts, n_blocks)
        xb = _dispatch(slot_tiles, cnt, pad_end, h2, n_blocks * EXP_ROWS)
        yblk = _expert_ffn(block_expert, n_used, xb, moe_w1, moe_w3, moe_w2, l)
        out = _moe_residual(slot_tiles, x_all, mods, route, final_norm[None], yblk, not last, last)
        x_all = out
    return out
```

```python
import functools

import numpy as np
import jax
import jax.numpy as jnp
from jax import lax
from jax.experimental import pallas as pl
from jax.experimental.pallas import tpu as pltpu

F32 = jnp.float32
BF16 = jnp.bfloat16

LANE = 128
SUBLANE = 8
VMEM_LIMIT = 56 * 1024 * 1024

GRID_W = 64
ROPE_BASE = 10000.0
EPS = 1e-6
MLA_HEADS, MLA_NOPE, MLA_ROPE, MLA_V = 8, 64, 32, 64
MLA_Q_LORA, MLA_KV_LORA = 256, 128
RET_HEADS, RET_DK, RET_DV, RET_CHUNK = 4, 64, 128, 128
LRU_WIDTH, LRU_BLOCKS, LRU_C, CONV_W, CONV_PAD_LEFT = 512, 8, 8.0, 4, 2
LRU_BLOCK_W = LRU_WIDTH // LRU_BLOCKS
GQA_HEADS, GQA_KV_HEADS, GQA_HEAD_DIM = 8, 2, 64
N_BRANCH, BRANCH_W = 4, 512
N_EXPERTS, N_GROUPS, TOP_K, EXPERT_FF, MOE_BLOCK = 64, 8, 2, 256, 128
EXPERTS_PER_GROUP = N_EXPERTS // N_GROUPS

IN_SPLITS = (MLA_Q_LORA, MLA_KV_LORA, MLA_ROPE,
             RET_HEADS * RET_DK, RET_HEADS * RET_DK, RET_HEADS * RET_DV, RET_HEADS * RET_DV,
             LRU_WIDTH, LRU_WIDTH,
             GQA_HEADS * GQA_HEAD_DIM, GQA_KV_HEADS * GQA_HEAD_DIM, GQA_KV_HEADS * GQA_HEAD_DIM)
IN_OFF = tuple(int(o) for o in np.cumsum((0,) + IN_SPLITS))
D_IN = IN_OFF[-1]

TM = 256
EXP_ROWS = 256
HEAD_SLOT = LANE
ONE_LANE = 64

ZP_CQ, ZP_CKV, ZP_KR = 0, 256, 384
ZP_RQ, ZP_RK, ZP_RV, ZP_RG = 512, 1024, 1536, 2048
ZP_LU, ZP_LG = 2560, 3072
ZP_GQ, ZP_GK, ZP_GV = 3584, 4608, 4864
ZP_W = 5120


def _cparams(sem):
    return pltpu.CompilerParams(dimension_semantics=sem, vmem_limit_bytes=VMEM_LIMIT)


def _const_spec(shape):
    nd = len(shape)
    return pl.BlockSpec(shape, lambda *_: (0,) * nd, pipeline_mode=pl.Buffered(1))


def _in_proj_columns():
    idx = np.full((ZP_W,), D_IN, np.int64)
    o = IN_OFF
    idx[ZP_CQ:ZP_CQ + 256] = o[0] + np.arange(256)
    idx[ZP_CKV:ZP_CKV + 128] = o[1] + np.arange(128)
    idx[ZP_KR + MLA_NOPE:ZP_KR + MLA_NOPE + MLA_ROPE] = o[2] + np.arange(MLA_ROPE)
    for h in range(RET_HEADS):
        idx[ZP_RQ + h * 128:ZP_RQ + h * 128 + 64] = o[3] + h * 64 + np.arange(64)
        idx[ZP_RK + h * 128:ZP_RK + h * 128 + 64] = o[4] + h * 64 + np.arange(64)
    idx[ZP_RV:ZP_RV + 512] = o[5] + np.arange(512)
    idx[ZP_RG:ZP_RG + 512] = o[6] + np.arange(512)
    idx[ZP_LU:ZP_LU + 512] = o[7] + np.arange(512)
    idx[ZP_LG:ZP_LG + 512] = o[8] + np.arange(512)
    for h in range(GQA_HEADS):
        idx[ZP_GQ + h * 128:ZP_GQ + h * 128 + 64] = o[9] + h * 64 + np.arange(64)
    for h in range(GQA_KV_HEADS):
        idx[ZP_GK + h * 128:ZP_GK + h * 128 + 64] = o[10] + h * 64 + np.arange(64)
        idx[ZP_GV + h * 128:ZP_GV + h * 128 + 64] = o[11] + h * 64 + np.arange(64)
    return idx


def _mla_up_columns():
    dq = MLA_NOPE + MLA_ROPE
    dkv = MLA_NOPE + MLA_V
    qi = np.full((MLA_HEADS * 128,), MLA_HEADS * dq, np.int64)
    ki = np.full((MLA_HEADS * 128,), MLA_HEADS * dkv, np.int64)
    vi = np.full((MLA_HEADS * 128,), MLA_HEADS * dkv, np.int64)
    for h in range(MLA_HEADS):
        qi[h * 128:h * 128 + dq] = h * dq + np.arange(dq)
        ki[h * 128:h * 128 + MLA_NOPE] = h * dkv + np.arange(MLA_NOPE)
        vi[h * 128:h * 128 + MLA_V] = h * dkv + MLA_NOPE + np.arange(MLA_V)
    return qi, np.concatenate([ki, vi])


def _take_cols(w, idx):
    wz = jnp.concatenate([w, jnp.zeros((w.shape[0], 1), w.dtype)], axis=1)
    return jnp.take(wz, jnp.asarray(idx, jnp.int32), axis=1)


def _rope_slot_tables(n_ctx, seq, rot_dim, lane0):
    half = rot_dim // 2
    q = half // 2
    pos = jnp.arange(seq, dtype=jnp.int32)
    rows = (pos // GRID_W).astype(F32)
    cols = (pos % GRID_W).astype(F32)
    inv = ROPE_BASE ** (-jnp.arange(0, half, 2, dtype=F32) / half)
    ar = rows[:, None] * inv
    ac = cols[:, None] * inv
    cos = jnp.concatenate([jnp.cos(ar), jnp.cos(ar), jnp.cos(ac), jnp.cos(ac)], axis=1)
    sin = jnp.concatenate([-jnp.sin(ar), jnp.sin(ar), -jnp.sin(ac), jnp.sin(ac)], axis=1)
    assert cos.shape[1] == rot_dim and q * 4 == rot_dim
    cos_t = jnp.ones((n_ctx + seq, LANE), F32).at[n_ctx:, lane0:lane0 + rot_dim].set(cos)
    sin_t = jnp.zeros((n_ctx + seq, LANE), F32).at[n_ctx:, lane0:lane0 + rot_dim].set(sin)
    return cos_t, sin_t


def _swap_lanes(x, blk):
    n = x.shape[-1]
    lane = lax.broadcasted_iota(jnp.int32, x.shape, x.ndim - 1)
    up = pltpu.roll(x, n - blk, x.ndim - 1)
    dn = pltpu.roll(x, blk, x.ndim - 1)
    return jnp.where((lane % (2 * blk)) < blk, up, dn)


def _rms(x, g):
    return x * lax.rsqrt(jnp.mean(x * x, axis=-1, keepdims=True) + EPS) * g


def _ada_body(c_ref, w_ref, b_ref, o_ref):
    c = c_ref[...]
    s = (c * jax.nn.sigmoid(c)).astype(BF16)
    o_ref[...] = jnp.dot(s, w_ref[...].astype(BF16), preferred_element_type=F32) + b_ref[...]


def _ada_mods(cc, ada_w, ada_b):
    depth, d, n = ada_w.shape
    r = cc.shape[0]
    tn = 1536
    return pl.pallas_call(
        _ada_body,
        grid=(depth, n // tn),
        in_specs=[pl.BlockSpec((r, d), lambda l, j: (0, 0)),
                  pl.BlockSpec((None, d, tn), lambda l, j: (l, 0, j)),
                  pl.BlockSpec((None, 1, tn), lambda l, j: (l, 0, j))],
        out_specs=pl.BlockSpec((None, r, tn), lambda l, j: (l, 0, j)),
        out_shape=jax.ShapeDtypeStruct((depth, r, n), F32),
        compiler_params=_cparams(("arbitrary", "arbitrary")),
        name="ada_mods",
    )(cc, ada_w, ada_b.reshape(depth, 1, n))


def _inproj_body(x_ref, mod_ref, g1_ref, tab_ref, win_ref, wuq_ref, wukv_ref, gq_ref, gkv_ref,
                 mq_ref, mk_ref, mv_ref, rq_ref, rk_ref, rv_ref, rg_ref, lu_ref, lg_ref,
                 gq_out, gk_out, gv_out):
    x = x_ref[...]
    shift = mod_ref[0:1, :]
    scale = mod_ref[1:2, :]
    h = _rms(x, g1_ref[...]) * (1.0 + scale) + shift
    hb = h.astype(BF16)

    def proj(c0, c1):
        return jnp.dot(hb, win_ref[:, c0:c1], preferred_element_type=F32)

    lane = lax.broadcasted_iota(jnp.int32, (x.shape[0], LANE), 1)
    one_col = jnp.where(lane == ONE_LANE, 1.0, 0.0).astype(F32)

    def rope(v, ci, blk):
        return v * tab_ref[ci] + _swap_lanes(v, blk) * tab_ref[ci + 1]

    cq = proj(ZP_CQ, ZP_CQ + 256)
    qn = _rms(cq, gq_ref[...]).astype(BF16)
    q = jnp.dot(qn, wuq_ref[...], preferred_element_type=F32)
    for hh in range(MLA_HEADS):
        sl = slice(hh * 128, (hh + 1) * 128)
        mq_ref[:, sl] = rope(q[:, sl], 0, MLA_ROPE // 4).astype(BF16)
    ckv = proj(ZP_CKV, ZP_CKV + 128)
    kvn = _rms(ckv, gkv_ref[...]).astype(BF16)
    kv = jnp.dot(kvn, wukv_ref[...], preferred_element_type=F32)
    kr = rope(proj(ZP_KR, ZP_KR + 128), 2, MLA_ROPE // 4)
    for hh in range(MLA_HEADS):
        sl = slice(hh * 128, (hh + 1) * 128)
        mk_ref[sl, :] = (kv[:, sl] + kr).T.astype(BF16)
        mv_ref[:, sl] = (kv[:, MLA_HEADS * 128 + hh * 128:MLA_HEADS * 128 + (hh + 1) * 128] + one_col).astype(BF16)

    rq = proj(ZP_RQ, ZP_RQ + 512)
    rk = proj(ZP_RK, ZP_RK + 512)
    for hh in range(RET_HEADS):
        sl = slice(hh * 128, (hh + 1) * 128)
        rq_ref[:, sl] = rope(rq[:, sl], 4, RET_DK // 4).astype(BF16)
        rk_ref[:, sl] = rope(rk[:, sl], 6, RET_DK // 4).astype(BF16)
    rv_ref[...] = proj(ZP_RV, ZP_RV + 512).astype(BF16)
    rg_ref[...] = proj(ZP_RG, ZP_RG + 512).astype(BF16)

    lu_ref[...] = proj(ZP_LU, ZP_LU + 512).astype(BF16)
    lg_ref[...] = proj(ZP_LG, ZP_LG + 512).astype(BF16)

    gq = proj(ZP_GQ, ZP_GQ + 1024)
    for hh in range(GQA_HEADS):
        sl = slice(hh * 128, (hh + 1) * 128)
        v = gq[:, sl]
        v = v * lax.rsqrt(jnp.sum(v * v, axis=-1, keepdims=True) * (1.0 / GQA_HEAD_DIM) + EPS)
        gq_out[:, sl] = rope(v, 8, GQA_HEAD_DIM // 4).astype(BF16)
    gk = proj(ZP_GK, ZP_GK + 256)
    gv = proj(ZP_GV, ZP_GV + 256)
    for hh in range(GQA_KV_HEADS):
        sl = slice(hh * 128, (hh + 1) * 128)
        v = gk[:, sl]
        v = v * lax.rsqrt(jnp.sum(v * v, axis=-1, keepdims=True) * (1.0 / GQA_HEAD_DIM) + EPS)
        gk_out[sl, :] = rope(v, 10, GQA_HEAD_DIM // 4).T.astype(BF16)
        gv_out[:, sl] = (gv[:, sl] + one_col).astype(BF16)


def _inproj(x_all, mods, g1, tabs, win_p, wuq_p, wukv_p, gq, gkv):
    b, s, d = x_all.shape
    nt = s // TM
    widths = (1024, 1024, 1024, 512, 512, 512, 512, 512, 512, 1024, 256, 256)
    tile = lambda w: pl.BlockSpec((None, TM, w), lambda i, bb: (bb, i, 0))
    ttile = lambda w: pl.BlockSpec((None, w, TM), lambda i, bb: (bb, 0, i))
    transposed = (1, 10)
    return pl.pallas_call(
        _inproj_body,
        grid=(nt, b),
        in_specs=[tile(d),
                  pl.BlockSpec((None, None, 6, d), lambda i, bb: (bb, jnp.minimum(i, 1), 0, 0)),
                  _const_spec((1, d)),
                  pl.BlockSpec((12, TM, LANE), lambda i, bb: (0, i, 0)),
                  _const_spec(win_p.shape), _const_spec(wuq_p.shape), _const_spec(wukv_p.shape),
                  _const_spec((1, MLA_Q_LORA)), _const_spec((1, MLA_KV_LORA))],
        out_specs=[ttile(w) if j in transposed else tile(w) for j, w in enumerate(widths)],
        out_shape=[jax.ShapeDtypeStruct((b, w, s) if j in transposed else (b, s, w), BF16)
                   for j, w in enumerate(widths)],
        compiler_params=_cparams(("arbitrary", "arbitrary")),
        name="in_proj",
    )(x_all, mods, g1, tabs, win_p, wuq_p, wukv_p, gq, gkv)


def _attn_body(q_ref, k_ref, v_ref, o_ref, *, heads, kv_heads, n_ctx, tile0):
    i = pl.program_id(1) + tile0
    grp = heads // kv_heads

    def run(klen):
        for hp in range(heads // 2):
            outs = []
            for h in (2 * hp, 2 * hp + 1):
                g = h // grp
                q = q_ref[:, h * 128:(h + 1) * 128]
                kt = k_ref[g * 128:(g + 1) * 128, 0:klen]
                v = v_ref[0:klen, g * 128:(g + 1) * 128]
                s = jnp.dot(q, kt, preferred_element_type=F32).astype(BF16)
                m = jnp.max(s, axis=-1, keepdims=True)
                p = jnp.exp(s - m)
                o = jnp.dot(p, v, preferred_element_type=F32)
                outs.append(o[:, :64] / o[:, ONE_LANE:ONE_LANE + 1])
            o_ref[:, hp * 128:(hp + 1) * 128] = jnp.concatenate(outs, axis=1).astype(o_ref.dtype)

    if tile0 == 0:
        @pl.when(i == 0)
        def _():
            run(n_ctx)

        @pl.when(i > 0)
        def _():
            run(v_ref.shape[0])
    else:
        run(v_ref.shape[0])


def _attention(q, k, v, heads, kv_heads, n_ctx, with_ctx):
    b, s, _ = q.shape
    tile0 = 0 if with_ctx else n_ctx // TM
    nq = s // TM - tile0
    body = functools.partial(_attn_body, heads=heads, kv_heads=kv_heads, n_ctx=n_ctx, tile0=tile0)
    return pl.pallas_call(
        body,
        grid=(b, nq),
        in_specs=[pl.BlockSpec((None, TM, heads * 128), lambda bb, i: (bb, i + tile0, 0)),
                  pl.BlockSpec((None, kv_heads * 128, s), lambda bb, i: (bb, 0, 0)),
                  pl.BlockSpec((None, s, kv_heads * 128), lambda bb, i: (bb, 0, 0))],
        out_specs=pl.BlockSpec((None, TM, heads * 64), lambda bb, i: (bb, i, 0)),
        out_shape=jax.ShapeDtypeStruct((b, nq * TM, heads * 64), BF16),
        compiler_params=_cparams(("arbitrary", "arbitrary")),
        name="attention_h%d_kv%d" % (heads, kv_heads),
    )(q, k, v)


RET_ROWS = 256


def _retention_body(lg_ref, q_ref, k_ref, v_ref, g_ref, gn_ref, o_ref,
                    kv_scr, st_scr, dec_scr, m_scr, *, n_ctx):
    c = RET_ROWS
    s = q_ref.shape[0]
    nc = s // c
    nctx = n_ctx // c
    back_order = list(range(nctx - 1, -1, -1)) + list(range(nc - 1, nctx - 1, -1))
    pos = lax.broadcasted_iota(jnp.int32, (c, LANE), 0).astype(F32)
    ri = lax.broadcasted_iota(jnp.int32, (c, c), 0)
    ci = lax.broadcasted_iota(jnp.int32, (c, c), 1)
    diff = (ri - ci).astype(F32)
    heads = [(h, slice(h * 128, (h + 1) * 128)) for h in range(RET_HEADS)]

    for h, _ in heads:
        lgf = lg_ref[0, h]
        lgb = lg_ref[1, h]
        dec_scr[h, 0] = jnp.exp(lgf * (c - 1.0 - pos))
        dec_scr[h, 1] = jnp.exp(lgb * pos)
        dec_scr[h, 2] = jnp.exp(lgf * (pos + 1.0))
        dec_scr[h, 3] = jnp.exp(lgb * (c - pos))
        m_scr[h] = jnp.where(diff >= 0, jnp.exp(lgf * jnp.maximum(diff, 0.0)),
                             jnp.exp(lgb * jnp.maximum(-diff, 0.0)))

    def kv_step(j, carry):
        r0 = pl.multiple_of(j * c, c)
        for h, sl in heads:
            kc = k_ref[pl.ds(r0, c), sl].astype(F32)
            vc = v_ref[pl.ds(r0, c), sl]
            kk = jnp.concatenate([(kc * dec_scr[h, 0]).astype(BF16), (kc * dec_scr[h, 1]).astype(BF16)], axis=1)
            kv_scr[h, j] = lax.dot_general(kk, vc, (((0,), (0,)), ((), ())), preferred_element_type=F32)
        return carry

    lax.fori_loop(0, nc, kv_step, 0, unroll=3)

    for h, _ in heads:
        gcf = jnp.exp(lg_ref[0, h] * c)
        gcb = jnp.exp(lg_ref[1, h] * c)
        sf = jnp.zeros((128, 128), F32)
        for j in range(nc):
            st_scr[h, j, 0:128, :] = sf.astype(BF16)
            sf = sf * gcf + kv_scr[h, j, 0:128, :]
        sb = jnp.zeros((128, 128), F32)
        for j in back_order:
            st_scr[h, j, 128:256, :] = sb.astype(BF16)
            sb = sb * gcb + kv_scr[h, j, 128:256, :]

    def out_step(j, carry):
        r0 = pl.multiple_of(j * c, c)
        for h, sl in heads:
            qb = q_ref[pl.ds(r0, c), sl]
            kb = k_ref[pl.ds(r0, c), sl]
            vc = v_ref[pl.ds(r0, c), sl]
            sc = lax.dot_general(qb, kb, (((1,), (1,)), ((), ())), preferred_element_type=F32) * m_scr[h]
            o = jnp.dot(sc.astype(BF16), vc, preferred_element_type=F32)
            qf = qb.astype(F32)
            qd = jnp.concatenate([(qf * dec_scr[h, 2]).astype(BF16), (qf * dec_scr[h, 3]).astype(BF16)], axis=1)
            o = o + jnp.dot(qd, st_scr[h, j], preferred_element_type=F32)
            mu = jnp.mean(o, axis=-1, keepdims=True)
            oc = o - mu
            var = jnp.mean(oc * oc, axis=-1, keepdims=True)
            y = oc * lax.rsqrt(var + EPS) * gn_ref[:, sl]
            gate = g_ref[pl.ds(r0, c), sl].astype(F32)
            o_ref[pl.ds(r0, c), sl] = (gate * jax.nn.sigmoid(gate) * y).astype(o_ref.dtype)
        return carry

    lax.fori_loop(0, nc, out_step, 0, unroll=3)


def _retention(log_g, q, k, v, g, gn, n_ctx):
    b, s, w = v.shape
    nc = s // RET_ROWS
    blk = lambda ww: pl.BlockSpec((None, s, ww), lambda bb: (bb, 0, 0))
    return pl.pallas_call(
        functools.partial(_retention_body, n_ctx=n_ctx),
        grid=(b,),
        in_specs=[pl.BlockSpec(memory_space=pltpu.SMEM),
                  blk(512), blk(512), blk(512), blk(512), _const_spec((1, 512))],
        out_specs=blk(512),
        out_shape=jax.ShapeDtypeStruct((b, s, 512), BF16),
        scratch_shapes=[pltpu.VMEM((RET_HEADS, nc, 256, 128), F32),
                        pltpu.VMEM((RET_HEADS, nc, 256, 128), BF16),
                        pltpu.VMEM((RET_HEADS, 4, RET_ROWS, LANE), F32),
                        pltpu.VMEM((RET_HEADS, RET_ROWS, RET_ROWS), F32)],
        compiler_params=_cparams(("arbitrary",)),
        name="retention",
    )(log_g, q, k, v, g, gn)


LRU_ROWS = 256
LRU_HALO = 16


def _tile_scan(a, bv, carry, reverse):
    row = lax.broadcasted_iota(jnp.int32, a.shape, 0)
    for dlt in (1, 2, 4):
        if reverse:
            a_s = pltpu.roll(a, SUBLANE - dlt, 0)
            b_s = pltpu.roll(bv, SUBLANE - dlt, 0)
            ok = row < SUBLANE - dlt
        else:
            a_s = pltpu.roll(a, dlt, 0)
            b_s = pltpu.roll(bv, dlt, 0)
            ok = row >= dlt
        bv = jnp.where(ok, a * b_s + bv, bv)
        a = jnp.where(ok, a * a_s, a)
    h = a * carry + bv
    new_carry = h[0:1, :] if reverse else h[SUBLANE - 1:SUBLANE, :]
    return h, new_carry


def _rglru_body(u_ref, g_ref, wc_ref, bc_ref, cw_ref, cb_ref, lam_ref, o_ref, a_scr, b_scr, *, n_ctx):
    s = u_ref.shape[0]
    w = LRU_WIDTH
    r = LRU_ROWS
    cdec = [-LRU_C * jax.nn.softplus(-lam_ref[d:d + 1, :]) for d in range(2)]
    zeros = jnp.zeros((LRU_HALO, w), F32)

    for ch in range(s // r):
        r0 = ch * r
        seg_start = r0 == 0 or r0 == n_ctx
        seg_end = r0 + r == n_ctx or r0 + r == s
        lo = r0 if seg_start else r0 - LRU_HALO
        hi = r0 + r if seg_end else r0 + r + LRU_HALO
        parts = [u_ref[lo:hi, :].astype(F32)]
        if seg_start:
            parts = [zeros] + parts
        if seg_end:
            parts = parts + [zeros]
        ext = jnp.concatenate(parts, axis=0) if len(parts) > 1 else parts[0]
        n = ext.shape[0]
        u = cb_ref[...]
        for j in range(CONV_W):
            sh = (CONV_PAD_LEFT - j) % n
            tap = ext if sh == 0 else pltpu.roll(ext, sh, 0)
            u = u + tap[LRU_HALO:LRU_HALO + r, :] * cw_ref[j:j + 1, :]
        gates = jnp.dot(u.astype(BF16), wc_ref[...], preferred_element_type=F32) + bc_ref[...]
        for d in range(2):
            rg = jax.nn.sigmoid(gates[:, (2 * d) * w:(2 * d + 1) * w])
            ig = jax.nn.sigmoid(gates[:, (2 * d + 1) * w:(2 * d + 2) * w])
            a = jnp.exp(rg * cdec[d])
            a_scr[d, r0:r0 + r, :] = a
            b_scr[d, r0:r0 + r, :] = jnp.sqrt(1.0 - a * a) * (ig * u)

    def fwd(t, carry):
        r0 = pl.multiple_of(t * SUBLANE, SUBLANE)
        h, carry = _tile_scan(a_scr[0, pl.ds(r0, SUBLANE), :], b_scr[0, pl.ds(r0, SUBLANE), :], carry, False)
        b_scr[0, pl.ds(r0, SUBLANE), :] = h
        return carry

    lax.fori_loop(0, s // SUBLANE, fwd, jnp.zeros((1, w), F32), unroll=4)

    def bwd(t0):
        def step(t, carry):
            r0 = pl.multiple_of((t0 - t) * SUBLANE, SUBLANE)
            h, carry = _tile_scan(a_scr[1, pl.ds(r0, SUBLANE), :], b_scr[1, pl.ds(r0, SUBLANE), :], carry, True)
            b_scr[1, pl.ds(r0, SUBLANE), :] = h
            return carry
        return step

    nct = n_ctx // SUBLANE
    carry = lax.fori_loop(0, nct, bwd(nct - 1), jnp.zeros((1, w), F32), unroll=4)
    lax.fori_loop(0, s // SUBLANE - nct, bwd(s // SUBLANE - 1), carry, unroll=4)

    def fin(j, carry):
        r0 = pl.multiple_of(j * r, r)
        hsum = b_scr[0, pl.ds(r0, r), :] + b_scr[1, pl.ds(r0, r), :]
        gate = g_ref[pl.ds(r0, r), :].astype(F32)
        o_ref[pl.ds(r0, r), :] = (hsum * jax.nn.gelu(gate)).astype(o_ref.dtype)
        return carry

    lax.fori_loop(0, s // r, fin, 0)


def _rglru(u, g, wcat, bcat, conv_w, conv_b, lam, n_ctx):
    b, s, w = u.shape
    blk = pl.BlockSpec((None, s, w), lambda bb: (bb, 0, 0))
    return pl.pallas_call(
        functools.partial(_rglru_body, n_ctx=n_ctx),
        grid=(b,),
        in_specs=[blk, blk, _const_spec(wcat.shape), _const_spec(bcat.shape),
                  _const_spec(conv_w.shape), _const_spec(conv_b.shape), _const_spec(lam.shape)],
        out_specs=blk,
        out_shape=jax.ShapeDtypeStruct((b, s, w), BF16),
        scratch_shapes=[pltpu.VMEM((2, s, w), F32), pltpu.VMEM((2, s, w), F32)],
        compiler_params=_cparams(("arbitrary",)),
        name="rglru",
    )(u, g, wcat, bcat, conv_w, conv_b, lam)


def _group_all(v, op):
    blk = 1
    while blk < EXPERTS_PER_GROUP:
        v = op(v, _swap_lanes(v, blk))
        blk *= 2
    return v


def _route_tile(scores, bias):
    neg = jnp.float32(-3.0e38)
    far = jnp.float32(1.0e9)
    lane = lax.broadcasted_iota(jnp.int32, scores.shape, 1)
    lane_f = lane.astype(F32)
    valid = lane < N_EXPERTS
    x = jnp.where(valid, scores + bias, neg)
    m1 = _group_all(x, jnp.maximum)
    i1 = _group_all(jnp.where(x == m1, lane_f, far), jnp.minimum)
    x2 = jnp.where(lane_f == i1, neg, x)
    m2 = _group_all(x2, jnp.maximum)
    i2 = _group_all(jnp.where(x2 == m2, lane_f, far), jnp.minimum)
    gs = jnp.where(valid, m1 + m2, neg)
    gmax = jnp.max(gs, axis=-1, keepdims=True)
    g0 = jnp.min(jnp.where(gs == gmax, lane_f, far), axis=-1, keepdims=True)
    in_best = jnp.where(lane_f >= g0, jnp.where(lane_f < g0 + EXPERTS_PER_GROUP, 1.0, 0.0), 0.0)
    sel1 = jnp.where(lane_f == i1, in_best, 0.0)
    sel2 = jnp.where(lane_f == i2, in_best, 0.0)
    both = sel1 + sel2
    wsum = jnp.sum(both * scores, axis=-1, keepdims=True)
    wmat = both * scores / wsum
    return wmat + pltpu.roll(sel1 + 2.0 * sel2, N_EXPERTS, 1)


def _pack_rows(v):
    w = v.shape[1] // 2
    bits = lax.bitcast_convert_type(v.astype(BF16).astype(F32), jnp.uint32)
    return bits[:, :w] | (bits[:, w:] >> 16)


def _unpack_rows(p):
    hi = lax.bitcast_convert_type(p & jnp.uint32(0xFFFF0000), F32)
    lo = lax.bitcast_convert_type(p << 16, F32)
    return hi, lo


def _merge_body(x_ref, mod_ref, g1_ref, g2_ref, ya_ref, yb_ref, yc_ref, yd_ref,
                wm_ref, bm_ref, wb_ref, wo_ref, rwh_ref, rwl_ref, rb_ref, xo_ref, h2_ref, sc_ref):
    d = x_ref.shape[1]
    x = x_ref[...]
    h = (_rms(x, g1_ref[...]) * (1.0 + mod_ref[1:2, :]) + mod_ref[0:1, :]).astype(BF16)
    acc = jnp.zeros(x.shape, F32)
    for n, y_ref in enumerate((ya_ref, yb_ref, yc_ref, yd_ref)):
        gate = jax.nn.sigmoid(jnp.dot(h, wm_ref[:, n * d:(n + 1) * d], preferred_element_type=F32)
                              + bm_ref[:, n * d:(n + 1) * d])
        acc = acc + gate * jnp.dot(y_ref[...], wb_ref[n], preferred_element_type=F32)
    y = jnp.dot(acc.astype(BF16), wo_ref[...], preferred_element_type=F32)
    xn = x + mod_ref[2:3, :] * y
    xo_ref[...] = xn
    h2 = _rms(xn, g2_ref[...]) * (1.0 + mod_ref[4:5, :]) + mod_ref[3:4, :]
    h2_ref[...] = _pack_rows(h2)
    h2_hi = h2.astype(BF16)
    h2_lo = (h2 - h2_hi.astype(F32)).astype(BF16)
    logits = (jnp.dot(h2_hi, rwh_ref[...], preferred_element_type=F32)
              + jnp.dot(h2_lo, rwh_ref[...], preferred_element_type=F32)
              + jnp.dot(h2_hi, rwl_ref[...], preferred_element_type=F32))
    sc_ref[...] = _route_tile(jax.nn.sigmoid(logits), rb_ref[...])


def _merge(x_all, mods, g1, g2, ys, wm, bm, wb, wo, rwh, rwl, rb, n_ctx, with_ctx):
    b, s, d = x_all.shape
    tile0 = 0 if with_ctx else n_ctx // TM
    nt = s // TM - tile0
    tile = lambda w: pl.BlockSpec((None, TM, w), lambda bb, i: (bb, i + tile0, 0))
    sub = lambda w: pl.BlockSpec((None, TM, w), lambda bb, i: (bb, i, 0))
    so = nt * TM
    return pl.pallas_call(
        _merge_body,
        grid=(b, nt),
        in_specs=[tile(d),
                  pl.BlockSpec((None, None, 6, d), lambda bb, i: (bb, jnp.minimum(i + tile0, 1), 0, 0)),
                  _const_spec((1, d)), _const_spec((1, d)),
                  sub(BRANCH_W), tile(BRANCH_W), tile(BRANCH_W), sub(BRANCH_W),
                  _const_spec(wm.shape), _const_spec(bm.shape), _const_spec(wb.shape),
                  _const_spec(wo.shape), _const_spec(rwh.shape), _const_spec(rwl.shape),
                  _const_spec(rb.shape)],
        out_specs=[sub(d), sub(d // 2), sub(LANE)],
        out_shape=[jax.ShapeDtypeStruct((b, so, d), F32),
                   jax.ShapeDtypeStruct((b, so, d // 2), jnp.uint32),
                   jax.ShapeDtypeStruct((b, so, LANE), F32)],
        compiler_params=_cparams(("arbitrary", "arbitrary")),
        name="merge",
    )(x_all, mods, g1, g2, *ys, wm, bm, wb, wo, rwh, rwl, rb)


def _expert_body(be_ref, nu_ref, x_ref, w1_ref, w3_ref, w2_ref, o_ref):
    j = pl.program_id(0)

    @pl.when(j < nu_ref[0])
    def _():
        half = x_ref.shape[1]
        x_hi, x_lo = _unpack_rows(x_ref[...])
        x_hi = x_hi.astype(BF16)
        x_lo = x_lo.astype(BF16)

        def up(w_ref):
            return (jnp.dot(x_hi, w_ref[0:half, :].astype(BF16), preferred_element_type=F32)
                    + jnp.dot(x_lo, w_ref[half:, :].astype(BF16), preferred_element_type=F32))

        a = up(w1_ref)
        g = up(w3_ref)
        hmid = (a * jax.nn.sigmoid(a) * g).astype(BF16)
        o_ref[...] = _pack_rows(jnp.dot(hmid, w2_ref[...].astype(BF16), preferred_element_type=F32))

    @pl.when(j >= nu_ref[0])
    def _():
        o_ref[...] = jnp.zeros(o_ref.shape, o_ref.dtype)


def _expert_ffn(block_expert, n_used, xb, w1, w3, w2, layer):
    n_rows, dh = xb.shape
    nb = n_rows // EXP_ROWS
    d, ff = w1.shape[-2:]
    wspec = lambda r, c: pl.BlockSpec((None, None, r, c), lambda j, be, nu: (layer, be[j], 0, 0))
    grid_spec = pltpu.PrefetchScalarGridSpec(
        num_scalar_prefetch=2,
        grid=(nb,),
        in_specs=[pl.BlockSpec((EXP_ROWS, dh), lambda j, be, nu: (jnp.minimum(j, nu[0] - 1), 0)),
                  wspec(d, ff), wspec(d, ff), wspec(ff, d)],
        out_specs=pl.BlockSpec((EXP_ROWS, dh), lambda j, be, nu: (j, 0)),
    )
    return pl.pallas_call(
        _expert_body,
        grid_spec=grid_spec,
        out_shape=jax.ShapeDtypeStruct((n_rows, dh), jnp.uint32),
        compiler_params=_cparams(("arbitrary",)),
        name="expert_ffn",
    )(block_expert, n_used, xb, w1, w3, w2)


def _markers(route):
    lane = lax.broadcasted_iota(jnp.int32, route.shape, 1)
    return jnp.where(lane < N_EXPERTS, pltpu.roll(route, N_EXPERTS, 1), 0.0)


def _plan_body(rt_ref, slot_ref, cnt_ref, run_scr, start_scr):
    ph = pl.program_id(0)
    first = jnp.logical_and(pl.program_id(1) == 0, pl.program_id(2) == 0)
    mk = _markers(rt_ref[...])
    p = jnp.where(mk > 0.0, 1.0, 0.0)

    @pl.when(jnp.logical_and(ph == 0, first))
    def _():
        run_scr[...] = jnp.zeros(run_scr.shape, F32)

    @pl.when(ph == 0)
    def _():
        run_scr[...] += jnp.sum(p, axis=0, keepdims=True)

    @pl.when(jnp.logical_and(ph == 1, first))
    def _():
        cnt = run_scr[...]
        cnt_ref[...] = cnt
        padded = jnp.floor((cnt + (EXP_ROWS - 1.0)) * (1.0 / EXP_ROWS)) * EXP_ROWS
        lane = lax.broadcasted_iota(jnp.int32, cnt.shape, 1)
        end = padded
        sh = 1
        while sh < LANE:
            end = end + jnp.where(lane >= sh, pltpu.roll(end, sh, 1), 0.0)
            sh *= 2
        start_scr[...] = end - padded
        run_scr[...] = jnp.zeros(run_scr.shape, F32)

    @pl.when(ph == 1)
    def _():
        tm = p.shape[0]
        ri = lax.broadcasted_iota(jnp.int32, (tm, tm), 0)
        ci = lax.broadcasted_iota(jnp.int32, (tm, tm), 1)
        lower = jnp.where(ri > ci, 1.0, 0.0).astype(BF16)
        before = jnp.dot(lower, p.astype(BF16), preferred_element_type=F32)
        pos = before + run_scr[0:1, :] + start_scr[0:1, :]
        s1 = jnp.sum(jnp.where(mk == 1.0, pos, 0.0), axis=-1, keepdims=True)
        s2 = jnp.sum(jnp.where(mk == 2.0, pos, 0.0), axis=-1, keepdims=True)
        which = lax.broadcasted_iota(jnp.int32, (tm, TOP_K), 1)
        slot_ref[...] = jnp.where(which == 0, s1, s2).astype(jnp.int32)
        run_scr[...] += jnp.sum(p, axis=0, keepdims=True)


def _plan(route):
    b, s, _ = route.shape
    nt = s // TM
    return pl.pallas_call(
        _plan_body,
        grid=(2, b, nt),
        in_specs=[pl.BlockSpec((None, TM, LANE), lambda ph, bb, i: (bb, i, 0))],
        out_specs=[pl.BlockSpec((None, TM, TOP_K), lambda ph, bb, i: (bb * ph, i * ph, 0)),
                   pl.BlockSpec((SUBLANE, LANE), lambda ph, bb, i: (0, 0))],
        out_shape=[jax.ShapeDtypeStruct((b, s, TOP_K), jnp.int32),
                   jax.ShapeDtypeStruct((SUBLANE, LANE), F32)],
        scratch_shapes=[pltpu.VMEM((SUBLANE, LANE), F32), pltpu.VMEM((SUBLANE, LANE), F32)],
        compiler_params=_cparams(("arbitrary", "arbitrary", "arbitrary")),
        name="dispatch_plan",
    )(route)


def _block_experts(counts, n_blocks):
    cnt = counts[0, :N_EXPERTS].astype(jnp.int32)
    padded = (cnt + EXP_ROWS - 1) // EXP_ROWS * EXP_ROWS
    pad_end = jnp.cumsum(padded)
    block_expert = jnp.minimum(
        jnp.searchsorted(pad_end, jnp.arange(n_blocks, dtype=jnp.int32) * EXP_ROWS, side='right'),
        N_EXPERTS - 1).astype(jnp.int32)
    n_used = (pad_end[-1] // EXP_ROWS).astype(jnp.int32).reshape(1)
    return block_expert, n_used, cnt, pad_end.astype(jnp.int32)


def _row_copy(src, src_row, dst, dst_row, sem):
    return pltpu.make_async_copy(src.at[pl.ds(src_row, 1)], dst.at[pl.ds(dst_row, 1)], sem)


def _dispatch_body(slot_ref, cnt_ref, end_ref, nu_ref, h_ref, xb_ref, zero_scr, sem, zsem):
    tm = h_ref.shape[0]

    @pl.when(pl.program_id(0) == 0)
    def _():
        zero_scr[...] = jnp.zeros(zero_scr.shape, zero_scr.dtype)

        def fill(e, carry):
            @pl.when(cnt_ref[e] > 0)
            def _():
                r0 = pl.multiple_of(end_ref[e] - EXP_ROWS, EXP_ROWS)
                pltpu.make_async_copy(zero_scr, xb_ref.at[pl.ds(r0, EXP_ROWS)], zsem).start()
            return carry

        def drain(e, carry):
            @pl.when(cnt_ref[e] > 0)
            def _():
                pltpu.make_async_copy(zero_scr, xb_ref.at[pl.ds(0, EXP_ROWS)], zsem).wait()
            return carry

        def fill_tail(j, carry):
            r0 = pl.multiple_of(j * EXP_ROWS, EXP_ROWS)
            pltpu.make_async_copy(zero_scr, xb_ref.at[pl.ds(r0, EXP_ROWS)], zsem).start()
            return carry

        def drain_tail(j, carry):
            pltpu.make_async_copy(zero_scr, xb_ref.at[pl.ds(0, EXP_ROWS)], zsem).wait()
            return carry

        n_blocks = xb_ref.shape[0] // EXP_ROWS
        lax.fori_loop(0, N_EXPERTS, fill, 0)
        lax.fori_loop(nu_ref[0], n_blocks, fill_tail, 0)
        lax.fori_loop(0, N_EXPERTS, drain, 0)
        lax.fori_loop(nu_ref[0], n_blocks, drain_tail, 0)

    def issue(t, carry):
        for k in range(TOP_K):
            _row_copy(h_ref, t, xb_ref, slot_ref[0, TOP_K * t + k], sem).start(priority=k % 2)
        return carry

    lax.fori_loop(0, tm, issue, 0, unroll=8)
    for k in range(TOP_K):
        pltpu.make_async_copy(h_ref, xb_ref.at[pl.ds(0, tm)], sem).wait()


def _dispatch(slot_tiles, cnt, pad_end, n_used, h2, n_rows):
    b, s, dh = h2.shape
    nt = b * s // TM
    smem = pl.BlockSpec(memory_space=pltpu.SMEM)
    return pl.pallas_call(
        _dispatch_body,
        grid=(nt,),
        in_specs=[pl.BlockSpec((None, 1, TOP_K * TM), lambda i: (i, 0, 0), memory_space=pltpu.SMEM),
                  smem, smem, smem,
                  pl.BlockSpec((TM, dh), lambda i: (i, 0))],
        out_specs=pl.BlockSpec(memory_space=pl.ANY),
        out_shape=jax.ShapeDtypeStruct((n_rows, dh), h2.dtype),
        scratch_shapes=[pltpu.VMEM((EXP_ROWS, dh), h2.dtype),
                        pltpu.SemaphoreType.DMA(()), pltpu.SemaphoreType.DMA(())],
        compiler_params=_cparams(("arbitrary",)),
        name="dispatch",
    )(slot_tiles, cnt, pad_end, n_used, h2.reshape(b * s, dh))


def _resid_body(slot_ref, x_ref, mod_ref, rt_ref, gf_ref, y_hbm, o_ref, ybuf0, ybuf1, sem, *, final):
    tm = x_ref.shape[0]
    ybuf = (ybuf0, ybuf1)

    def issue(t, carry):
        for k in range(TOP_K):
            _row_copy(y_hbm, slot_ref[0, TOP_K * t + k], ybuf[k], t, sem).start(priority=k % 2)
        return carry

    lax.fori_loop(0, tm, issue, 0, unroll=8)
    rt = rt_ref[...]
    mk = _markers(rt)
    w1 = jnp.sum(jnp.where(mk == 1.0, rt, 0.0), axis=-1, keepdims=True)
    w2 = jnp.sum(jnp.where(mk == 2.0, rt, 0.0), axis=-1, keepdims=True)
    for k in range(TOP_K):
        pltpu.make_async_copy(y_hbm.at[pl.ds(0, tm)], ybuf[k], sem).wait()
    a_hi, a_lo = _unpack_rows(ybuf0[...])
    b_hi, b_lo = _unpack_rows(ybuf1[...])
    f = jnp.concatenate([w1 * a_hi + w2 * b_hi, w1 * a_lo + w2 * b_lo], axis=1)
    xn = x_ref[...] + mod_ref[5:6, :] * f
    if final:
        xn = _rms(xn, gf_ref[...])
    o_ref[...] = xn


def _moe_residual(slot_tiles, x_all, mods, route, gf, yblk, has_ctx, final):
    b, s, d = x_all.shape
    nt = s // TM
    tin = lambda ww: pl.BlockSpec((None, TM, ww), lambda bb, i: (bb, i, 0))
    mod_row = (lambda i: jnp.minimum(i, 1)) if has_ctx else (lambda i: 1)
    return pl.pallas_call(
        functools.partial(_resid_body, final=final),
        grid=(b, nt),
        in_specs=[pl.BlockSpec((None, 1, TOP_K * TM), lambda bb, i: (bb * nt + i, 0, 0),
                               memory_space=pltpu.SMEM),
                  tin(d),
                  pl.BlockSpec((None, None, 6, d), lambda bb, i: (bb, mod_row(i), 0, 0)),
                  tin(LANE), _const_spec((1, d)),
                  pl.BlockSpec(memory_space=pl.ANY)],
        out_specs=tin(d),
        out_shape=jax.ShapeDtypeStruct((b, s, d), F32),
        scratch_shapes=[pltpu.VMEM((TM, d // 2), jnp.uint32), pltpu.VMEM((TM, d // 2), jnp.uint32),
                        pltpu.SemaphoreType.DMA(())],
        compiler_params=_cparams(("arbitrary", "arbitrary")),
        name="moe_residual",
    )(slot_tiles, x_all, mods, route, gf, yblk)


def kernel(x, c, ctx, c_ctx, ada_w, ada_b, norm1_g, norm2_g, w_in, mla_q_norm, mla_kv_norm, mla_w_uq,
           mla_w_ukv, ret_decay, ret_norm, lru_conv_w, lru_conv_b, lru_w_a, lru_b_a, lru_w_x, lru_b_x,
           lru_lambda, gqa_q_norm, gqa_k_norm, w_branch, w_merge, b_merge, w_out, router_w, router_bias,
           moe_w1, moe_w3, moe_w2, final_norm):
    b, seq, d = x.shape
    n_ctx = ctx.shape[1]
    depth = ada_w.shape[0]
    s = n_ctx + seq
    assert n_ctx == TM and seq % TM == 0 and seq % GRID_W == 0

    r_pad = -(-(b + 1) // SUBLANE) * SUBLANE
    cc = jnp.zeros((r_pad, d), F32).at[:b].set(c).at[b].set(c_ctx)
    mods_all = _ada_mods(cc, ada_w, ada_b)

    mla_cos, mla_sin = _rope_slot_tables(n_ctx, seq, MLA_ROPE, MLA_NOPE)
    hd_cos, hd_sin = _rope_slot_tables(n_ctx, seq, GQA_HEAD_DIM, 0)

    in_cols = _in_proj_columns()
    uq_cols, ukv_cols = _mla_up_columns()
    rw = jnp.concatenate([router_w.astype(F32), jnp.zeros((d, LANE - N_EXPERTS), F32)], axis=1)
    rw_hi = rw.astype(BF16)
    rw_lo = (rw - rw_hi.astype(F32)).astype(BF16)
    rb = jnp.concatenate([router_bias.astype(F32), jnp.zeros((LANE - N_EXPERTS,), F32)])[None]

    x_all = jnp.concatenate([ctx, x], axis=1)
    out = None
    for l in range(depth):
        last = l == depth - 1
        m = mods_all[l].reshape(r_pad, 6, d)
        mods = jnp.stack([jnp.broadcast_to(m[b], (b, 6, d)), m[:b]], axis=1)

        mla_scale = (MLA_NOPE + MLA_ROPE) ** -0.5
        gqa_scale = GQA_HEAD_DIM ** -0.5

        def swap_gain(g):
            g4 = g.reshape(2, 2, GQA_HEAD_DIM // 4)
            return g4[:, ::-1, :].reshape(GQA_HEAD_DIM)

        def gain_slot(g):
            return jnp.concatenate([g, jnp.zeros((LANE - GQA_HEAD_DIM,), F32)])

        gqc, gqs = gain_slot(gqa_q_norm[l]), gain_slot(swap_gain(gqa_q_norm[l]))
        gkc, gks = gain_slot(gqa_k_norm[l]), gain_slot(swap_gain(gqa_k_norm[l]))
        tabs = jnp.stack([
            mla_cos * mla_scale, mla_sin * mla_scale, mla_cos, mla_sin,
            hd_cos, hd_sin, hd_cos * RET_DK ** -0.5, hd_sin * RET_DK ** -0.5,
            hd_cos * gqc * gqa_scale, hd_sin * gqs * gqa_scale, hd_cos * gkc, hd_sin * gks])

        win_p = _take_cols(w_in[l], in_cols).astype(BF16)
        wuq_p = _take_cols(mla_w_uq[l], uq_cols).astype(BF16)
        wukv_p = _take_cols(mla_w_ukv[l], ukv_cols).astype(BF16)

        (mq, mk, mv, rq, rk, rv, rg, lu, lg, gq, gk, gv) = _inproj(
            x_all, mods, norm1_g[l][None], tabs, win_p, wuq_p, wukv_p,
            mla_q_norm[l][None], mla_kv_norm[l][None])

        ya = _attention(mq, mk, mv, MLA_HEADS, MLA_HEADS, n_ctx, not last)
        yd = _attention(gq, gk, gv, GQA_HEADS, GQA_KV_HEADS, n_ctx, not last)

        log_g = -jax.nn.softplus(-ret_decay[l].astype(F32))
        yb = _retention(log_g, rq, rk, rv, rg, ret_norm[l][None], n_ctx)

        eye = jnp.eye(LRU_BLOCKS, dtype=F32)

        def block_diag(wblk):
            return jnp.einsum('ncd,nm->ncmd', wblk, eye).reshape(LRU_WIDTH, LRU_WIDTH)

        wcat = jnp.concatenate([block_diag(lru_w_a[l, 0]), block_diag(lru_w_x[l, 0]),
                                block_diag(lru_w_a[l, 1]), block_diag(lru_w_x[l, 1])], axis=1).astype(BF16)
        bcat = jnp.concatenate([lru_b_a[l, 0], lru_b_x[l, 0], lru_b_a[l, 1], lru_b_x[l, 1]])[None]
        yc = _rglru(lu, lg, wcat, bcat, lru_conv_w[l], lru_conv_b[l][None], lru_lambda[l], n_ctx)

        x_all, h2, route = _merge(
            x_all, mods, norm1_g[l][None], norm2_g[l][None], (ya, yb, yc, yd),
            w_merge[l].astype(BF16), b_merge[l][None], w_branch[l].astype(BF16), w_out[l].astype(BF16),
            rw_hi, rw_lo, rb, n_ctx, not last)

        t = b * (seq if last else s)
        n_blocks = -(-t * TOP_K // EXP_ROWS) + N_EXPERTS
        slot, counts = _plan(route)
        slot_tiles = slot.reshape(t // TM, 1, TOP_K * TM)
        block_expert, n_used, cnt, pad_end = _block_experts(counts, n_blocks)
        xb = _dispatch(slot_tiles, cnt, pad_end, n_used, h2, n_blocks * EXP_ROWS)
        yblk = _expert_ffn(block_expert, n_used, xb, moe_w1, moe_w3, moe_w2, l)
        out = _moe_residual(slot_tiles, x_all, mods, route, final_norm[None], yblk, not last, last)
        x_all = out
    return out
```

```python
import functools

import numpy as np
import jax
import jax.numpy as jnp
from jax import lax
from jax.experimental import pallas as pl
from jax.experimental.pallas import tpu as pltpu

F32 = jnp.float32
BF16 = jnp.bfloat16

LANE = 128
SUBLANE = 8
VMEM_LIMIT = 56 * 1024 * 1024

GRID_W = 64
ROPE_BASE = 10000.0
EPS = 1e-6
MLA_HEADS, MLA_NOPE, MLA_ROPE, MLA_V = 8, 64, 32, 64
MLA_Q_LORA, MLA_KV_LORA = 256, 128
RET_HEADS, RET_DK, RET_DV, RET_CHUNK = 4, 64, 128, 128
LRU_WIDTH, LRU_BLOCKS, LRU_C, CONV_W, CONV_PAD_LEFT = 512, 8, 8.0, 4, 2
LRU_BLOCK_W = LRU_WIDTH // LRU_BLOCKS
GQA_HEADS, GQA_KV_HEADS, GQA_HEAD_DIM = 8, 2, 64
N_BRANCH, BRANCH_W = 4, 512
N_EXPERTS, N_GROUPS, TOP_K, EXPERT_FF, MOE_BLOCK = 64, 8, 2, 256, 128
EXPERTS_PER_GROUP = N_EXPERTS // N_GROUPS

IN_SPLITS = (MLA_Q_LORA, MLA_KV_LORA, MLA_ROPE,
             RET_HEADS * RET_DK, RET_HEADS * RET_DK, RET_HEADS * RET_DV, RET_HEADS * RET_DV,
             LRU_WIDTH, LRU_WIDTH,
             GQA_HEADS * GQA_HEAD_DIM, GQA_KV_HEADS * GQA_HEAD_DIM, GQA_KV_HEADS * GQA_HEAD_DIM)
IN_OFF = tuple(int(o) for o in np.cumsum((0,) + IN_SPLITS))
D_IN = IN_OFF[-1]

TM = 256
EXP_ROWS = 256
HEAD_SLOT = LANE
ONE_LANE = 64

ZP_CQ, ZP_CKV, ZP_KR = 0, 256, 384
ZP_RQ, ZP_RK, ZP_RV, ZP_RG = 512, 1024, 1536, 2048
ZP_LU, ZP_LG = 2560, 3072
ZP_GQ, ZP_GK, ZP_GV = 3584, 4608, 4864
ZP_W = 5120


def _cparams(sem):
    return pltpu.CompilerParams(dimension_semantics=sem, vmem_limit_bytes=VMEM_LIMIT)


def _const_spec(shape):
    nd = len(shape)
    return pl.BlockSpec(shape, lambda *_: (0,) * nd, pipeline_mode=pl.Buffered(1))


def _in_proj_columns():
    idx = np.full((ZP_W,), D_IN, np.int64)
    o = IN_OFF
    idx[ZP_CQ:ZP_CQ + 256] = o[0] + np.arange(256)
    idx[ZP_CKV:ZP_CKV + 128] = o[1] + np.arange(128)
    idx[ZP_KR + MLA_NOPE:ZP_KR + MLA_NOPE + MLA_ROPE] = o[2] + np.arange(MLA_ROPE)
    for h in range(RET_HEADS):
        idx[ZP_RQ + h * 128:ZP_RQ + h * 128 + 64] = o[3] + h * 64 + np.arange(64)
        idx[ZP_RK + h * 128:ZP_RK + h * 128 + 64] = o[4] + h * 64 + np.arange(64)
    idx[ZP_RV:ZP_RV + 512] = o[5] + np.arange(512)
    idx[ZP_RG:ZP_RG + 512] = o[6] + np.arange(512)
    idx[ZP_LU:ZP_LU + 512] = o[7] + np.arange(512)
    idx[ZP_LG:ZP_LG + 512] = o[8] + np.arange(512)
    for h in range(GQA_HEADS):
        idx[ZP_GQ + h * 128:ZP_GQ + h * 128 + 64] = o[9] + h * 64 + np.arange(64)
    for h in range(GQA_KV_HEADS):
        idx[ZP_GK + h * 128:ZP_GK + h * 128 + 64] = o[10] + h * 64 + np.arange(64)
        idx[ZP_GV + h * 128:ZP_GV + h * 128 + 64] = o[11] + h * 64 + np.arange(64)
    return idx


def _mla_up_columns():
    dq = MLA_NOPE + MLA_ROPE
    dkv = MLA_NOPE + MLA_V
    qi = np.full((MLA_HEADS * 128,), MLA_HEADS * dq, np.int64)
    ki = np.full((MLA_HEADS * 128,), MLA_HEADS * dkv, np.int64)
    vi = np.full((MLA_HEADS * 128,), MLA_HEADS * dkv, np.int64)
    for h in range(MLA_HEADS):
        qi[h * 128:h * 128 + dq] = h * dq + np.arange(dq)
        ki[h * 128:h * 128 + MLA_NOPE] = h * dkv + np.arange(MLA_NOPE)
        vi[h * 128:h * 128 + MLA_V] = h * dkv + MLA_NOPE + np.arange(MLA_V)
    return qi, np.concatenate([ki, vi])


def _take_cols(w, idx):
    wz = jnp.concatenate([w, jnp.zeros((w.shape[0], 1), w.dtype)], axis=1)
    return jnp.take(wz, jnp.asarray(idx, jnp.int32), axis=1)


def _rope_slot_tables(n_ctx, seq, rot_dim, lane0):
    half = rot_dim // 2
    q = half // 2
    pos = jnp.arange(seq, dtype=jnp.int32)
    rows = (pos // GRID_W).astype(F32)
    cols = (pos % GRID_W).astype(F32)
    inv = ROPE_BASE ** (-jnp.arange(0, half, 2, dtype=F32) / half)
    ar = rows[:, None] * inv
    ac = cols[:, None] * inv
    cos = jnp.concatenate([jnp.cos(ar), jnp.cos(ar), jnp.cos(ac), jnp.cos(ac)], axis=1)
    sin = jnp.concatenate([-jnp.sin(ar), jnp.sin(ar), -jnp.sin(ac), jnp.sin(ac)], axis=1)
    assert cos.shape[1] == rot_dim and q * 4 == rot_dim
    cos_t = jnp.ones((n_ctx + seq, LANE), F32).at[n_ctx:, lane0:lane0 + rot_dim].set(cos)
    sin_t = jnp.zeros((n_ctx + seq, LANE), F32).at[n_ctx:, lane0:lane0 + rot_dim].set(sin)
    return cos_t, sin_t


def _swap_lanes(x, blk):
    n = x.shape[-1]
    lane = lax.broadcasted_iota(jnp.int32, x.shape, x.ndim - 1)
    up = pltpu.roll(x, n - blk, x.ndim - 1)
    dn = pltpu.roll(x, blk, x.ndim - 1)
    return jnp.where((lane % (2 * blk)) < blk, up, dn)


def _rms(x, g):
    return x * lax.rsqrt(jnp.mean(x * x, axis=-1, keepdims=True) + EPS) * g


def _ada_body(c_ref, w_ref, b_ref, o_ref):
    c = c_ref[...]
    s = (c * jax.nn.sigmoid(c)).astype(BF16)
    o_ref[...] = jnp.dot(s, w_ref[...].astype(BF16), preferred_element_type=F32) + b_ref[...]


def _ada_mods(cc, ada_w, ada_b):
    depth, d, n = ada_w.shape
    r = cc.shape[0]
    tn = 1536
    return pl.pallas_call(
        _ada_body,
        grid=(depth, n // tn),
        in_specs=[pl.BlockSpec((r, d), lambda l, j: (0, 0)),
                  pl.BlockSpec((None, d, tn), lambda l, j: (l, 0, j)),
                  pl.BlockSpec((None, 1, tn), lambda l, j: (l, 0, j))],
        out_specs=pl.BlockSpec((None, r, tn), lambda l, j: (l, 0, j)),
        out_shape=jax.ShapeDtypeStruct((depth, r, n), F32),
        compiler_params=_cparams(("arbitrary", "arbitrary")),
        name="ada_mods",
    )(cc, ada_w, ada_b.reshape(depth, 1, n))


def _inproj_body(x_ref, mod_ref, g1_ref, tab_ref, win_ref, wuq_ref, wukv_ref, gq_ref, gkv_ref,
                 mq_ref, mk_ref, mv_ref, rq_ref, rk_ref, rv_ref, rg_ref, lu_ref, lg_ref,
                 gq_out, gk_out, gv_out):
    x = x_ref[...]
    shift = mod_ref[0:1, :]
    scale = mod_ref[1:2, :]
    h = _rms(x, g1_ref[...]) * (1.0 + scale) + shift
    hb = h.astype(BF16)

    def proj(c0, c1):
        return jnp.dot(hb, win_ref[:, c0:c1], preferred_element_type=F32)

    lane = lax.broadcasted_iota(jnp.int32, (x.shape[0], LANE), 1)
    one_col = jnp.where(lane == ONE_LANE, 1.0, 0.0).astype(F32)

    def rope(v, ci, blk):
        return v * tab_ref[ci] + _swap_lanes(v, blk) * tab_ref[ci + 1]

    cq = proj(ZP_CQ, ZP_CQ + 256)
    qn = _rms(cq, gq_ref[...]).astype(BF16)
    q = jnp.dot(qn, wuq_ref[...], preferred_element_type=F32)
    for hh in range(MLA_HEADS):
        sl = slice(hh * 128, (hh + 1) * 128)
        mq_ref[:, sl] = rope(q[:, sl], 0, MLA_ROPE // 4).astype(BF16)
    ckv = proj(ZP_CKV, ZP_CKV + 128)
    kvn = _rms(ckv, gkv_ref[...]).astype(BF16)
    kv = jnp.dot(kvn, wukv_ref[...], preferred_element_type=F32)
    kr = rope(proj(ZP_KR, ZP_KR + 128), 2, MLA_ROPE // 4)
    for hh in range(MLA_HEADS):
        sl = slice(hh * 128, (hh + 1) * 128)
        mk_ref[sl, :] = (kv[:, sl] + kr).T.astype(BF16)
        mv_ref[:, sl] = (kv[:, MLA_HEADS * 128 + hh * 128:MLA_HEADS * 128 + (hh + 1) * 128] + one_col).astype(BF16)

    rq = proj(ZP_RQ, ZP_RQ + 512)
    rk = proj(ZP_RK, ZP_RK + 512)
    for hh in range(RET_HEADS):
        sl = slice(hh * 128, (hh + 1) * 128)
        rq_ref[:, sl] = rope(rq[:, sl], 4, RET_DK // 4).astype(BF16)
        rk_ref[:, sl] = rope(rk[:, sl], 6, RET_DK // 4).astype(BF16)
    rv_ref[...] = proj(ZP_RV, ZP_RV + 512).astype(BF16)
    rg_ref[...] = proj(ZP_RG, ZP_RG + 512).astype(BF16)

    lu_ref[...] = proj(ZP_LU, ZP_LU + 512).astype(BF16)
    lg_ref[...] = proj(ZP_LG, ZP_LG + 512).astype(BF16)

    gq = proj(ZP_GQ, ZP_GQ + 1024)
    for hh in range(GQA_HEADS):
        sl = slice(hh * 128, (hh + 1) * 128)
        v = gq[:, sl]
        v = v * lax.rsqrt(jnp.sum(v * v, axis=-1, keepdims=True) * (1.0 / GQA_HEAD_DIM) + EPS)
        gq_out[:, sl] = rope(v, 8, GQA_HEAD_DIM // 4).astype(BF16)
    gk = proj(ZP_GK, ZP_GK + 256)
    gv = proj(ZP_GV, ZP_GV + 256)
    for hh in range(GQA_KV_HEADS):
        sl = slice(hh * 128, (hh + 1) * 128)
        v = gk[:, sl]
        v = v * lax.rsqrt(jnp.sum(v * v, axis=-1, keepdims=True) * (1.0 / GQA_HEAD_DIM) + EPS)
        gk_out[sl, :] = rope(v, 10, GQA_HEAD_DIM // 4).T.astype(BF16)
        gv_out[:, sl] = (gv[:, sl] + one_col).astype(BF16)


def _inproj(x_all, mods, g1, tabs, win_p, wuq_p, wukv_p, gq, gkv):
    b, s, d = x_all.shape
    nt = s // TM
    widths = (1024, 1024, 1024, 512, 512, 512, 512, 512, 512, 1024, 256, 256)
    tile = lambda w: pl.BlockSpec((None, TM, w), lambda i, bb: (bb, i, 0))
    ttile = lambda w: pl.BlockSpec((None, w, TM), lambda i, bb: (bb, 0, i))
    transposed = (1, 10)
    return pl.pallas_call(
        _inproj_body,
        grid=(nt, b),
        in_specs=[tile(d),
                  pl.BlockSpec((None, None, 6, d), lambda i, bb: (bb, jnp.minimum(i, 1), 0, 0)),
                  _const_spec((1, d)),
                  pl.BlockSpec((12, TM, LANE), lambda i, bb: (0, i, 0)),
                  _const_spec(win_p.shape), _const_spec(wuq_p.shape), _const_spec(wukv_p.shape),
                  _const_spec((1, MLA_Q_LORA)), _const_spec((1, MLA_KV_LORA))],
        out_specs=[ttile(w) if j in transposed else tile(w) for j, w in enumerate(widths)],
        out_shape=[jax.ShapeDtypeStruct((b, w, s) if j in transposed else (b, s, w), BF16)
                   for j, w in enumerate(widths)],
        compiler_params=_cparams(("arbitrary", "arbitrary")),
        name="in_proj",
    )(x_all, mods, g1, tabs, win_p, wuq_p, wukv_p, gq, gkv)


def _attn_body(q_ref, k_ref, v_ref, o_ref, *, heads, kv_heads, n_ctx, tile0):
    i = pl.program_id(1) + tile0
    grp = heads // kv_heads

    def run(klen):
        for hp in range(heads // 2):
            outs = []
            for h in (2 * hp, 2 * hp + 1):
                g = h // grp
                q = q_ref[:, h * 128:(h + 1) * 128]
                kt = k_ref[g * 128:(g + 1) * 128, 0:klen]
                v = v_ref[0:klen, g * 128:(g + 1) * 128]
                s = jnp.dot(q, kt, preferred_element_type=F32).astype(BF16)
                m = jnp.max(s, axis=-1, keepdims=True)
                p = jnp.exp(s - m)
                o = jnp.dot(p, v, preferred_element_type=F32)
                outs.append(o[:, :64] / o[:, ONE_LANE:ONE_LANE + 1])
            o_ref[:, hp * 128:(hp + 1) * 128] = jnp.concatenate(outs, axis=1).astype(o_ref.dtype)

    if tile0 == 0:
        @pl.when(i == 0)
        def _():
            run(n_ctx)

        @pl.when(i > 0)
        def _():
            run(v_ref.shape[0])
    else:
        run(v_ref.shape[0])


def _attention(q, k, v, heads, kv_heads, n_ctx, with_ctx):
    b, s, _ = q.shape
    tile0 = 0 if with_ctx else n_ctx // TM
    nq = s // TM - tile0
    body = functools.partial(_attn_body, heads=heads, kv_heads=kv_heads, n_ctx=n_ctx, tile0=tile0)
    return pl.pallas_call(
        body,
        grid=(b, nq),
        in_specs=[pl.BlockSpec((None, TM, heads * 128), lambda bb, i: (bb, i + tile0, 0)),
                  pl.BlockSpec((None, kv_heads * 128, s), lambda bb, i: (bb, 0, 0)),
                  pl.BlockSpec((None, s, kv_heads * 128), lambda bb, i: (bb, 0, 0))],
        out_specs=pl.BlockSpec((None, TM, heads * 64), lambda bb, i: (bb, i, 0)),
        out_shape=jax.ShapeDtypeStruct((b, nq * TM, heads * 64), BF16),
        compiler_params=_cparams(("arbitrary", "arbitrary")),
        name="attention_h%d_kv%d" % (heads, kv_heads),
    )(q, k, v)


RET_ROWS = 256


def _retention_body(lg_ref, q_ref, k_ref, v_ref, g_ref, gn_ref, o_ref,
                    kv_scr, st_scr, dec_scr, m_scr, *, n_ctx):
    c = RET_ROWS
    s = q_ref.shape[0]
    nc = s // c
    nctx = n_ctx // c
    back_order = list(range(nctx - 1, -1, -1)) + list(range(nc - 1, nctx - 1, -1))
    pos = lax.broadcasted_iota(jnp.int32, (c, LANE), 0).astype(F32)
    ri = lax.broadcasted_iota(jnp.int32, (c, c), 0)
    ci = lax.broadcasted_iota(jnp.int32, (c, c), 1)
    diff = (ri - ci).astype(F32)
    heads = [(h, slice(h * 128, (h + 1) * 128)) for h in range(RET_HEADS)]

    for h, _ in heads:
        lgf = lg_ref[0, h]
        lgb = lg_ref[1, h]
        dec_scr[h, 0] = jnp.exp(lgf * (c - 1.0 - pos))
        dec_scr[h, 1] = jnp.exp(lgb * pos)
        dec_scr[h, 2] = jnp.exp(lgf * (pos + 1.0))
        dec_scr[h, 3] = jnp.exp(lgb * (c - pos))
        m_scr[h] = jnp.where(diff >= 0, jnp.exp(lgf * jnp.maximum(diff, 0.0)),
                             jnp.exp(lgb * jnp.maximum(-diff, 0.0)))

    def kv_step(j, carry):
        r0 = pl.multiple_of(j * c, c)
        for h, sl in heads:
            kc = k_ref[pl.ds(r0, c), sl].astype(F32)
            vc = v_ref[pl.ds(r0, c), sl]
            kk = jnp.concatenate([(kc * dec_scr[h, 0]).astype(BF16), (kc * dec_scr[h, 1]).astype(BF16)], axis=1)
            kv_scr[h, j] = lax.dot_general(kk, vc, (((0,), (0,)), ((), ())), preferred_element_type=F32)
        return carry

    lax.fori_loop(0, nc, kv_step, 0, unroll=3)

    for h, _ in heads:
        gcf = jnp.exp(lg_ref[0, h] * c)
        gcb = jnp.exp(lg_ref[1, h] * c)
        sf = jnp.zeros((128, 128), F32)
        for j in range(nc):
            st_scr[h, j, 0:128, :] = sf.astype(BF16)
            sf = sf * gcf + kv_scr[h, j, 0:128, :]
        sb = jnp.zeros((128, 128), F32)
        for j in back_order:
            st_scr[h, j, 128:256, :] = sb.astype(BF16)
            sb = sb * gcb + kv_scr[h, j, 128:256, :]

    def out_step(j, carry):
        r0 = pl.multiple_of(j * c, c)
        for h, sl in heads:
            qb = q_ref[pl.ds(r0, c), sl]
            kb = k_ref[pl.ds(r0, c), sl]
            vc = v_ref[pl.ds(r0, c), sl]
            sc = lax.dot_general(qb, kb, (((1,), (1,)), ((), ())), preferred_element_type=F32) * m_scr[h]
            o = jnp.dot(sc.astype(BF16), vc, preferred_element_type=F32)
            qf = qb.astype(F32)
            qd = jnp.concatenate([(qf * dec_scr[h, 2]).astype(BF16), (qf * dec_scr[h, 3]).astype(BF16)], axis=1)
            o = o + jnp.dot(qd, st_scr[h, j], preferred_element_type=F32)
            mu = jnp.mean(o, axis=-1, keepdims=True)
            oc = o - mu
            var = jnp.mean(oc * oc, axis=-1, keepdims=True)
            y = oc * lax.rsqrt(var + EPS) * gn_ref[:, sl]
            gate = g_ref[pl.ds(r0, c), sl].astype(F32)
            o_ref[pl.ds(r0, c), sl] = (gate * jax.nn.sigmoid(gate) * y).astype(o_ref.dtype)
        return carry

    lax.fori_loop(0, nc, out_step, 0, unroll=3)


def _retention(log_g, q, k, v, g, gn, n_ctx):
    b, s, w = v.shape
    nc = s // RET_ROWS
    blk = lambda ww: pl.BlockSpec((None, s, ww), lambda bb: (bb, 0, 0))
    return pl.pallas_call(
        functools.partial(_retention_body, n_ctx=n_ctx),
        grid=(b,),
        in_specs=[pl.BlockSpec(memory_space=pltpu.SMEM),
                  blk(512), blk(512), blk(512), blk(512), _const_spec((1, 512))],
        out_specs=blk(512),
        out_shape=jax.ShapeDtypeStruct((b, s, 512), BF16),
        scratch_shapes=[pltpu.VMEM((RET_HEADS, nc, 256, 128), F32),
                        pltpu.VMEM((RET_HEADS, nc, 256, 128), BF16),
                        pltpu.VMEM((RET_HEADS, 4, RET_ROWS, LANE), F32),
                        pltpu.VMEM((RET_HEADS, RET_ROWS, RET_ROWS), F32)],
        compiler_params=_cparams(("arbitrary",)),
        name="retention",
    )(log_g, q, k, v, g, gn)


LRU_ROWS = 256
LRU_HALO = 16


def _tile_scan(a, bv, carry, reverse):
    row = lax.broadcasted_iota(jnp.int32, a.shape, 0)
    for dlt in (1, 2, 4):
        if reverse:
            a_s = pltpu.roll(a, SUBLANE - dlt, 0)
            b_s = pltpu.roll(bv, SUBLANE - dlt, 0)
            ok = row < SUBLANE - dlt
        else:
            a_s = pltpu.roll(a, dlt, 0)
            b_s = pltpu.roll(bv, dlt, 0)
            ok = row >= dlt
        bv = jnp.where(ok, a * b_s + bv, bv)
        a = jnp.where(ok, a * a_s, a)
    h = a * carry + bv
    new_carry = h[0:1, :] if reverse else h[SUBLANE - 1:SUBLANE, :]
    return h, new_carry


def _rglru_body(u_ref, g_ref, wc_ref, bc_ref, cw_ref, cb_ref, lam_ref, o_ref, a_scr, b_scr, *, n_ctx):
    s = u_ref.shape[0]
    w = LRU_WIDTH
    r = LRU_ROWS
    cdec = [-LRU_C * jax.nn.softplus(-lam_ref[d:d + 1, :]) for d in range(2)]
    zeros = jnp.zeros((LRU_HALO, w), F32)

    for ch in range(s // r):
        r0 = ch * r
        seg_start = r0 == 0 or r0 == n_ctx
        seg_end = r0 + r == n_ctx or r0 + r == s
        lo = r0 if seg_start else r0 - LRU_HALO
        hi = r0 + r if seg_end else r0 + r + LRU_HALO
        parts = [u_ref[lo:hi, :].astype(F32)]
        if seg_start:
            parts = [zeros] + parts
        if seg_end:
            parts = parts + [zeros]
        ext = jnp.concatenate(parts, axis=0) if len(parts) > 1 else parts[0]
        n = ext.shape[0]
        u = cb_ref[...]
        for j in range(CONV_W):
            sh = (CONV_PAD_LEFT - j) % n
            tap = ext if sh == 0 else pltpu.roll(ext, sh, 0)
            u = u + tap[LRU_HALO:LRU_HALO + r, :] * cw_ref[j:j + 1, :]
        gates = jnp.dot(u.astype(BF16), wc_ref[...], preferred_element_type=F32) + bc_ref[...]
        for d in range(2):
            rg = jax.nn.sigmoid(gates[:, (2 * d) * w:(2 * d + 1) * w])
            ig = jax.nn.sigmoid(gates[:, (2 * d + 1) * w:(2 * d + 2) * w])
            a = jnp.exp(rg * cdec[d])
            a_scr[d, r0:r0 + r, :] = a
            b_scr[d, r0:r0 + r, :] = jnp.sqrt(1.0 - a * a) * (ig * u)

    def fwd(t, carry):
        r0 = pl.multiple_of(t * SUBLANE, SUBLANE)
        h, carry = _tile_scan(a_scr[0, pl.ds(r0, SUBLANE), :], b_scr[0, pl.ds(r0, SUBLANE), :], carry, False)
        b_scr[0, pl.ds(r0, SUBLANE), :] = h
        return carry

    lax.fori_loop(0, s // SUBLANE, fwd, jnp.zeros((1, w), F32), unroll=4)

    def bwd(t0):
        def step(t, carry):
            r0 = pl.multiple_of((t0 - t) * SUBLANE, SUBLANE)
            h, carry = _tile_scan(a_scr[1, pl.ds(r0, SUBLANE), :], b_scr[1, pl.ds(r0, SUBLANE), :], carry, True)
            b_scr[1, pl.ds(r0, SUBLANE), :] = h
            return carry
        return step

    nct = n_ctx // SUBLANE
    carry = lax.fori_loop(0, nct, bwd(nct - 1), jnp.zeros((1, w), F32), unroll=4)
    lax.fori_loop(0, s // SUBLANE - nct, bwd(s // SUBLANE - 1), carry, unroll=4)

    def fin(j, carry):
        r0 = pl.multiple_of(j * r, r)
        hsum = b_scr[0, pl.ds(r0, r), :] + b_scr[1, pl.ds(r0, r), :]
        gate = g_ref[pl.ds(r0, r), :].astype(F32)
        o_ref[pl.ds(r0, r), :] = (hsum * jax.nn.gelu(gate)).astype(o_ref.dtype)
        return carry

    lax.fori_loop(0, s // r, fin, 0)


def _rglru(u, g, wcat, bcat, conv_w, conv_b, lam, n_ctx):
    b, s, w = u.shape
    blk = pl.BlockSpec((None, s, w), lambda bb: (bb, 0, 0))
    return pl.pallas_call(
        functools.partial(_rglru_body, n_ctx=n_ctx),
        grid=(b,),
        in_specs=[blk, blk, _const_spec(wcat.shape), _const_spec(bcat.shape),
                  _const_spec(conv_w.shape), _const_spec(conv_b.shape), _const_spec(lam.shape)],
        out_specs=blk,
        out_shape=jax.ShapeDtypeStruct((b, s, w), BF16),
        scratch_shapes=[pltpu.VMEM((2, s, w), F32), pltpu.VMEM((2, s, w), F32)],
        compiler_params=_cparams(("arbitrary",)),
        name="rglru",
    )(u, g, wcat, bcat, conv_w, conv_b, lam)


def _group_all(v, op):
    blk = 1
    while blk < EXPERTS_PER_GROUP:
        v = op(v, _swap_lanes(v, blk))
        blk *= 2
    return v


def _route_tile(scores, bias):
    neg = jnp.float32(-3.0e38)
    far = jnp.float32(1.0e9)
    lane = lax.broadcasted_iota(jnp.int32, scores.shape, 1)
    lane_f = lane.astype(F32)
    valid = lane < N_EXPERTS
    x = jnp.where(valid, scores + bias, neg)
    m1 = _group_all(x, jnp.maximum)
    i1 = _group_all(jnp.where(x == m1, lane_f, far), jnp.minimum)
    x2 = jnp.where(lane_f == i1, neg, x)
    m2 = _group_all(x2, jnp.maximum)
    i2 = _group_all(jnp.where(x2 == m2, lane_f, far), jnp.minimum)
    gs = jnp.where(valid, m1 + m2, neg)
    gmax = jnp.max(gs, axis=-1, keepdims=True)
    g0 = jnp.min(jnp.where(gs == gmax, lane_f, far), axis=-1, keepdims=True)
    in_best = jnp.where(lane_f >= g0, jnp.where(lane_f < g0 + EXPERTS_PER_GROUP, 1.0, 0.0), 0.0)
    sel1 = jnp.where(lane_f == i1, in_best, 0.0)
    sel2 = jnp.where(lane_f == i2, in_best, 0.0)
    both = sel1 + sel2
    wsum = jnp.sum(both * scores, axis=-1, keepdims=True)
    wmat = both * scores / wsum
    return wmat + pltpu.roll(sel1 + 2.0 * sel2, N_EXPERTS, 1)


def _pack_rows(v):
    w = v.shape[1] // 2
    bits = lax.bitcast_convert_type(v.astype(BF16).astype(F32), jnp.uint32)
    return bits[:, :w] | (bits[:, w:] >> 16)


def _unpack_rows(p):
    hi = lax.bitcast_convert_type(p & jnp.uint32(0xFFFF0000), F32)
    lo = lax.bitcast_convert_type(p << 16, F32)
    return hi, lo


def _merge_body(x_ref, mod_ref, g1_ref, g2_ref, ya_ref, yb_ref, yc_ref, yd_ref,
                wm_ref, bm_ref, wb_ref, wo_ref, rwh_ref, rwl_ref, xo_ref, h2_ref, sc_ref):
    d = x_ref.shape[1]
    x = x_ref[...]
    h = (_rms(x, g1_ref[...]) * (1.0 + mod_ref[1:2, :]) + mod_ref[0:1, :]).astype(BF16)
    acc = jnp.zeros(x.shape, F32)
    for n, y_ref in enumerate((ya_ref, yb_ref, yc_ref, yd_ref)):
        gate = jax.nn.sigmoid(jnp.dot(h, wm_ref[:, n * d:(n + 1) * d], preferred_element_type=F32)
                              + bm_ref[:, n * d:(n + 1) * d])
        acc = acc + gate * jnp.dot(y_ref[...], wb_ref[n], preferred_element_type=F32)
    y = jnp.dot(acc.astype(BF16), wo_ref[...], preferred_element_type=F32)
    xn = x + mod_ref[2:3, :] * y
    xo_ref[...] = xn
    h2 = _rms(xn, g2_ref[...]) * (1.0 + mod_ref[4:5, :]) + mod_ref[3:4, :]
    h2_ref[...] = _pack_rows(h2)
    h2_hi = h2.astype(BF16)
    h2_lo = (h2 - h2_hi.astype(F32)).astype(BF16)
    logits = (jnp.dot(h2_hi, rwh_ref[...], preferred_element_type=F32)
              + jnp.dot(h2_lo, rwh_ref[...], preferred_element_type=F32)
              + jnp.dot(h2_hi, rwl_ref[...], preferred_element_type=F32))
    sc_ref[...] = jax.nn.sigmoid(logits)


def _merge(x_all, mods, g1, g2, ys, wm, bm, wb, wo, rwh, rwl, n_ctx, with_ctx):
    b, s, d = x_all.shape
    tile0 = 0 if with_ctx else n_ctx // TM
    nt = s // TM - tile0
    tile = lambda w: pl.BlockSpec((None, TM, w), lambda bb, i: (bb, i + tile0, 0))
    sub = lambda w: pl.BlockSpec((None, TM, w), lambda bb, i: (bb, i, 0))
    so = nt * TM
    return pl.pallas_call(
        _merge_body,
        grid=(b, nt),
        in_specs=[tile(d),
                  pl.BlockSpec((None, None, 6, d), lambda bb, i: (bb, jnp.minimum(i + tile0, 1), 0, 0)),
                  _const_spec((1, d)), _const_spec((1, d)),
                  sub(BRANCH_W), tile(BRANCH_W), tile(BRANCH_W), sub(BRANCH_W),
                  _const_spec(wm.shape), _const_spec(bm.shape), _const_spec(wb.shape),
                  _const_spec(wo.shape), _const_spec(rwh.shape), _const_spec(rwl.shape)],
        out_specs=[sub(d), sub(d // 2), sub(LANE)],
        out_shape=[jax.ShapeDtypeStruct((b, so, d), F32),
                   jax.ShapeDtypeStruct((b, so, d // 2), jnp.uint32),
                   jax.ShapeDtypeStruct((b, so, LANE), F32)],
        compiler_params=_cparams(("arbitrary", "arbitrary")),
        name="merge",
    )(x_all, mods, g1, g2, *ys, wm, bm, wb, wo, rwh, rwl)


def _expert_body(be_ref, nu_ref, x_ref, w1_ref, w3_ref, w2_ref, o_ref):
    j = pl.program_id(0)

    @pl.when(j < nu_ref[0])
    def _():
        half = x_ref.shape[1]
        x_hi, x_lo = _unpack_rows(x_ref[...])
        x_hi = x_hi.astype(BF16)
        x_lo = x_lo.astype(BF16)

        def up(w_ref):
            return (jnp.dot(x_hi, w_ref[0:half, :].astype(BF16), preferred_element_type=F32)
                    + jnp.dot(x_lo, w_ref[half:, :].astype(BF16), preferred_element_type=F32))

        a = up(w1_ref)
        g = up(w3_ref)
        hmid = (a * jax.nn.sigmoid(a) * g).astype(BF16)
        o_ref[...] = _pack_rows(jnp.dot(hmid, w2_ref[...].astype(BF16), preferred_element_type=F32))

    @pl.when(j >= nu_ref[0])
    def _():
        o_ref[...] = jnp.zeros(o_ref.shape, o_ref.dtype)


def _expert_ffn(block_expert, n_used, xb, w1, w3, w2, layer):
    n_rows, dh = xb.shape
    nb = n_rows // EXP_ROWS
    d, ff = w1.shape[-2:]
    wspec = lambda r, c: pl.BlockSpec((None, None, r, c), lambda j, be, nu: (layer, be[j], 0, 0))
    grid_spec = pltpu.PrefetchScalarGridSpec(
        num_scalar_prefetch=2,
        grid=(nb,),
        in_specs=[pl.BlockSpec((EXP_ROWS, dh), lambda j, be, nu: (jnp.minimum(j, nu[0] - 1), 0)),
                  wspec(d, ff), wspec(d, ff), wspec(ff, d)],
        out_specs=pl.BlockSpec((EXP_ROWS, dh), lambda j, be, nu: (j, 0)),
    )
    return pl.pallas_call(
        _expert_body,
        grid_spec=grid_spec,
        out_shape=jax.ShapeDtypeStruct((n_rows, dh), jnp.uint32),
        compiler_params=_cparams(("arbitrary",)),
        name="expert_ffn",
    )(block_expert, n_used, xb, w1, w3, w2)


def _markers(route):
    lane = lax.broadcasted_iota(jnp.int32, route.shape, 1)
    return jnp.where(lane < N_EXPERTS, pltpu.roll(route, N_EXPERTS, 1), 0.0)


ROUTE_ROWS = 1024


def _route_body(pr_ref, rb_ref, rt_ref, cnt_ref):
    @pl.when(pl.program_id(0) == 0)
    def _():
        cnt_ref[...] = jnp.zeros(cnt_ref.shape, F32)

    rt = _route_tile(pr_ref[...], rb_ref[...])
    rt_ref[...] = rt
    cnt_ref[...] += jnp.sum(jnp.where(_markers(rt) > 0.0, 1.0, 0.0), axis=0, keepdims=True)


def _route(probs, rb):
    t = probs.shape[0]
    tile = pl.BlockSpec((ROUTE_ROWS, LANE), lambda i: (i, 0))
    return pl.pallas_call(
        _route_body,
        grid=(t // ROUTE_ROWS,),
        in_specs=[tile, _const_spec(rb.shape)],
        out_specs=[tile, pl.BlockSpec((SUBLANE, LANE), lambda i: (0, 0))],
        out_shape=[jax.ShapeDtypeStruct((t, LANE), F32), jax.ShapeDtypeStruct((SUBLANE, LANE), F32)],
        compiler_params=_cparams(("arbitrary",)),
        name="route",
    )(probs, rb)


def _rank_body(rt_ref, cnt_ref, slot_ref, run_scr, start_scr, tri_scr):
    rows = rt_ref.shape[0]

    @pl.when(pl.program_id(0) == 0)
    def _():
        cnt = cnt_ref[...]
        padded = jnp.floor((cnt + (EXP_ROWS - 1.0)) * (1.0 / EXP_ROWS)) * EXP_ROWS
        lane = lax.broadcasted_iota(jnp.int32, cnt.shape, 1)
        end = padded
        sh = 1
        while sh < LANE:
            end = end + jnp.where(lane >= sh, pltpu.roll(end, sh, 1), 0.0)
            sh *= 2
        start_scr[...] = end - padded
        run_scr[...] = jnp.zeros(run_scr.shape, F32)
        ri = lax.broadcasted_iota(jnp.int32, (rows, rows), 0)
        ci = lax.broadcasted_iota(jnp.int32, (rows, rows), 1)
        tri_scr[...] = jnp.where(ri > ci, 1.0, 0.0).astype(BF16)

    mk = _markers(rt_ref[...])
    p = jnp.where(mk > 0.0, 1.0, 0.0)
    before = jnp.dot(tri_scr[...], p.astype(BF16), preferred_element_type=F32)
    pos = before + run_scr[0:1, :] + start_scr[0:1, :]
    s1 = jnp.sum(jnp.where(mk == 1.0, pos, 0.0), axis=-1, keepdims=True)
    s2 = jnp.sum(jnp.where(mk == 2.0, pos, 0.0), axis=-1, keepdims=True)
    which = lax.broadcasted_iota(jnp.int32, (rows, TOP_K), 1)
    slot_ref[...] = jnp.where(which == 0, s1, s2).astype(jnp.int32)
    run_scr[...] += jnp.sum(p, axis=0, keepdims=True)


def _rank(route, counts):
    t = route.shape[0]
    return pl.pallas_call(
        _rank_body,
        grid=(t // ROUTE_ROWS,),
        in_specs=[pl.BlockSpec((ROUTE_ROWS, LANE), lambda i: (i, 0)), _const_spec((SUBLANE, LANE))],
        out_specs=pl.BlockSpec((ROUTE_ROWS, TOP_K), lambda i: (i, 0)),
        out_shape=jax.ShapeDtypeStruct((t, TOP_K), jnp.int32),
        scratch_shapes=[pltpu.VMEM((SUBLANE, LANE), F32), pltpu.VMEM((SUBLANE, LANE), F32),
                        pltpu.VMEM((ROUTE_ROWS, ROUTE_ROWS), BF16)],
        compiler_params=_cparams(("arbitrary",)),
        name="rank",
    )(route, counts)


def _block_experts(counts, n_blocks):
    cnt = counts[0, :N_EXPERTS].astype(jnp.int32)
    padded = (cnt + EXP_ROWS - 1) // EXP_ROWS * EXP_ROWS
    pad_end = jnp.cumsum(padded)
    block_expert = jnp.minimum(
        jnp.searchsorted(pad_end, jnp.arange(n_blocks, dtype=jnp.int32) * EXP_ROWS, side='right'),
        N_EXPERTS - 1).astype(jnp.int32)
    n_used = (pad_end[-1] // EXP_ROWS).astype(jnp.int32).reshape(1)
    return block_expert, n_used, cnt, pad_end.astype(jnp.int32)


def _row_copy(src, src_row, dst, dst_row, sem):
    return pltpu.make_async_copy(src.at[pl.ds(src_row, 1)], dst.at[pl.ds(dst_row, 1)], sem)


def _dispatch_body(slot_ref, cnt_ref, end_ref, nu_ref, h_ref, xb_ref, zero_scr, sem, zsem):
    tm = h_ref.shape[0]

    @pl.when(pl.program_id(0) == 0)
    def _():
        zero_scr[...] = jnp.zeros(zero_scr.shape, zero_scr.dtype)

        def fill(e, carry):
            @pl.when(cnt_ref[e] > 0)
            def _():
                r0 = pl.multiple_of(end_ref[e] - EXP_ROWS, EXP_ROWS)
                pltpu.make_async_copy(zero_scr, xb_ref.at[pl.ds(r0, EXP_ROWS)], zsem).start()
            return carry

        def drain(e, carry):
            @pl.when(cnt_ref[e] > 0)
            def _():
                pltpu.make_async_copy(zero_scr, xb_ref.at[pl.ds(0, EXP_ROWS)], zsem).wait()
            return carry

        def fill_tail(j, carry):
            r0 = pl.multiple_of(j * EXP_ROWS, EXP_ROWS)
            pltpu.make_async_copy(zero_scr, xb_ref.at[pl.ds(r0, EXP_ROWS)], zsem).start()
            return carry

        def drain_tail(j, carry):
            pltpu.make_async_copy(zero_scr, xb_ref.at[pl.ds(0, EXP_ROWS)], zsem).wait()
            return carry

        n_blocks = xb_ref.shape[0] // EXP_ROWS
        lax.fori_loop(0, N_EXPERTS, fill, 0)
        lax.fori_loop(nu_ref[0], n_blocks, fill_tail, 0)
        lax.fori_loop(0, N_EXPERTS, drain, 0)
        lax.fori_loop(nu_ref[0], n_blocks, drain_tail, 0)

    def issue(t, carry):
        for k in range(TOP_K):
            _row_copy(h_ref, t, xb_ref, slot_ref[0, TOP_K * t + k], sem).start(priority=k % 2)
        return carry

    lax.fori_loop(0, tm, issue, 0, unroll=8)
    for k in range(TOP_K):
        pltpu.make_async_copy(h_ref, xb_ref.at[pl.ds(0, tm)], sem).wait()


def _dispatch(slot_tiles, cnt, pad_end, n_used, h2, n_rows):
    b, s, dh = h2.shape
    nt = b * s // TM
    smem = pl.BlockSpec(memory_space=pltpu.SMEM)
    return pl.pallas_call(
        _dispatch_body,
        grid=(nt,),
        in_specs=[pl.BlockSpec((None, 1, TOP_K * TM), lambda i: (i, 0, 0), memory_space=pltpu.SMEM),
                  smem, smem, smem,
                  pl.BlockSpec((TM, dh), lambda i: (i, 0))],
        out_specs=pl.BlockSpec(memory_space=pl.ANY),
        out_shape=jax.ShapeDtypeStruct((n_rows, dh), h2.dtype),
        scratch_shapes=[pltpu.VMEM((EXP_ROWS, dh), h2.dtype),
                        pltpu.SemaphoreType.DMA(()), pltpu.SemaphoreType.DMA(())],
        compiler_params=_cparams(("arbitrary",)),
        name="dispatch",
    )(slot_tiles, cnt, pad_end, n_used, h2.reshape(b * s, dh))


def _resid_body(slot_ref, x_ref, mod_ref, rt_ref, gf_ref, y_hbm, o_ref, ybuf0, ybuf1, sem, *, final):
    tm = x_ref.shape[0]
    ybuf = (ybuf0, ybuf1)

    def issue(t, carry):
        for k in range(TOP_K):
            _row_copy(y_hbm, slot_ref[0, TOP_K * t + k], ybuf[k], t, sem).start(priority=k % 2)
        return carry

    lax.fori_loop(0, tm, issue, 0, unroll=8)
    rt = rt_ref[...]
    mk = _markers(rt)
    w1 = jnp.sum(jnp.where(mk == 1.0, rt, 0.0), axis=-1, keepdims=True)
    w2 = jnp.sum(jnp.where(mk == 2.0, rt, 0.0), axis=-1, keepdims=True)
    for k in range(TOP_K):
        pltpu.make_async_copy(y_hbm.at[pl.ds(0, tm)], ybuf[k], sem).wait()
    a_hi, a_lo = _unpack_rows(ybuf0[...])
    b_hi, b_lo = _unpack_rows(ybuf1[...])
    f = jnp.concatenate([w1 * a_hi + w2 * b_hi, w1 * a_lo + w2 * b_lo], axis=1)
    xn = x_ref[...] + mod_ref[5:6, :] * f
    if final:
        xn = _rms(xn, gf_ref[...])
    o_ref[...] = xn


def _moe_residual(slot_tiles, x_all, mods, route, gf, yblk, has_ctx, final):
    b, s, d = x_all.shape
    nt = s // TM
    tin = lambda ww: pl.BlockSpec((None, TM, ww), lambda bb, i: (bb, i, 0))
    mod_row = (lambda i: jnp.minimum(i, 1)) if has_ctx else (lambda i: 1)
    return pl.pallas_call(
        functools.partial(_resid_body, final=final),
        grid=(b, nt),
        in_specs=[pl.BlockSpec((None, 1, TOP_K * TM), lambda bb, i: (bb * nt + i, 0, 0),
                               memory_space=pltpu.SMEM),
                  tin(d),
                  pl.BlockSpec((None, None, 6, d), lambda bb, i: (bb, mod_row(i), 0, 0)),
                  tin(LANE), _const_spec((1, d)),
                  pl.BlockSpec(memory_space=pl.ANY)],
        out_specs=tin(d),
        out_shape=jax.ShapeDtypeStruct((b, s, d), F32),
        scratch_shapes=[pltpu.VMEM((TM, d // 2), jnp.uint32), pltpu.VMEM((TM, d // 2), jnp.uint32),
                        pltpu.SemaphoreType.DMA(())],
        compiler_params=_cparams(("arbitrary", "arbitrary")),
        name="moe_residual",
    )(slot_tiles, x_all, mods, route, gf, yblk)


def kernel(x, c, ctx, c_ctx, ada_w, ada_b, norm1_g, norm2_g, w_in, mla_q_norm, mla_kv_norm, mla_w_uq,
           mla_w_ukv, ret_decay, ret_norm, lru_conv_w, lru_conv_b, lru_w_a, lru_b_a, lru_w_x, lru_b_x,
           lru_lambda, gqa_q_norm, gqa_k_norm, w_branch, w_merge, b_merge, w_out, router_w, router_bias,
           moe_w1, moe_w3, moe_w2, final_norm):
    b, seq, d = x.shape
    n_ctx = ctx.shape[1]
    depth = ada_w.shape[0]
    s = n_ctx + seq
    assert n_ctx == TM and seq % TM == 0 and seq % GRID_W == 0

    r_pad = -(-(b + 1) // SUBLANE) * SUBLANE
    cc = jnp.zeros((r_pad, d), F32).at[:b].set(c).at[b].set(c_ctx)
    mods_all = _ada_mods(cc, ada_w, ada_b)

    mla_cos, mla_sin = _rope_slot_tables(n_ctx, seq, MLA_ROPE, MLA_NOPE)
    hd_cos, hd_sin = _rope_slot_tables(n_ctx, seq, GQA_HEAD_DIM, 0)

    in_cols = _in_proj_columns()
    uq_cols, ukv_cols = _mla_up_columns()
    rw = jnp.concatenate([router_w.astype(F32), jnp.zeros((d, LANE - N_EXPERTS), F32)], axis=1)
    rw_hi = rw.astype(BF16)
    rw_lo = (rw - rw_hi.astype(F32)).astype(BF16)
    rb = jnp.concatenate([router_bias.astype(F32), jnp.zeros((LANE - N_EXPERTS,), F32)])[None]

    x_all = jnp.concatenate([ctx, x], axis=1)
    out = None
    for l in range(depth):
        last = l == depth - 1
        m = mods_all[l].reshape(r_pad, 6, d)
        mods = jnp.stack([jnp.broadcast_to(m[b], (b, 6, d)), m[:b]], axis=1)

        mla_scale = (MLA_NOPE + MLA_ROPE) ** -0.5
        gqa_scale = GQA_HEAD_DIM ** -0.5

        def swap_gain(g):
            g4 = g.reshape(2, 2, GQA_HEAD_DIM // 4)
            return g4[:, ::-1, :].reshape(GQA_HEAD_DIM)

        def gain_slot(g):
            return jnp.concatenate([g, jnp.zeros((LANE - GQA_HEAD_DIM,), F32)])

        gqc, gqs = gain_slot(gqa_q_norm[l]), gain_slot(swap_gain(gqa_q_norm[l]))
        gkc, gks = gain_slot(gqa_k_norm[l]), gain_slot(swap_gain(gqa_k_norm[l]))
        tabs = jnp.stack([
            mla_cos * mla_scale, mla_sin * mla_scale, mla_cos, mla_sin,
            hd_cos, hd_sin, hd_cos * RET_DK ** -0.5, hd_sin * RET_DK ** -0.5,
            hd_cos * gqc * gqa_scale, hd_sin * gqs * gqa_scale, hd_cos * gkc, hd_sin * gks])

        win_p = _take_cols(w_in[l], in_cols).astype(BF16)
        wuq_p = _take_cols(mla_w_uq[l], uq_cols).astype(BF16)
        wukv_p = _take_cols(mla_w_ukv[l], ukv_cols).astype(BF16)

        (mq, mk, mv, rq, rk, rv, rg, lu, lg, gq, gk, gv) = _inproj(
            x_all, mods, norm1_g[l][None], tabs, win_p, wuq_p, wukv_p,
            mla_q_norm[l][None], mla_kv_norm[l][None])

        ya = _attention(mq, mk, mv, MLA_HEADS, MLA_HEADS, n_ctx, not last)
        yd = _attention(gq, gk, gv, GQA_HEADS, GQA_KV_HEADS, n_ctx, not last)

        log_g = -jax.nn.softplus(-ret_decay[l].astype(F32))
        yb = _retention(log_g, rq, rk, rv, rg, ret_norm[l][None], n_ctx)

        eye = jnp.eye(LRU_BLOCKS, dtype=F32)

        def block_diag(wblk):
            return jnp.einsum('ncd,nm->ncmd', wblk, eye).reshape(LRU_WIDTH, LRU_WIDTH)

        wcat = jnp.concatenate([block_diag(lru_w_a[l, 0]), block_diag(lru_w_x[l, 0]),
                                block_diag(lru_w_a[l, 1]), block_diag(lru_w_x[l, 1])], axis=1).astype(BF16)
        bcat = jnp.concatenate([lru_b_a[l, 0], lru_b_x[l, 0], lru_b_a[l, 1], lru_b_x[l, 1]])[None]
        yc = _rglru(lu, lg, wcat, bcat, lru_conv_w[l], lru_conv_b[l][None], lru_lambda[l], n_ctx)

        x_all, h2, probs = _merge(
            x_all, mods, norm1_g[l][None], norm2_g[l][None], (ya, yb, yc, yd),
            w_merge[l].astype(BF16), b_merge[l][None], w_branch[l].astype(BF16), w_out[l].astype(BF16),
            rw_hi, rw_lo, n_ctx, not last)

        t = b * (seq if last else s)
        assert t % ROUTE_ROWS == 0
        n_blocks = -(-t * TOP_K // EXP_ROWS) + N_EXPERTS
        route, counts = _route(probs.reshape(t, LANE), rb)
        slot_tiles = _rank(route, counts).reshape(t // TM, 1, TOP_K * TM)
        route = route.reshape(b, t // b, LANE)
        block_expert, n_used, cnt, pad_end = _block_experts(counts, n_blocks)
        xb = _dispatch(slot_tiles, cnt, pad_end, n_used, h2, n_blocks * EXP_ROWS)
        yblk = _expert_ffn(block_expert, n_used, xb, moe_w1, moe_w3, moe_w2, l)
        out = _moe_residual(slot_tiles, x_all, mods, route, final_norm[None], yblk, not last, last)
        x_all = out
    return out
```

```python
import functools

import numpy as np
import jax
import jax.numpy as jnp
from jax import lax
from jax.experimental import pallas as pl
from jax.experimental.pallas import tpu as pltpu

F32 = jnp.float32
BF16 = jnp.bfloat16

LANE = 128
SUBLANE = 8
VMEM_LIMIT = 56 * 1024 * 1024

GRID_W = 64
ROPE_BASE = 10000.0
EPS = 1e-6
MLA_HEADS, MLA_NOPE, MLA_ROPE, MLA_V = 8, 64, 32, 64
MLA_Q_LORA, MLA_KV_LORA = 256, 128
RET_HEADS, RET_DK, RET_DV, RET_CHUNK = 4, 64, 128, 128
LRU_WIDTH, LRU_BLOCKS, LRU_C, CONV_W, CONV_PAD_LEFT = 512, 8, 8.0, 4, 2
LRU_BLOCK_W = LRU_WIDTH // LRU_BLOCKS
GQA_HEADS, GQA_KV_HEADS, GQA_HEAD_DIM = 8, 2, 64
N_BRANCH, BRANCH_W = 4, 512
N_EXPERTS, N_GROUPS, TOP_K, EXPERT_FF, MOE_BLOCK = 64, 8, 2, 256, 128
EXPERTS_PER_GROUP = N_EXPERTS // N_GROUPS

IN_SPLITS = (MLA_Q_LORA, MLA_KV_LORA, MLA_ROPE,
             RET_HEADS * RET_DK, RET_HEADS * RET_DK, RET_HEADS * RET_DV, RET_HEADS * RET_DV,
             LRU_WIDTH, LRU_WIDTH,
             GQA_HEADS * GQA_HEAD_DIM, GQA_KV_HEADS * GQA_HEAD_DIM, GQA_KV_HEADS * GQA_HEAD_DIM)
IN_OFF = tuple(int(o) for o in np.cumsum((0,) + IN_SPLITS))
D_IN = IN_OFF[-1]

TM = 256
EXP_ROWS = 256
HEAD_SLOT = LANE
ONE_LANE = 64

ZP_CQ, ZP_CKV, ZP_KR = 0, 256, 384
ZP_RQ, ZP_RK, ZP_RV, ZP_RG = 512, 1024, 1536, 2048
ZP_LU, ZP_LG = 2560, 3072
ZP_GQ, ZP_GK, ZP_GV = 3584, 4608, 4864
ZP_W = 5120


def _cparams(sem):
    return pltpu.CompilerParams(dimension_semantics=sem, vmem_limit_bytes=VMEM_LIMIT)


def _const_spec(shape):
    nd = len(shape)
    return pl.BlockSpec(shape, lambda *_: (0,) * nd, pipeline_mode=pl.Buffered(1))


def _in_proj_columns():
    idx = np.full((ZP_W,), D_IN, np.int64)
    o = IN_OFF
    idx[ZP_CQ:ZP_CQ + 256] = o[0] + np.arange(256)
    idx[ZP_CKV:ZP_CKV + 128] = o[1] + np.arange(128)
    idx[ZP_KR + MLA_NOPE:ZP_KR + MLA_NOPE + MLA_ROPE] = o[2] + np.arange(MLA_ROPE)
    for h in range(RET_HEADS):
        idx[ZP_RQ + h * 128:ZP_RQ + h * 128 + 64] = o[3] + h * 64 + np.arange(64)
        idx[ZP_RK + h * 128:ZP_RK + h * 128 + 64] = o[4] + h * 64 + np.arange(64)
    idx[ZP_RV:ZP_RV + 512] = o[5] + np.arange(512)
    idx[ZP_RG:ZP_RG + 512] = o[6] + np.arange(512)
    idx[ZP_LU:ZP_LU + 512] = o[7] + np.arange(512)
    idx[ZP_LG:ZP_LG + 512] = o[8] + np.arange(512)
    for h in range(GQA_HEADS):
        idx[ZP_GQ + h * 128:ZP_GQ + h * 128 + 64] = o[9] + h * 64 + np.arange(64)
    for h in range(GQA_KV_HEADS):
        idx[ZP_GK + h * 128:ZP_GK + h * 128 + 64] = o[10] + h * 64 + np.arange(64)
        idx[ZP_GV + h * 128:ZP_GV + h * 128 + 64] = o[11] + h * 64 + np.arange(64)
    return idx


def _mla_up_columns():
    dq = MLA_NOPE + MLA_ROPE
    dkv = MLA_NOPE + MLA_V
    qi = np.full((MLA_HEADS * 128,), MLA_HEADS * dq, np.int64)
    ki = np.full((MLA_HEADS * 128,), MLA_HEADS * dkv, np.int64)
    vi = np.full((MLA_HEADS * 128,), MLA_HEADS * dkv, np.int64)
    for h in range(MLA_HEADS):
        qi[h * 128:h * 128 + dq] = h * dq + np.arange(dq)
        ki[h * 128:h * 128 + MLA_NOPE] = h * dkv + np.arange(MLA_NOPE)
        vi[h * 128:h * 128 + MLA_V] = h * dkv + MLA_NOPE + np.arange(MLA_V)
    return qi, np.concatenate([ki, vi])


def _take_cols(w, idx):
    wz = jnp.concatenate([w, jnp.zeros((w.shape[0], 1), w.dtype)], axis=1)
    return jnp.take(wz, jnp.asarray(idx, jnp.int32), axis=1)


def _rope_slot_tables(n_ctx, seq, rot_dim, lane0):
    half = rot_dim // 2
    q = half // 2
    pos = jnp.arange(seq, dtype=jnp.int32)
    rows = (pos // GRID_W).astype(F32)
    cols = (pos % GRID_W).astype(F32)
    inv = ROPE_BASE ** (-jnp.arange(0, half, 2, dtype=F32) / half)
    ar = rows[:, None] * inv
    ac = cols[:, None] * inv
    cos = jnp.concatenate([jnp.cos(ar), jnp.cos(ar), jnp.cos(ac), jnp.cos(ac)], axis=1)
    sin = jnp.concatenate([-jnp.sin(ar), jnp.sin(ar), -jnp.sin(ac), jnp.sin(ac)], axis=1)
    assert cos.shape[1] == rot_dim and q * 4 == rot_dim
    cos_t = jnp.ones((n_ctx + seq, LANE), F32).at[n_ctx:, lane0:lane0 + rot_dim].set(cos)
    sin_t = jnp.zeros((n_ctx + seq, LANE), F32).at[n_ctx:, lane0:lane0 + rot_dim].set(sin)
    return cos_t, sin_t


def _swap_lanes(x, blk):
    n = x.shape[-1]
    lane = lax.broadcasted_iota(jnp.int32, x.shape, x.ndim - 1)
    up = pltpu.roll(x, n - blk, x.ndim - 1)
    dn = pltpu.roll(x, blk, x.ndim - 1)
    return jnp.where((lane % (2 * blk)) < blk, up, dn)


def _rms(x, g):
    return x * lax.rsqrt(jnp.mean(x * x, axis=-1, keepdims=True) + EPS) * g


def _ada_body(c_ref, w_ref, b_ref, o_ref):
    c = c_ref[...]
    s = (c * jax.nn.sigmoid(c)).astype(BF16)
    o_ref[...] = jnp.dot(s, w_ref[...].astype(BF16), preferred_element_type=F32) + b_ref[...]


def _ada_mods(cc, ada_w, ada_b):
    depth, d, n = ada_w.shape
    r = cc.shape[0]
    tn = 1536
    return pl.pallas_call(
        _ada_body,
        grid=(depth, n // tn),
        in_specs=[pl.BlockSpec((r, d), lambda l, j: (0, 0)),
                  pl.BlockSpec((None, d, tn), lambda l, j: (l, 0, j)),
                  pl.BlockSpec((None, 1, tn), lambda l, j: (l, 0, j))],
        out_specs=pl.BlockSpec((None, r, tn), lambda l, j: (l, 0, j)),
        out_shape=jax.ShapeDtypeStruct((depth, r, n), F32),
        compiler_params=_cparams(("arbitrary", "arbitrary")),
        name="ada_mods",
    )(cc, ada_w, ada_b.reshape(depth, 1, n))


def _inproj_body(x_ref, mod_ref, g1_ref, tab_ref, win_ref, wuq_ref, wukv_ref, gq_ref, gkv_ref,
                 mq_ref, mk_ref, mv_ref, rq_ref, rk_ref, rv_ref, rg_ref, lu_ref, lg_ref,
                 gq_out, gk_out, gv_out):
    x = x_ref[...]
    shift = mod_ref[0:1, :]
    scale = mod_ref[1:2, :]
    h = _rms(x, g1_ref[...]) * (1.0 + scale) + shift
    hb = h.astype(BF16)

    def proj(c0, c1):
        return jnp.dot(hb, win_ref[:, c0:c1], preferred_element_type=F32)

    lane = lax.broadcasted_iota(jnp.int32, (x.shape[0], LANE), 1)
    one_col = jnp.where(lane == ONE_LANE, 1.0, 0.0).astype(F32)

    def rope(v, ci, blk):
        return v * tab_ref[ci] + _swap_lanes(v, blk) * tab_ref[ci + 1]

    cq = proj(ZP_CQ, ZP_CQ + 256)
    qn = _rms(cq, gq_ref[...]).astype(BF16)
    q = jnp.dot(qn, wuq_ref[...], preferred_element_type=F32)
    for hh in range(MLA_HEADS):
        sl = slice(hh * 128, (hh + 1) * 128)
        mq_ref[:, sl] = rope(q[:, sl], 0, MLA_ROPE // 4).astype(BF16)
    ckv = proj(ZP_CKV, ZP_CKV + 128)
    kvn = _rms(ckv, gkv_ref[...]).astype(BF16)
    kv = jnp.dot(kvn, wukv_ref[...], preferred_element_type=F32)
    kr = rope(proj(ZP_KR, ZP_KR + 128), 2, MLA_ROPE // 4)
    for hh in range(MLA_HEADS):
        sl = slice(hh * 128, (hh + 1) * 128)
        mk_ref[sl, :] = (kv[:, sl] + kr).T.astype(BF16)
        mv_ref[:, sl] = (kv[:, MLA_HEADS * 128 + hh * 128:MLA_HEADS * 128 + (hh + 1) * 128] + one_col).astype(BF16)

    rq = proj(ZP_RQ, ZP_RQ + 512)
    rk = proj(ZP_RK, ZP_RK + 512)
    for hh in range(RET_HEADS):
        sl = slice(hh * 128, (hh + 1) * 128)
        rq_ref[:, sl] = rope(rq[:, sl], 4, RET_DK // 4).astype(BF16)
        rk_ref[:, sl] = rope(rk[:, sl], 6, RET_DK // 4).astype(BF16)
    rv_ref[...] = proj(ZP_RV, ZP_RV + 512).astype(BF16)
    rg_ref[...] = proj(ZP_RG, ZP_RG + 512).astype(BF16)

    lu_ref[...] = proj(ZP_LU, ZP_LU + 512).astype(BF16)
    lg_ref[...] = proj(ZP_LG, ZP_LG + 512).astype(BF16)

    gq = proj(ZP_GQ, ZP_GQ + 1024)
    for hh in range(GQA_HEADS):
        sl = slice(hh * 128, (hh + 1) * 128)
        v = gq[:, sl]
        v = v * lax.rsqrt(jnp.sum(v * v, axis=-1, keepdims=True) * (1.0 / GQA_HEAD_DIM) + EPS)
        gq_out[:, sl] = rope(v, 8, GQA_HEAD_DIM // 4).astype(BF16)
    gk = proj(ZP_GK, ZP_GK + 256)
    gv = proj(ZP_GV, ZP_GV + 256)
    for hh in range(GQA_KV_HEADS):
        sl = slice(hh * 128, (hh + 1) * 128)
        v = gk[:, sl]
        v = v * lax.rsqrt(jnp.sum(v * v, axis=-1, keepdims=True) * (1.0 / GQA_HEAD_DIM) + EPS)
        gk_out[sl, :] = rope(v, 10, GQA_HEAD_DIM // 4).T.astype(BF16)
        gv_out[:, sl] = (gv[:, sl] + one_col).astype(BF16)


def _inproj(x_all, mods, g1, tabs, win_p, wuq_p, wukv_p, gq, gkv):
    b, s, d = x_all.shape
    nt = s // TM
    widths = (1024, 1024, 1024, 512, 512, 512, 512, 512, 512, 1024, 256, 256)
    tile = lambda w: pl.BlockSpec((None, TM, w), lambda i, bb: (bb, i, 0))
    ttile = lambda w: pl.BlockSpec((None, w, TM), lambda i, bb: (bb, 0, i))
    transposed = (1, 10)
    return pl.pallas_call(
        _inproj_body,
        grid=(nt, b),
        in_specs=[tile(d),
                  pl.BlockSpec((None, None, 6, d), lambda i, bb: (bb, jnp.minimum(i, 1), 0, 0)),
                  _const_spec((1, d)),
                  pl.BlockSpec((12, TM, LANE), lambda i, bb: (0, i, 0)),
                  _const_spec(win_p.shape), _const_spec(wuq_p.shape), _const_spec(wukv_p.shape),
                  _const_spec((1, MLA_Q_LORA)), _const_spec((1, MLA_KV_LORA))],
        out_specs=[ttile(w) if j in transposed else tile(w) for j, w in enumerate(widths)],
        out_shape=[jax.ShapeDtypeStruct((b, w, s) if j in transposed else (b, s, w), BF16)
                   for j, w in enumerate(widths)],
        compiler_params=_cparams(("arbitrary", "arbitrary")),
        name="in_proj",
    )(x_all, mods, g1, tabs, win_p, wuq_p, wukv_p, gq, gkv)


ATT_TILES = 2


def _attn_body(*refs, heads, kv_heads, n_q):
    q_refs, (kt_ref, v_ref, o_ref) = refs[:n_q], refs[n_q:]
    grp = heads // kv_heads
    for hp in range(heads // 2):
        outs = []
        for h in (2 * hp, 2 * hp + 1):
            g = h // grp
            sl = slice(h * 128, (h + 1) * 128)
            q = q_refs[0][:, sl] if n_q == 1 else jnp.concatenate([r[:, sl] for r in q_refs], axis=0)
            kt = kt_ref[g * 128:(g + 1) * 128, :]
            v = v_ref[:, g * 128:(g + 1) * 128]
            s = jnp.dot(q, kt, preferred_element_type=F32).astype(BF16)
            m = jnp.max(s, axis=-1, keepdims=True)
            p = jnp.exp(s - m)
            o = jnp.dot(p, v, preferred_element_type=F32)
            outs.append(o[:, :64] / o[:, ONE_LANE:ONE_LANE + 1])
        o_ref[:, hp * 128:(hp + 1) * 128] = jnp.concatenate(outs, axis=1).astype(o_ref.dtype)


def _attention_lat(q, kt, v, heads, kv_heads, n_ctx):
    b, s, _ = q.shape
    off = n_ctx // TM
    rows = ATT_TILES * TM
    nq = (s - n_ctx) // rows
    qspec = lambda r: pl.BlockSpec((None, TM, heads * 128), lambda bb, j: (bb, off + ATT_TILES * j + r, 0))
    return pl.pallas_call(
        functools.partial(_attn_body, heads=heads, kv_heads=kv_heads, n_q=ATT_TILES),
        grid=(b, nq),
        in_specs=[qspec(r) for r in range(ATT_TILES)] + [
            pl.BlockSpec((None, kv_heads * 128, s), lambda bb, j: (bb, 0, 0)),
            pl.BlockSpec((None, s, kv_heads * 128), lambda bb, j: (bb, 0, 0))],
        out_specs=pl.BlockSpec((None, rows, heads * 64), lambda bb, j: (bb, j, 0)),
        out_shape=jax.ShapeDtypeStruct((b, nq * rows, heads * 64), BF16),
        compiler_params=_cparams(("arbitrary", "arbitrary")),
        name="attention_lat_h%d_kv%d" % (heads, kv_heads),
    )(*([q] * ATT_TILES), kt, v)


def _attention_ctx(q, kt, v, heads, kv_heads, n_ctx):
    b = q.shape[0]
    return pl.pallas_call(
        functools.partial(_attn_body, heads=heads, kv_heads=kv_heads, n_q=1),
        grid=(b,),
        in_specs=[pl.BlockSpec((None, n_ctx, heads * 128), lambda bb: (bb, 0, 0)),
                  pl.BlockSpec((None, kv_heads * 128, n_ctx), lambda bb: (bb, 0, 0)),
                  pl.BlockSpec((None, n_ctx, kv_heads * 128), lambda bb: (bb, 0, 0))],
        out_specs=pl.BlockSpec((None, n_ctx, heads * 64), lambda bb: (bb, 0, 0)),
        out_shape=jax.ShapeDtypeStruct((b, n_ctx, heads * 64), BF16),
        compiler_params=_cparams(("arbitrary",)),
        name="attention_ctx_h%d_kv%d" % (heads, kv_heads),
    )(q, kt, v)


def _attention(q, kt, v, heads, kv_heads, n_ctx, with_ctx):
    y = _attention_lat(q, kt, v, heads, kv_heads, n_ctx)
    if with_ctx:
        y = jnp.concatenate([_attention_ctx(q, kt, v, heads, kv_heads, n_ctx), y], axis=1)
    return y


RET_ROWS = 256


def _retention_body(lg_ref, q_ref, k_ref, v_ref, g_ref, gn_ref, o_ref,
                    kv_scr, st_scr, dec_scr, m_scr, *, n_ctx):
    c = RET_ROWS
    s = q_ref.shape[0]
    nc = s // c
    nctx = n_ctx // c
    back_order = list(range(nctx - 1, -1, -1)) + list(range(nc - 1, nctx - 1, -1))
    pos = lax.broadcasted_iota(jnp.int32, (c, LANE), 0).astype(F32)
    ri = lax.broadcasted_iota(jnp.int32, (c, c), 0)
    ci = lax.broadcasted_iota(jnp.int32, (c, c), 1)
    diff = (ri - ci).astype(F32)
    heads = [(h, slice(h * 128, (h + 1) * 128)) for h in range(RET_HEADS)]

    for h, _ in heads:
        lgf = lg_ref[0, h]
        lgb = lg_ref[1, h]
        dec_scr[h, 0] = jnp.exp(lgf * (c - 1.0 - pos))
        dec_scr[h, 1] = jnp.exp(lgb * pos)
        dec_scr[h, 2] = jnp.exp(lgf * (pos + 1.0))
        dec_scr[h, 3] = jnp.exp(lgb * (c - pos))
        m_scr[h] = jnp.where(diff >= 0, jnp.exp(lgf * jnp.maximum(diff, 0.0)),
                             jnp.exp(lgb * jnp.maximum(-diff, 0.0)))

    def kv_step(j, carry):
        r0 = pl.multiple_of(j * c, c)
        for h, sl in heads:
            kc = k_ref[pl.ds(r0, c), sl].astype(F32)
            vc = v_ref[pl.ds(r0, c), sl]
            kk = jnp.concatenate([(kc * dec_scr[h, 0]).astype(BF16), (kc * dec_scr[h, 1]).astype(BF16)], axis=1)
            kv_scr[h, j] = lax.dot_general(kk, vc, (((0,), (0,)), ((), ())), preferred_element_type=F32)
        return carry

    lax.fori_loop(0, nc, kv_step, 0, unroll=3)

    for h, _ in heads:
        gcf = jnp.exp(lg_ref[0, h] * c)
        gcb = jnp.exp(lg_ref[1, h] * c)
        sf = jnp.zeros((128, 128), F32)
        for j in range(nc):
            st_scr[h, j, 0:128, :] = sf.astype(BF16)
            sf = sf * gcf + kv_scr[h, j, 0:128, :]
        sb = jnp.zeros((128, 128), F32)
        for j in back_order:
            st_scr[h, j, 128:256, :] = sb.astype(BF16)
            sb = sb * gcb + kv_scr[h, j, 128:256, :]

    def out_step(j, carry):
        r0 = pl.multiple_of(j * c, c)
        for h, sl in heads:
            qb = q_ref[pl.ds(r0, c), sl]
            kb = k_ref[pl.ds(r0, c), sl]
            vc = v_ref[pl.ds(r0, c), sl]
            sc = lax.dot_general(qb, kb, (((1,), (1,)), ((), ())), preferred_element_type=F32) * m_scr[h]
            o = jnp.dot(sc.astype(BF16), vc, preferred_element_type=F32)
            qf = qb.astype(F32)
            qd = jnp.concatenate([(qf * dec_scr[h, 2]).astype(BF16), (qf * dec_scr[h, 3]).astype(BF16)], axis=1)
            o = o + jnp.dot(qd, st_scr[h, j], preferred_element_type=F32)
            mu = jnp.mean(o, axis=-1, keepdims=True)
            oc = o - mu
            var = jnp.mean(oc * oc, axis=-1, keepdims=True)
            y = oc * lax.rsqrt(var + EPS) * gn_ref[:, sl]
            gate = g_ref[pl.ds(r0, c), sl].astype(F32)
            o_ref[pl.ds(r0, c), sl] = (gate * jax.nn.sigmoid(gate) * y).astype(o_ref.dtype)
        return carry

    lax.fori_loop(0, nc, out_step, 0, unroll=3)


def _retention(log_g, q, k, v, g, gn, n_ctx):
    b, s, w = v.shape
    nc = s // RET_ROWS
    blk = lambda ww: pl.BlockSpec((None, s, ww), lambda bb: (bb, 0, 0))
    return pl.pallas_call(
        functools.partial(_retention_body, n_ctx=n_ctx),
        grid=(b,),
        in_specs=[pl.BlockSpec(memory_space=pltpu.SMEM),
                  blk(512), blk(512), blk(512), blk(512), _const_spec((1, 512))],
        out_specs=blk(512),
        out_shape=jax.ShapeDtypeStruct((b, s, 512), BF16),
        scratch_shapes=[pltpu.VMEM((RET_HEADS, nc, 256, 128), F32),
                        pltpu.VMEM((RET_HEADS, nc, 256, 128), BF16),
                        pltpu.VMEM((RET_HEADS, 4, RET_ROWS, LANE), F32),
                        pltpu.VMEM((RET_HEADS, RET_ROWS, RET_ROWS), F32)],
        compiler_params=_cparams(("arbitrary",)),
        name="retention",
    )(log_g, q, k, v, g, gn)


LRU_ROWS = 256
LRU_HALO = 16


def _tile_scan(a, bv, carry, reverse):
    row = lax.broadcasted_iota(jnp.int32, a.shape, 0)
    for dlt in (1, 2, 4):
        if reverse:
            a_s = pltpu.roll(a, SUBLANE - dlt, 0)
            b_s = pltpu.roll(bv, SUBLANE - dlt, 0)
            ok = row < SUBLANE - dlt
        else:
            a_s = pltpu.roll(a, dlt, 0)
            b_s = pltpu.roll(bv, dlt, 0)
            ok = row >= dlt
        bv = jnp.where(ok, a * b_s + bv, bv)
        a = jnp.where(ok, a * a_s, a)
    h = a * carry + bv
    new_carry = h[0:1, :] if reverse else h[SUBLANE - 1:SUBLANE, :]
    return h, new_carry


def _rglru_body(u_ref, g_ref, wc_ref, bc_ref, cw_ref, cb_ref, lam_ref, o_ref, a_scr, b_scr, *, n_ctx):
    s = u_ref.shape[0]
    w = LRU_WIDTH
    r = LRU_ROWS
    cdec = [-LRU_C * jax.nn.softplus(-lam_ref[d:d + 1, :]) for d in range(2)]
    zeros = jnp.zeros((LRU_HALO, w), F32)

    for ch in range(s // r):
        r0 = ch * r
        seg_start = r0 == 0 or r0 == n_ctx
        seg_end = r0 + r == n_ctx or r0 + r == s
        lo = r0 if seg_start else r0 - LRU_HALO
        hi = r0 + r if seg_end else r0 + r + LRU_HALO
        parts = [u_ref[lo:hi, :].astype(F32)]
        if seg_start:
            parts = [zeros] + parts
        if seg_end:
            parts = parts + [zeros]
        ext = jnp.concatenate(parts, axis=0) if len(parts) > 1 else parts[0]
        n = ext.shape[0]
        u = cb_ref[...]
        for j in range(CONV_W):
            sh = (CONV_PAD_LEFT - j) % n
            tap = ext if sh == 0 else pltpu.roll(ext, sh, 0)
            u = u + tap[LRU_HALO:LRU_HALO + r, :] * cw_ref[j:j + 1, :]
        gates = jnp.dot(u.astype(BF16), wc_ref[...], preferred_element_type=F32) + bc_ref[...]
        for d in range(2):
            rg = jax.nn.sigmoid(gates[:, (2 * d) * w:(2 * d + 1) * w])
            ig = jax.nn.sigmoid(gates[:, (2 * d + 1) * w:(2 * d + 2) * w])
            a = jnp.exp(rg * cdec[d])
            a_scr[d, r0:r0 + r, :] = a
            b_scr[d, r0:r0 + r, :] = jnp.sqrt(1.0 - a * a) * (ig * u)

    def fwd(t, carry):
        r0 = pl.multiple_of(t * SUBLANE, SUBLANE)
        h, carry = _tile_scan(a_scr[0, pl.ds(r0, SUBLANE), :], b_scr[0, pl.ds(r0, SUBLANE), :], carry, False)
        b_scr[0, pl.ds(r0, SUBLANE), :] = h
        return carry

    lax.fori_loop(0, s // SUBLANE, fwd, jnp.zeros((1, w), F32), unroll=4)

    def bwd(t0):
        def step(t, carry):
            r0 = pl.multiple_of((t0 - t) * SUBLANE, SUBLANE)
            h, carry = _tile_scan(a_scr[1, pl.ds(r0, SUBLANE), :], b_scr[1, pl.ds(r0, SUBLANE), :], carry, True)
            b_scr[1, pl.ds(r0, SUBLANE), :] = h
            return carry
        return step

    nct = n_ctx // SUBLANE
    carry = lax.fori_loop(0, nct, bwd(nct - 1), jnp.zeros((1, w), F32), unroll=4)
    lax.fori_loop(0, s // SUBLANE - nct, bwd(s // SUBLANE - 1), carry, unroll=4)

    def fin(j, carry):
        r0 = pl.multiple_of(j * r, r)
        hsum = b_scr[0, pl.ds(r0, r), :] + b_scr[1, pl.ds(r0, r), :]
        gate = g_ref[pl.ds(r0, r), :].astype(F32)
        o_ref[pl.ds(r0, r), :] = (hsum * jax.nn.gelu(gate)).astype(o_ref.dtype)
        return carry

    lax.fori_loop(0, s // r, fin, 0)


def _rglru(u, g, wcat, bcat, conv_w, conv_b, lam, n_ctx):
    b, s, w = u.shape
    blk = pl.BlockSpec((None, s, w), lambda bb: (bb, 0, 0))
    return pl.pallas_call(
        functools.partial(_rglru_body, n_ctx=n_ctx),
        grid=(b,),
        in_specs=[blk, blk, _const_spec(wcat.shape), _const_spec(bcat.shape),
                  _const_spec(conv_w.shape), _const_spec(conv_b.shape), _const_spec(lam.shape)],
        out_specs=blk,
        out_shape=jax.ShapeDtypeStruct((b, s, w), BF16),
        scratch_shapes=[pltpu.VMEM((2, s, w), F32), pltpu.VMEM((2, s, w), F32)],
        compiler_params=_cparams(("arbitrary",)),
        name="rglru",
    )(u, g, wcat, bcat, conv_w, conv_b, lam)


def _group_all(v, op):
    blk = 1
    while blk < EXPERTS_PER_GROUP:
        v = op(v, _swap_lanes(v, blk))
        blk *= 2
    return v


def _route_tile(scores, bias):
    neg = jnp.float32(-3.0e38)
    far = jnp.float32(1.0e9)
    lane = lax.broadcasted_iota(jnp.int32, scores.shape, 1)
    lane_f = lane.astype(F32)
    valid = lane < N_EXPERTS
    x = jnp.where(valid, scores + bias, neg)
    m1 = _group_all(x, jnp.maximum)
    i1 = _group_all(jnp.where(x == m1, lane_f, far), jnp.minimum)
    x2 = jnp.where(lane_f == i1, neg, x)
    m2 = _group_all(x2, jnp.maximum)
    i2 = _group_all(jnp.where(x2 == m2, lane_f, far), jnp.minimum)
    gs = jnp.where(valid, m1 + m2, neg)
    gmax = jnp.max(gs, axis=-1, keepdims=True)
    g0 = jnp.min(jnp.where(gs == gmax, lane_f, far), axis=-1, keepdims=True)
    in_best = jnp.where(lane_f >= g0, jnp.where(lane_f < g0 + EXPERTS_PER_GROUP, 1.0, 0.0), 0.0)
    sel1 = jnp.where(lane_f == i1, in_best, 0.0)
    sel2 = jnp.where(lane_f == i2, in_best, 0.0)
    both = sel1 + sel2
    wsum = jnp.sum(both * scores, axis=-1, keepdims=True)
    wmat = both * scores / wsum
    return wmat + pltpu.roll(sel1 + 2.0 * sel2, N_EXPERTS, 1)


def _merge_body(x_ref, mod_ref, g1_ref, g2_ref, ya_ref, yb_ref, yc_ref, yd_ref,
                wm_ref, bm_ref, wb_ref, wo_ref, rwh_ref, rwl_ref, xo_ref, h2_ref, sc_ref):
    d = x_ref.shape[1]
    x = x_ref[...]
    h = (_rms(x, g1_ref[...]) * (1.0 + mod_ref[1:2, :]) + mod_ref[0:1, :]).astype(BF16)
    acc = jnp.zeros(x.shape, F32)
    for n, y_ref in enumerate((ya_ref, yb_ref, yc_ref, yd_ref)):
        gate = jax.nn.sigmoid(jnp.dot(h, wm_ref[:, n * d:(n + 1) * d], preferred_element_type=F32)
                              + bm_ref[:, n * d:(n + 1) * d])
        acc = acc + gate * jnp.dot(y_ref[...], wb_ref[n], preferred_element_type=F32)
    y = jnp.dot(acc.astype(BF16), wo_ref[...], preferred_element_type=F32)
    xn = x + mod_ref[2:3, :] * y
    xo_ref[...] = xn
    h2 = _rms(xn, g2_ref[...]) * (1.0 + mod_ref[4:5, :]) + mod_ref[3:4, :]
    h2_ref[...] = h2
    h2_hi = h2.astype(BF16)
    h2_lo = (h2 - h2_hi.astype(F32)).astype(BF16)
    logits = (jnp.dot(h2_hi, rwh_ref[...], preferred_element_type=F32)
              + jnp.dot(h2_lo, rwh_ref[...], preferred_element_type=F32)
              + jnp.dot(h2_hi, rwl_ref[...], preferred_element_type=F32))
    sc_ref[...] = jax.nn.sigmoid(logits)


def _merge(x_all, mods, g1, g2, ys, wm, bm, wb, wo, rwh, rwl, n_ctx, with_ctx):
    b, s, d = x_all.shape
    tile0 = 0 if with_ctx else n_ctx // TM
    nt = s // TM - tile0
    tile = lambda w: pl.BlockSpec((None, TM, w), lambda bb, i: (bb, i + tile0, 0))
    sub = lambda w: pl.BlockSpec((None, TM, w), lambda bb, i: (bb, i, 0))
    so = nt * TM
    return pl.pallas_call(
        _merge_body,
        grid=(b, nt),
        in_specs=[tile(d),
                  pl.BlockSpec((None, None, 6, d), lambda bb, i: (bb, jnp.minimum(i + tile0, 1), 0, 0)),
                  _const_spec((1, d)), _const_spec((1, d)),
                  sub(BRANCH_W), tile(BRANCH_W), tile(BRANCH_W), sub(BRANCH_W),
                  _const_spec(wm.shape), _const_spec(bm.shape), _const_spec(wb.shape),
                  _const_spec(wo.shape), _const_spec(rwh.shape), _const_spec(rwl.shape)],
        out_specs=[sub(d), sub(d), sub(LANE)],
        out_shape=[jax.ShapeDtypeStruct((b, so, d), F32),
                   jax.ShapeDtypeStruct((b, so, d), F32),
                   jax.ShapeDtypeStruct((b, so, LANE), F32)],
        compiler_params=_cparams(("arbitrary", "arbitrary")),
        name="merge",
    )(x_all, mods, g1, g2, *ys, wm, bm, wb, wo, rwh, rwl)


def _expert_body(be_ref, nu_ref, x_ref, w1_ref, w3_ref, w2_ref, o_ref):
    j = pl.program_id(0)

    @pl.when(j < nu_ref[0])
    def _():
        x = x_ref[...].astype(BF16)
        a = jnp.dot(x, w1_ref[...].astype(BF16), preferred_element_type=F32)
        g = jnp.dot(x, w3_ref[...].astype(BF16), preferred_element_type=F32)
        hmid = (a * jax.nn.sigmoid(a) * g).astype(BF16)
        o_ref[...] = jnp.dot(hmid, w2_ref[...].astype(BF16), preferred_element_type=F32)

    @pl.when(j >= nu_ref[0])
    def _():
        o_ref[...] = jnp.zeros(o_ref.shape, o_ref.dtype)


def _expert_ffn(block_expert, n_used, xb, w1, w3, w2, layer):
    n_rows, d = xb.shape
    nb = n_rows // EXP_ROWS
    ff = w1.shape[-1]
    wspec = lambda r, c: pl.BlockSpec((None, None, r, c), lambda j, be, nu: (layer, be[j], 0, 0))
    grid_spec = pltpu.PrefetchScalarGridSpec(
        num_scalar_prefetch=2,
        grid=(nb,),
        in_specs=[pl.BlockSpec((EXP_ROWS, d), lambda j, be, nu: (jnp.minimum(j, nu[0] - 1), 0)),
                  wspec(d, ff), wspec(d, ff), wspec(ff, d)],
        out_specs=pl.BlockSpec((EXP_ROWS, d), lambda j, be, nu: (j, 0)),
    )
    return pl.pallas_call(
        _expert_body,
        grid_spec=grid_spec,
        out_shape=jax.ShapeDtypeStruct((n_rows, d), F32),
        compiler_params=_cparams(("arbitrary",)),
        name="expert_ffn",
    )(block_expert, n_used, xb, w1, w3, w2)


def _markers(route):
    lane = lax.broadcasted_iota(jnp.int32, route.shape, 1)
    return jnp.where(lane < N_EXPERTS, pltpu.roll(route, N_EXPERTS, 1), 0.0)


ROUTE_ROWS = 1024


def _route_body(pr_ref, rb_ref, rt_ref, cnt_ref):
    @pl.when(pl.program_id(0) == 0)
    def _():
        cnt_ref[...] = jnp.zeros(cnt_ref.shape, F32)

    rt = _route_tile(pr_ref[...], rb_ref[...])
    rt_ref[...] = rt
    cnt_ref[...] += jnp.sum(jnp.where(_markers(rt) > 0.0, 1.0, 0.0), axis=0, keepdims=True)


def _route(probs, rb):
    t = probs.shape[0]
    tile = pl.BlockSpec((ROUTE_ROWS, LANE), lambda i: (i, 0))
    return pl.pallas_call(
        _route_body,
        grid=(t // ROUTE_ROWS,),
        in_specs=[tile, _const_spec(rb.shape)],
        out_specs=[tile, pl.BlockSpec((SUBLANE, LANE), lambda i: (0, 0))],
        out_shape=[jax.ShapeDtypeStruct((t, LANE), F32), jax.ShapeDtypeStruct((SUBLANE, LANE), F32)],
        compiler_params=_cparams(("arbitrary",)),
        name="route",
    )(probs, rb)


def _rank_body(rt_ref, cnt_ref, slot_ref, run_scr, start_scr, tri_scr):
    rows = rt_ref.shape[0]

    @pl.when(pl.program_id(0) == 0)
    def _():
        cnt = cnt_ref[...]
        padded = jnp.floor((cnt + (EXP_ROWS - 1.0)) * (1.0 / EXP_ROWS)) * EXP_ROWS
        lane = lax.broadcasted_iota(jnp.int32, cnt.shape, 1)
        end = padded
        sh = 1
        while sh < LANE:
            end = end + jnp.where(lane >= sh, pltpu.roll(end, sh, 1), 0.0)
            sh *= 2
        start_scr[...] = end - padded
        run_scr[...] = jnp.zeros(run_scr.shape, F32)
        ri = lax.broadcasted_iota(jnp.int32, (rows, rows), 0)
        ci = lax.broadcasted_iota(jnp.int32, (rows, rows), 1)
        tri_scr[...] = jnp.where(ri > ci, 1.0, 0.0).astype(BF16)

    mk = _markers(rt_ref[...])
    p = jnp.where(mk > 0.0, 1.0, 0.0)
    before = jnp.dot(tri_scr[...], p.astype(BF16), preferred_element_type=F32)
    pos = before + run_scr[0:1, :] + start_scr[0:1, :]
    s1 = jnp.sum(jnp.where(mk == 1.0, pos, 0.0), axis=-1, keepdims=True)
    s2 = jnp.sum(jnp.where(mk == 2.0, pos, 0.0), axis=-1, keepdims=True)
    which = lax.broadcasted_iota(jnp.int32, (rows, TOP_K), 1)
    slot_ref[...] = jnp.where(which == 0, s1, s2).astype(jnp.int32)
    run_scr[...] += jnp.sum(p, axis=0, keepdims=True)


def _rank(route, counts):
    t = route.shape[0]
    return pl.pallas_call(
        _rank_body,
        grid=(t // ROUTE_ROWS,),
        in_specs=[pl.BlockSpec((ROUTE_ROWS, LANE), lambda i: (i, 0)), _const_spec((SUBLANE, LANE))],
        out_specs=pl.BlockSpec((ROUTE_ROWS, TOP_K), lambda i: (i, 0)),
        out_shape=jax.ShapeDtypeStruct((t, TOP_K), jnp.int32),
        scratch_shapes=[pltpu.VMEM((SUBLANE, LANE), F32), pltpu.VMEM((SUBLANE, LANE), F32),
                        pltpu.VMEM((ROUTE_ROWS, ROUTE_ROWS), BF16)],
        compiler_params=_cparams(("arbitrary",)),
        name="rank",
    )(route, counts)


def _block_experts(counts, n_blocks):
    cnt = counts[0, :N_EXPERTS].astype(jnp.int32)
    padded = (cnt + EXP_ROWS - 1) // EXP_ROWS * EXP_ROWS
    pad_end = jnp.cumsum(padded)
    first_row = jnp.arange(n_blocks, dtype=jnp.int32) * EXP_ROWS
    ended = jnp.sum((pad_end[None, :] <= first_row[:, None]).astype(jnp.int32), axis=1)
    block_expert = jnp.minimum(ended, N_EXPERTS - 1).astype(jnp.int32)
    n_used = (pad_end[-1] // EXP_ROWS).astype(jnp.int32).reshape(1)
    return block_expert, n_used, cnt, pad_end.astype(jnp.int32)


def _row_copy(src, src_row, dst, dst_row, sem):
    return pltpu.make_async_copy(src.at[pl.ds(src_row, 1)], dst.at[pl.ds(dst_row, 1)], sem)


def _dispatch_body(slot_ref, cnt_ref, end_ref, nu_ref, h_ref, xb_ref, zero_scr, sem, zsem):
    tm = h_ref.shape[0]

    @pl.when(pl.program_id(0) == 0)
    def _():
        zero_scr[...] = jnp.zeros(zero_scr.shape, zero_scr.dtype)

        def fill(e, carry):
            @pl.when(cnt_ref[e] > 0)
            def _():
                r0 = pl.multiple_of(end_ref[e] - EXP_ROWS, EXP_ROWS)
                pltpu.make_async_copy(zero_scr, xb_ref.at[pl.ds(r0, EXP_ROWS)], zsem).start()
            return carry

        def drain(e, carry):
            @pl.when(cnt_ref[e] > 0)
            def _():
                pltpu.make_async_copy(zero_scr, xb_ref.at[pl.ds(0, EXP_ROWS)], zsem).wait()
            return carry

        def fill_tail(j, carry):
            r0 = pl.multiple_of(j * EXP_ROWS, EXP_ROWS)
            pltpu.make_async_copy(zero_scr, xb_ref.at[pl.ds(r0, EXP_ROWS)], zsem).start()
            return carry

        def drain_tail(j, carry):
            pltpu.make_async_copy(zero_scr, xb_ref.at[pl.ds(0, EXP_ROWS)], zsem).wait()
            return carry

        n_blocks = xb_ref.shape[0] // EXP_ROWS
        lax.fori_loop(0, N_EXPERTS, fill, 0)
        lax.fori_loop(nu_ref[0], n_blocks, fill_tail, 0)
        lax.fori_loop(0, N_EXPERTS, drain, 0)
        lax.fori_loop(nu_ref[0], n_blocks, drain_tail, 0)

    def issue(t, carry):
        for k in range(TOP_K):
            _row_copy(h_ref, t, xb_ref, slot_ref[0, TOP_K * t + k], sem).start(priority=k % 2)
        return carry

    lax.fori_loop(0, tm, issue, 0, unroll=8)
    for k in range(TOP_K):
        pltpu.make_async_copy(h_ref, xb_ref.at[pl.ds(0, tm)], sem).wait()


def _dispatch(slot_tiles, cnt, pad_end, n_used, h2, n_rows):
    b, s, dh = h2.shape
    nt = b * s // TM
    smem = pl.BlockSpec(memory_space=pltpu.SMEM)
    return pl.pallas_call(
        _dispatch_body,
        grid=(nt,),
        in_specs=[pl.BlockSpec((None, 1, TOP_K * TM), lambda i: (i, 0, 0), memory_space=pltpu.SMEM),
                  smem, smem, smem,
                  pl.BlockSpec((TM, dh), lambda i: (i, 0))],
        out_specs=pl.BlockSpec(memory_space=pl.ANY),
        out_shape=jax.ShapeDtypeStruct((n_rows, dh), h2.dtype),
        scratch_shapes=[pltpu.VMEM((EXP_ROWS, dh), h2.dtype),
                        pltpu.SemaphoreType.DMA(()), pltpu.SemaphoreType.DMA(())],
        compiler_params=_cparams(("arbitrary",)),
        name="dispatch",
    )(slot_tiles, cnt, pad_end, n_used, h2.reshape(b * s, dh))


def _resid_body(slot_ref, x_ref, mod_ref, rt_ref, gf_ref, y_hbm, o_ref, ybuf0, ybuf1, sem, *, final):
    tm = x_ref.shape[0]
    ybuf = (ybuf0, ybuf1)

    def issue(t, carry):
        for k in range(TOP_K):
            _row_copy(y_hbm, slot_ref[0, TOP_K * t + k], ybuf[k], t, sem).start(priority=k % 2)
        return carry

    lax.fori_loop(0, tm, issue, 0, unroll=8)
    rt = rt_ref[...]
    mk = _markers(rt)
    w1 = jnp.sum(jnp.where(mk == 1.0, rt, 0.0), axis=-1, keepdims=True)
    w2 = jnp.sum(jnp.where(mk == 2.0, rt, 0.0), axis=-1, keepdims=True)
    for k in range(TOP_K):
        pltpu.make_async_copy(y_hbm.at[pl.ds(0, tm)], ybuf[k], sem).wait()
    f = w1 * ybuf0[...] + w2 * ybuf1[...]
    xn = x_ref[...] + mod_ref[5:6, :] * f
    if final:
        xn = _rms(xn, gf_ref[...])
    o_ref[...] = xn


def _moe_residual(slot_tiles, x_all, mods, route, gf, yblk, has_ctx, final):
    b, s, d = x_all.shape
    nt = s // TM
    tin = lambda ww: pl.BlockSpec((None, TM, ww), lambda bb, i: (bb, i, 0))
    mod_row = (lambda i: jnp.minimum(i, 1)) if has_ctx else (lambda i: 1)
    return pl.pallas_call(
        functools.partial(_resid_body, final=final),
        grid=(b, nt),
        in_specs=[pl.BlockSpec((None, 1, TOP_K * TM), lambda bb, i: (bb * nt + i, 0, 0),
                               memory_space=pltpu.SMEM),
                  tin(d),
                  pl.BlockSpec((None, None, 6, d), lambda bb, i: (bb, mod_row(i), 0, 0)),
                  tin(LANE), _const_spec((1, d)),
                  pl.BlockSpec(memory_space=pl.ANY)],
        out_specs=tin(d),
        out_shape=jax.ShapeDtypeStruct((b, s, d), F32),
        scratch_shapes=[pltpu.VMEM((TM, d), F32), pltpu.VMEM((TM, d), F32), pltpu.SemaphoreType.DMA(())],
        compiler_params=_cparams(("arbitrary", "arbitrary")),
        name="moe_residual",
    )(slot_tiles, x_all, mods, route, gf, yblk)


def kernel(x, c, ctx, c_ctx, ada_w, ada_b, norm1_g, norm2_g, w_in, mla_q_norm, mla_kv_norm, mla_w_uq,
           mla_w_ukv, ret_decay, ret_norm, lru_conv_w, lru_conv_b, lru_w_a, lru_b_a, lru_w_x, lru_b_x,
           lru_lambda, gqa_q_norm, gqa_k_norm, w_branch, w_merge, b_merge, w_out, router_w, router_bias,
           moe_w1, moe_w3, moe_w2, final_norm):
    b, seq, d = x.shape
    n_ctx = ctx.shape[1]
    depth = ada_w.shape[0]
    s = n_ctx + seq
    assert n_ctx == TM and seq % TM == 0 and seq % GRID_W == 0

    r_pad = -(-(b + 1) // SUBLANE) * SUBLANE
    cc = jnp.zeros((r_pad, d), F32).at[:b].set(c).at[b].set(c_ctx)
    mods_all = _ada_mods(cc, ada_w, ada_b)

    mla_cos, mla_sin = _rope_slot_tables(n_ctx, seq, MLA_ROPE, MLA_NOPE)
    hd_cos, hd_sin = _rope_slot_tables(n_ctx, seq, GQA_HEAD_DIM, 0)

    in_cols = _in_proj_columns()
    uq_cols, ukv_cols = _mla_up_columns()
    rw = jnp.concatenate([router_w.astype(F32), jnp.zeros((d, LANE - N_EXPERTS), F32)], axis=1)
    rw_hi = rw.astype(BF16)
    rw_lo = (rw - rw_hi.astype(F32)).astype(BF16)
    rb = jnp.concatenate([router_bias.astype(F32), jnp.zeros((LANE - N_EXPERTS,), F32)])[None]

    x_all = jnp.concatenate([ctx, x], axis=1)
    out = None
    for l in range(depth):
        last = l == depth - 1
        m = mods_all[l].reshape(r_pad, 6, d)
        mods = jnp.stack([jnp.broadcast_to(m[b], (b, 6, d)), m[:b]], axis=1)

        mla_scale = (MLA_NOPE + MLA_ROPE) ** -0.5
        gqa_scale = GQA_HEAD_DIM ** -0.5

        def swap_gain(g):
            g4 = g.reshape(2, 2, GQA_HEAD_DIM // 4)
            return g4[:, ::-1, :].reshape(GQA_HEAD_DIM)

        def gain_slot(g):
            return jnp.concatenate([g, jnp.zeros((LANE - GQA_HEAD_DIM,), F32)])

        gqc, gqs = gain_slot(gqa_q_norm[l]), gain_slot(swap_gain(gqa_q_norm[l]))
        gkc, gks = gain_slot(gqa_k_norm[l]), gain_slot(swap_gain(gqa_k_norm[l]))
        tabs = jnp.stack([
            mla_cos * mla_scale, mla_sin * mla_scale, mla_cos, mla_sin,
            hd_cos, hd_sin, hd_cos * RET_DK ** -0.5, hd_sin * RET_DK ** -0.5,
            hd_cos * gqc * gqa_scale, hd_sin * gqs * gqa_scale, hd_cos * gkc, hd_sin * gks])

        win_p = _take_cols(w_in[l], in_cols).astype(BF16)
        wuq_p = _take_cols(mla_w_uq[l], uq_cols).astype(BF16)
        wukv_p = _take_cols(mla_w_ukv[l], ukv_cols).astype(BF16)

        (mq, mk, mv, rq, rk, rv, rg, lu, lg, gq, gk, gv) = _inproj(
            x_all, mods, norm1_g[l][None], tabs, win_p, wuq_p, wukv_p,
            mla_q_norm[l][None], mla_kv_norm[l][None])

        ya = _attention(mq, mk, mv, MLA_HEADS, MLA_HEADS, n_ctx, not last)
        yd = _attention(gq, gk, gv, GQA_HEADS, GQA_KV_HEADS, n_ctx, not last)

        log_g = -jax.nn.softplus(-ret_decay[l].astype(F32))
        yb = _retention(log_g, rq, rk, rv, rg, ret_norm[l][None], n_ctx)

        eye = jnp.eye(LRU_BLOCKS, dtype=F32)

        def block_diag(wblk):
            return jnp.einsum('ncd,nm->ncmd', wblk, eye).reshape(LRU_WIDTH, LRU_WIDTH)

        wcat = jnp.concatenate([block_diag(lru_w_a[l, 0]), block_diag(lru_w_x[l, 0]),
                                block_diag(lru_w_a[l, 1]), block_diag(lru_w_x[l, 1])], axis=1).astype(BF16)
        bcat = jnp.concatenate([lru_b_a[l, 0], lru_b_x[l, 0], lru_b_a[l, 1], lru_b_x[l, 1]])[None]
        yc = _rglru(lu, lg, wcat, bcat, lru_conv_w[l], lru_conv_b[l][None], lru_lambda[l], n_ctx)

        x_all, h2, probs = _merge(
            x_all, mods, norm1_g[l][None], norm2_g[l][None], (ya, yb, yc, yd),
            w_merge[l].astype(BF16), b_merge[l][None], w_branch[l].astype(BF16), w_out[l].astype(BF16),
            rw_hi, rw_lo, n_ctx, not last)

        t = b * (seq if last else s)
        assert t % ROUTE_ROWS == 0
        n_blocks = -(-t * TOP_K // EXP_ROWS) + N_EXPERTS
        route, counts = _route(probs.reshape(t, LANE), rb)
        slot_tiles = _rank(route, counts).reshape(t // TM, 1, TOP_K * TM)
        route = route.reshape(b, t // b, LANE)
        block_expert, n_used, cnt, pad_end = _block_experts(counts, n_blocks)
        xb = _dispatch(slot_tiles, cnt, pad_end, n_used, h2, n_blocks * EXP_ROWS)
        yblk = _expert_ffn(block_expert, n_used, xb, moe_w1, moe_w3, moe_w2, l)
        out = _moe_residual(slot_tiles, x_all, mods, route, final_norm[None], yblk, not last, last)
        x_all = out
    return out
```

```python
import functools

import numpy as np
import jax
import jax.numpy as jnp
from jax import lax
from jax.experimental import pallas as pl
from jax.experimental.pallas import tpu as pltpu

F32 = jnp.float32
BF16 = jnp.bfloat16

LANE = 128
SUBLANE = 8
VMEM_LIMIT = 56 * 1024 * 1024

GRID_W = 64
ROPE_BASE = 10000.0
EPS = 1e-6
MLA_HEADS, MLA_NOPE, MLA_ROPE, MLA_V = 8, 64, 32, 64
MLA_Q_LORA, MLA_KV_LORA = 256, 128
RET_HEADS, RET_DK, RET_DV, RET_CHUNK = 4, 64, 128, 128
LRU_WIDTH, LRU_BLOCKS, LRU_C, CONV_W, CONV_PAD_LEFT = 512, 8, 8.0, 4, 2
LRU_BLOCK_W = LRU_WIDTH // LRU_BLOCKS
GQA_HEADS, GQA_KV_HEADS, GQA_HEAD_DIM = 8, 2, 64
N_BRANCH, BRANCH_W = 4, 512
N_EXPERTS, N_GROUPS, TOP_K, EXPERT_FF, MOE_BLOCK = 64, 8, 2, 256, 128
EXPERTS_PER_GROUP = N_EXPERTS // N_GROUPS

IN_SPLITS = (MLA_Q_LORA, MLA_KV_LORA, MLA_ROPE,
             RET_HEADS * RET_DK, RET_HEADS * RET_DK, RET_HEADS * RET_DV, RET_HEADS * RET_DV,
             LRU_WIDTH, LRU_WIDTH,
             GQA_HEADS * GQA_HEAD_DIM, GQA_KV_HEADS * GQA_HEAD_DIM, GQA_KV_HEADS * GQA_HEAD_DIM)
IN_OFF = tuple(int(o) for o in np.cumsum((0,) + IN_SPLITS))
D_IN = IN_OFF[-1]

TM = 256
EXP_ROWS = 256
HEAD_SLOT = LANE
ONE_LANE = 64

ZP_CQ, ZP_CKV, ZP_KR = 0, 256, 384
ZP_RQ, ZP_RK, ZP_RV, ZP_RG = 512, 1024, 1536, 2048
ZP_LU, ZP_LG = 2560, 3072
ZP_GQ, ZP_GK, ZP_GV = 3584, 4608, 4864
ZP_W = 5120


def _cparams(sem):
    return pltpu.CompilerParams(dimension_semantics=sem, vmem_limit_bytes=VMEM_LIMIT)


def _const_spec(shape):
    nd = len(shape)
    return pl.BlockSpec(shape, lambda *_: (0,) * nd, pipeline_mode=pl.Buffered(1))


PARTNER = LANE // 2


def _rot_lanes(rot_dim):
    q = rot_dim // 4
    e = np.arange(rot_dim)
    blk, o = e // q, e % q
    return np.where(blk == 0, o, np.where(blk == 1, PARTNER + o, np.where(blk == 2, q + o, PARTNER + q + o)))


HEAD_LANES = _rot_lanes(GQA_HEAD_DIM)
MLA_ROT_LANES = _rot_lanes(MLA_ROPE)
MLA_NOPE_LANES = np.array([l for l in range(MLA_NOPE + MLA_ROPE)
                           if l not in set(MLA_ROT_LANES.tolist())])


def _in_proj_columns():
    idx = np.full((ZP_W,), D_IN, np.int64)
    o = IN_OFF
    idx[ZP_CQ:ZP_CQ + 256] = o[0] + np.arange(256)
    idx[ZP_CKV:ZP_CKV + 128] = o[1] + np.arange(128)
    idx[ZP_KR + MLA_ROT_LANES] = o[2] + np.arange(MLA_ROPE)
    for h in range(RET_HEADS):
        idx[ZP_RQ + h * 128 + HEAD_LANES] = o[3] + h * 64 + np.arange(64)
        idx[ZP_RK + h * 128 + HEAD_LANES] = o[4] + h * 64 + np.arange(64)
    idx[ZP_RV:ZP_RV + 512] = o[5] + np.arange(512)
    idx[ZP_RG:ZP_RG + 512] = o[6] + np.arange(512)
    idx[ZP_LU:ZP_LU + 512] = o[7] + np.arange(512)
    idx[ZP_LG:ZP_LG + 512] = o[8] + np.arange(512)
    for h in range(GQA_HEADS):
        idx[ZP_GQ + h * 128 + HEAD_LANES] = o[9] + h * 64 + np.arange(64)
    for h in range(GQA_KV_HEADS):
        idx[ZP_GK + h * 128 + HEAD_LANES] = o[10] + h * 64 + np.arange(64)
        idx[ZP_GV + h * 128:ZP_GV + h * 128 + 64] = o[11] + h * 64 + np.arange(64)
    return idx


def _mla_up_columns():
    dq = MLA_NOPE + MLA_ROPE
    dkv = MLA_NOPE + MLA_V
    qi = np.full((MLA_HEADS * 128,), MLA_HEADS * dq, np.int64)
    ki = np.full((MLA_HEADS * 128,), MLA_HEADS * dkv, np.int64)
    vi = np.full((MLA_HEADS * 128,), MLA_HEADS * dkv, np.int64)
    for h in range(MLA_HEADS):
        qi[h * 128 + MLA_NOPE_LANES] = h * dq + np.arange(MLA_NOPE)
        qi[h * 128 + MLA_ROT_LANES] = h * dq + MLA_NOPE + np.arange(MLA_ROPE)
        ki[h * 128 + MLA_NOPE_LANES] = h * dkv + np.arange(MLA_NOPE)
        vi[h * 128:h * 128 + MLA_V] = h * dkv + MLA_NOPE + np.arange(MLA_V)
    return qi, np.concatenate([ki, vi])


def _take_cols(w, idx):
    wz = jnp.concatenate([w, jnp.zeros((w.shape[0], 1), w.dtype)], axis=1)
    return jnp.take(wz, jnp.asarray(idx, jnp.int32), axis=1)


def _rope_slot_tables(n_ctx, seq, rot_dim, lanes):
    half = rot_dim // 2
    pos = np.arange(seq)
    inv = ROPE_BASE ** (-np.arange(0, half, 2, dtype=np.float64) / half)
    ar = (pos // GRID_W)[:, None] * inv
    ac = (pos % GRID_W)[:, None] * inv
    cos = np.concatenate([np.cos(ar), np.cos(ar), np.cos(ac), np.cos(ac)], axis=1)
    sin = np.concatenate([-np.sin(ar), np.sin(ar), -np.sin(ac), np.sin(ac)], axis=1)
    cos_t = np.ones((n_ctx + seq, LANE), np.float32)
    sin_t = np.zeros((n_ctx + seq, LANE), np.float32)
    cos_t[n_ctx:, lanes] = cos
    sin_t[n_ctx:, lanes] = sin
    return cos_t, sin_t


def _swap_lanes(x, blk):
    n = x.shape[-1]
    lane = lax.broadcasted_iota(jnp.int32, x.shape, x.ndim - 1)
    up = pltpu.roll(x, n - blk, x.ndim - 1)
    dn = pltpu.roll(x, blk, x.ndim - 1)
    return jnp.where((lane % (2 * blk)) < blk, up, dn)


def _rms(x, g):
    return x * lax.rsqrt(jnp.mean(x * x, axis=-1, keepdims=True) + EPS) * g


def _ada_body(c_ref, w_ref, b_ref, o_ref):
    c = c_ref[...]
    s = (c * jax.nn.sigmoid(c)).astype(BF16)
    o_ref[...] = jnp.dot(s, w_ref[...].astype(BF16), preferred_element_type=F32) + b_ref[...]


def _ada_mods(cc, ada_w, ada_b):
    depth, d, n = ada_w.shape
    r = cc.shape[0]
    tn = 1536
    return pl.pallas_call(
        _ada_body,
        grid=(depth, n // tn),
        in_specs=[pl.BlockSpec((r, d), lambda l, j: (0, 0)),
                  pl.BlockSpec((None, d, tn), lambda l, j: (l, 0, j)),
                  pl.BlockSpec((None, 1, tn), lambda l, j: (l, 0, j))],
        out_specs=pl.BlockSpec((None, r, tn), lambda l, j: (l, 0, j)),
        out_shape=jax.ShapeDtypeStruct((depth, r, n), F32),
        compiler_params=_cparams(("arbitrary", "arbitrary")),
        name="ada_mods",
    )(cc, ada_w, ada_b.reshape(depth, 1, n))


def _inproj_body(x_ref, mod_ref, g1_ref, tab_ref, win_ref, wuq_ref, wukv_ref, gq_ref, gkv_ref, hg_ref,
                 mq_ref, mk_ref, mv_ref, rq_ref, rk_ref, rv_ref, rg_ref, lu_ref, lg_ref,
                 gq_out, gk_out, gv_out):
    x = x_ref[...]
    shift = mod_ref[0:1, :]
    scale = mod_ref[1:2, :]
    h = _rms(x, g1_ref[...]) * (1.0 + scale) + shift
    hb = h.astype(BF16)

    def proj(c0, c1):
        return jnp.dot(hb, win_ref[:, c0:c1], preferred_element_type=F32)

    lane = lax.broadcasted_iota(jnp.int32, (x.shape[0], LANE), 1)
    one_col = jnp.where(lane == ONE_LANE, 1.0, 0.0).astype(F32)

    def rope(v, ci):
        return v * tab_ref[ci] + pltpu.roll(v, PARTNER, 1) * tab_ref[ci + 1]

    mla_scale = (MLA_NOPE + MLA_ROPE) ** -0.5
    gqa_scale = GQA_HEAD_DIM ** -0.5

    cq = proj(ZP_CQ, ZP_CQ + 256)
    qn = _rms(cq, gq_ref[...]).astype(BF16)
    q = jnp.dot(qn, wuq_ref[...], preferred_element_type=F32)
    for hh in range(MLA_HEADS):
        sl = slice(hh * 128, (hh + 1) * 128)
        mq_ref[:, sl] = rope(q[:, sl] * mla_scale, 0).astype(BF16)
    ckv = proj(ZP_CKV, ZP_CKV + 128)
    kvn = _rms(ckv, gkv_ref[...]).astype(BF16)
    kv = jnp.dot(kvn, wukv_ref[...], preferred_element_type=F32)
    kr = rope(proj(ZP_KR, ZP_KR + 128), 0)
    for hh in range(MLA_HEADS):
        sl = slice(hh * 128, (hh + 1) * 128)
        mk_ref[sl, :] = (kv[:, sl] + kr).T.astype(BF16)
        mv_ref[:, sl] = (kv[:, MLA_HEADS * 128 + hh * 128:MLA_HEADS * 128 + (hh + 1) * 128] + one_col).astype(BF16)

    rq = proj(ZP_RQ, ZP_RQ + 512)
    rk = proj(ZP_RK, ZP_RK + 512)
    for hh in range(RET_HEADS):
        sl = slice(hh * 128, (hh + 1) * 128)
        rq_ref[:, sl] = rope(rq[:, sl], 2).astype(BF16)
        rk_ref[:, sl] = rope(rk[:, sl] * RET_DK ** -0.5, 2).astype(BF16)
    rv_ref[...] = proj(ZP_RV, ZP_RV + 512).astype(BF16)
    rg_ref[...] = proj(ZP_RG, ZP_RG + 512).astype(BF16)

    lu_ref[...] = proj(ZP_LU, ZP_LU + 512).astype(BF16)
    lg_ref[...] = proj(ZP_LG, ZP_LG + 512).astype(BF16)

    gq = proj(ZP_GQ, ZP_GQ + 1024)
    for hh in range(GQA_HEADS):
        sl = slice(hh * 128, (hh + 1) * 128)
        v = gq[:, sl]
        v = v * lax.rsqrt(jnp.sum(v * v, axis=-1, keepdims=True) * (1.0 / GQA_HEAD_DIM) + EPS)
        gq_out[:, sl] = rope(v * (hg_ref[0:1, :] * gqa_scale), 2).astype(BF16)
    gk = proj(ZP_GK, ZP_GK + 256)
    gv = proj(ZP_GV, ZP_GV + 256)
    for hh in range(GQA_KV_HEADS):
        sl = slice(hh * 128, (hh + 1) * 128)
        v = gk[:, sl]
        v = v * lax.rsqrt(jnp.sum(v * v, axis=-1, keepdims=True) * (1.0 / GQA_HEAD_DIM) + EPS)
        gk_out[sl, :] = rope(v * hg_ref[1:2, :], 2).T.astype(BF16)
        gv_out[:, sl] = (gv[:, sl] + one_col).astype(BF16)


def _inproj(x_all, mods, g1, tabs, win_p, wuq_p, wukv_p, gq, gkv, head_gains):
    b, s, d = x_all.shape
    nt = s // TM
    widths = (1024, 1024, 1024, 512, 512, 512, 512, 512, 512, 1024, 256, 256)
    tile = lambda w: pl.BlockSpec((None, TM, w), lambda i, bb: (bb, i, 0))
    ttile = lambda w: pl.BlockSpec((None, w, TM), lambda i, bb: (bb, 0, i))
    transposed = (1, 10)
    return pl.pallas_call(
        _inproj_body,
        grid=(nt, b),
        in_specs=[tile(d),
                  pl.BlockSpec((None, None, 6, d), lambda i, bb: (bb, jnp.minimum(i, 1), 0, 0)),
                  _const_spec((1, d)),
                  pl.BlockSpec((4, TM, LANE), lambda i, bb: (0, i, 0)),
                  _const_spec(win_p.shape), _const_spec(wuq_p.shape), _const_spec(wukv_p.shape),
                  _const_spec((1, MLA_Q_LORA)), _const_spec((1, MLA_KV_LORA)), _const_spec((2, LANE))],
        out_specs=[ttile(w) if j in transposed else tile(w) for j, w in enumerate(widths)],
        out_shape=[jax.ShapeDtypeStruct((b, w, s) if j in transposed else (b, s, w), BF16)
                   for j, w in enumerate(widths)],
        compiler_params=_cparams(("arbitrary", "arbitrary")),
        name="in_proj",
    )(x_all, mods, g1, tabs, win_p, wuq_p, wukv_p, gq, gkv, head_gains)


ATT_TILES = 2


def _attn_body(*refs, heads, kv_heads, n_q):
    q_refs, (kt_ref, v_ref, o_ref) = refs[:n_q], refs[n_q:]
    grp = heads // kv_heads
    for hp in range(heads // 2):
        outs = []
        for h in (2 * hp, 2 * hp + 1):
            g = h // grp
            sl = slice(h * 128, (h + 1) * 128)
            q = q_refs[0][:, sl] if n_q == 1 else jnp.concatenate([r[:, sl] for r in q_refs], axis=0)
            kt = kt_ref[g * 128:(g + 1) * 128, :]
            v = v_ref[:, g * 128:(g + 1) * 128]
            s = jnp.dot(q, kt, preferred_element_type=F32).astype(BF16)
            m = jnp.max(s, axis=-1, keepdims=True)
            p = jnp.exp(s - m)
            o = jnp.dot(p, v, preferred_element_type=F32)
            outs.append(o[:, :64] / o[:, ONE_LANE:ONE_LANE + 1])
        o_ref[:, hp * 128:(hp + 1) * 128] = jnp.concatenate(outs, axis=1).astype(o_ref.dtype)


def _attention_lat(q, kt, v, heads, kv_heads, n_ctx):
    b, s, _ = q.shape
    off = n_ctx // TM
    rows = ATT_TILES * TM
    nq = (s - n_ctx) // rows
    qspec = lambda r: pl.BlockSpec((None, TM, heads * 128), lambda bb, j: (bb, off + ATT_TILES * j + r, 0))
    return pl.pallas_call(
        functools.partial(_attn_body, heads=heads, kv_heads=kv_heads, n_q=ATT_TILES),
        grid=(b, nq),
        in_specs=[qspec(r) for r in range(ATT_TILES)] + [
            pl.BlockSpec((None, kv_heads * 128, s), lambda bb, j: (bb, 0, 0)),
            pl.BlockSpec((None, s, kv_heads * 128), lambda bb, j: (bb, 0, 0))],
        out_specs=pl.BlockSpec((None, rows, heads * 64), lambda bb, j: (bb, j, 0)),
        out_shape=jax.ShapeDtypeStruct((b, nq * rows, heads * 64), BF16),
        compiler_params=_cparams(("arbitrary", "arbitrary")),
        name="attention_lat_h%d_kv%d" % (heads, kv_heads),
    )(*([q] * ATT_TILES), kt, v)


def _attention_ctx(q, kt, v, heads, kv_heads, n_ctx):
    b = q.shape[0]
    return pl.pallas_call(
        functools.partial(_attn_body, heads=heads, kv_heads=kv_heads, n_q=1),
        grid=(b,),
        in_specs=[pl.BlockSpec((None, n_ctx, heads * 128), lambda bb: (bb, 0, 0)),
                  pl.BlockSpec((None, kv_heads * 128, n_ctx), lambda bb: (bb, 0, 0)),
                  pl.BlockSpec((None, n_ctx, kv_heads * 128), lambda bb: (bb, 0, 0))],
        out_specs=pl.BlockSpec((None, n_ctx, heads * 64), lambda bb: (bb, 0, 0)),
        out_shape=jax.ShapeDtypeStruct((b, n_ctx, heads * 64), BF16),
        compiler_params=_cparams(("arbitrary",)),
        name="attention_ctx_h%d_kv%d" % (heads, kv_heads),
    )(q, kt, v)


def _attention(q, kt, v, heads, kv_heads, n_ctx, with_ctx):
    y = _attention_lat(q, kt, v, heads, kv_heads, n_ctx)
    if with_ctx:
        y = jnp.concatenate([_attention_ctx(q, kt, v, heads, kv_heads, n_ctx), y], axis=1)
    return y


RET_ROWS = 256


def _retention_body(lg_ref, q_ref, k_ref, v_ref, g_ref, gn_ref, o_ref,
                    kv_scr, st_scr, dec_scr, m_scr, *, n_ctx):
    c = RET_ROWS
    s = q_ref.shape[0]
    nc = s // c
    nctx = n_ctx // c
    back_order = list(range(nctx - 1, -1, -1)) + list(range(nc - 1, nctx - 1, -1))
    pos = lax.broadcasted_iota(jnp.int32, (c, LANE), 0).astype(F32)
    ri = lax.broadcasted_iota(jnp.int32, (c, c), 0)
    ci = lax.broadcasted_iota(jnp.int32, (c, c), 1)
    diff = (ri - ci).astype(F32)
    heads = [(h, slice(h * 128, (h + 1) * 128)) for h in range(RET_HEADS)]

    for h, _ in heads:
        lgf = lg_ref[0, h]
        lgb = lg_ref[1, h]
        dec_scr[h, 0] = jnp.exp(lgf * (c - 1.0 - pos))
        dec_scr[h, 1] = jnp.exp(lgb * pos)
        dec_scr[h, 2] = jnp.exp(lgf * (pos + 1.0))
        dec_scr[h, 3] = jnp.exp(lgb * (c - pos))
        m_scr[h] = jnp.where(diff >= 0, jnp.exp(lgf * jnp.maximum(diff, 0.0)),
                             jnp.exp(lgb * jnp.maximum(-diff, 0.0)))

    def kv_step(j, carry):
        r0 = pl.multiple_of(j * c, c)
        for h, sl in heads:
            kc = k_ref[pl.ds(r0, c), sl].astype(F32)
            vc = v_ref[pl.ds(r0, c), sl]
            kk = jnp.concatenate([(kc * dec_scr[h, 0]).astype(BF16), (kc * dec_scr[h, 1]).astype(BF16)], axis=1)
            kv_scr[h, j] = lax.dot_general(kk, vc, (((0,), (0,)), ((), ())), preferred_element_type=F32)
        return carry

    lax.fori_loop(0, nc, kv_step, 0, unroll=3)

    for h, _ in heads:
        gcf = jnp.exp(lg_ref[0, h] * c)
        gcb = jnp.exp(lg_ref[1, h] * c)
        sf = jnp.zeros((128, 128), F32)
        for j in range(nc):
            st_scr[h, j, 0:128, :] = sf.astype(BF16)
            sf = sf * gcf + kv_scr[h, j, 0:128, :]
        sb = jnp.zeros((128, 128), F32)
        for j in back_order:
            st_scr[h, j, 128:256, :] = sb.astype(BF16)
            sb = sb * gcb + kv_scr[h, j, 128:256, :]

    def out_step(j, carry):
        r0 = pl.multiple_of(j * c, c)
        for h, sl in heads:
            qb = q_ref[pl.ds(r0, c), sl]
            kb = k_ref[pl.ds(r0, c), sl]
            vc = v_ref[pl.ds(r0, c), sl]
            sc = lax.dot_general(qb, kb, (((1,), (1,)), ((), ())), preferred_element_type=F32) * m_scr[h]
            o = jnp.dot(sc.astype(BF16), vc, preferred_element_type=F32)
            qf = qb.astype(F32)
            qd = jnp.concatenate([(qf * dec_scr[h, 2]).astype(BF16), (qf * dec_scr[h, 3]).astype(BF16)], axis=1)
            o = o + jnp.dot(qd, st_scr[h, j], preferred_element_type=F32)
            mu = jnp.mean(o, axis=-1, keepdims=True)
            oc = o - mu
            var = jnp.mean(oc * oc, axis=-1, keepdims=True)
            y = oc * lax.rsqrt(var + EPS) * gn_ref[:, sl]
            gate = g_ref[pl.ds(r0, c), sl].astype(F32)
            o_ref[pl.ds(r0, c), sl] = (gate * jax.nn.sigmoid(gate) * y).astype(o_ref.dtype)
        return carry

    lax.fori_loop(0, nc, out_step, 0, unroll=3)


def _retention(log_g, q, k, v, g, gn, n_ctx):
    b, s, w = v.shape
    nc = s // RET_ROWS
    blk = lambda ww: pl.BlockSpec((None, s, ww), lambda bb: (bb, 0, 0))
    return pl.pallas_call(
        functools.partial(_retention_body, n_ctx=n_ctx),
        grid=(b,),
        in_specs=[pl.BlockSpec(memory_space=pltpu.SMEM),
                  blk(512), blk(512), blk(512), blk(512), _const_spec((1, 512))],
        out_specs=blk(512),
        out_shape=jax.ShapeDtypeStruct((b, s, 512), BF16),
        scratch_shapes=[pltpu.VMEM((RET_HEADS, nc, 256, 128), F32),
                        pltpu.VMEM((RET_HEADS, nc, 256, 128), BF16),
                        pltpu.VMEM((RET_HEADS, 4, RET_ROWS, LANE), F32),
                        pltpu.VMEM((RET_HEADS, RET_ROWS, RET_ROWS), F32)],
        compiler_params=_cparams(("arbitrary",)),
        name="retention",
    )(log_g, q, k, v, g, gn)


LRU_ROWS = 256
LRU_HALO = 16


def _tile_scan(a, bv, carry, reverse):
    row = lax.broadcasted_iota(jnp.int32, a.shape, 0)
    for dlt in (1, 2, 4):
        if reverse:
            a_s = pltpu.roll(a, SUBLANE - dlt, 0)
            b_s = pltpu.roll(bv, SUBLANE - dlt, 0)
            ok = row < SUBLANE - dlt
        else:
            a_s = pltpu.roll(a, dlt, 0)
            b_s = pltpu.roll(bv, dlt, 0)
            ok = row >= dlt
        bv = jnp.where(ok, a * b_s + bv, bv)
        a = jnp.where(ok, a * a_s, a)
    h = a * carry + bv
    new_carry = h[0:1, :] if reverse else h[SUBLANE - 1:SUBLANE, :]
    return h, new_carry


def _rglru_body(u_ref, g_ref, wc_ref, bc_ref, cw_ref, cb_ref, lam_ref, o_ref, a_scr, b_scr, *, n_ctx):
    s = u_ref.shape[0]
    w = LRU_WIDTH
    r = LRU_ROWS
    cdec = [-LRU_C * jax.nn.softplus(-lam_ref[d:d + 1, :]) for d in range(2)]
    zeros = jnp.zeros((LRU_HALO, w), F32)

    for ch in range(s // r):
        r0 = ch * r
        seg_start = r0 == 0 or r0 == n_ctx
        seg_end = r0 + r == n_ctx or r0 + r == s
        lo = r0 if seg_start else r0 - LRU_HALO
        hi = r0 + r if seg_end else r0 + r + LRU_HALO
        parts = [u_ref[lo:hi, :].astype(F32)]
        if seg_start:
            parts = [zeros] + parts
        if seg_end:
            parts = parts + [zeros]
        ext = jnp.concatenate(parts, axis=0) if len(parts) > 1 else parts[0]
        n = ext.shape[0]
        u = cb_ref[...]
        for j in range(CONV_W):
            sh = (CONV_PAD_LEFT - j) % n
            tap = ext if sh == 0 else pltpu.roll(ext, sh, 0)
            u = u + tap[LRU_HALO:LRU_HALO + r, :] * cw_ref[j:j + 1, :]
        gates = jnp.dot(u.astype(BF16), wc_ref[...], preferred_element_type=F32) + bc_ref[...]
        for d in range(2):
            rg = jax.nn.sigmoid(gates[:, (2 * d) * w:(2 * d + 1) * w])
            ig = jax.nn.sigmoid(gates[:, (2 * d + 1) * w:(2 * d + 2) * w])
            a = jnp.exp(rg * cdec[d])
            a_scr[d, r0:r0 + r, :] = a
            b_scr[d, r0:r0 + r, :] = jnp.sqrt(1.0 - a * a) * (ig * u)

    def fwd(t, carry):
        r0 = pl.multiple_of(t * SUBLANE, SUBLANE)
        h, carry = _tile_scan(a_scr[0, pl.ds(r0, SUBLANE), :], b_scr[0, pl.ds(r0, SUBLANE), :], carry, False)
        b_scr[0, pl.ds(r0, SUBLANE), :] = h
        return carry

    lax.fori_loop(0, s // SUBLANE, fwd, jnp.zeros((1, w), F32), unroll=4)

    def bwd(t0):
        def step(t, carry):
            r0 = pl.multiple_of((t0 - t) * SUBLANE, SUBLANE)
            h, carry = _tile_scan(a_scr[1, pl.ds(r0, SUBLANE), :], b_scr[1, pl.ds(r0, SUBLANE), :], carry, True)
            b_scr[1, pl.ds(r0, SUBLANE), :] = h
            return carry
        return step

    nct = n_ctx // SUBLANE
    carry = lax.fori_loop(0, nct, bwd(nct - 1), jnp.zeros((1, w), F32), unroll=4)
    lax.fori_loop(0, s // SUBLANE - nct, bwd(s // SUBLANE - 1), carry, unroll=4)

    def fin(j, carry):
        r0 = pl.multiple_of(j * r, r)
        hsum = b_scr[0, pl.ds(r0, r), :] + b_scr[1, pl.ds(r0, r), :]
        gate = g_ref[pl.ds(r0, r), :].astype(F32)
        o_ref[pl.ds(r0, r), :] = (hsum * jax.nn.gelu(gate)).astype(o_ref.dtype)
        return carry

    lax.fori_loop(0, s // r, fin, 0)


def _rglru(u, g, wcat, bcat, conv_w, conv_b, lam, n_ctx):
    b, s, w = u.shape
    blk = pl.BlockSpec((None, s, w), lambda bb: (bb, 0, 0))
    return pl.pallas_call(
        functools.partial(_rglru_body, n_ctx=n_ctx),
        grid=(b,),
        in_specs=[blk, blk, _const_spec(wcat.shape), _const_spec(bcat.shape),
                  _const_spec(conv_w.shape), _const_spec(conv_b.shape), _const_spec(lam.shape)],
        out_specs=blk,
        out_shape=jax.ShapeDtypeStruct((b, s, w), BF16),
        scratch_shapes=[pltpu.VMEM((2, s, w), F32), pltpu.VMEM((2, s, w), F32)],
        compiler_params=_cparams(("arbitrary",)),
        name="rglru",
    )(u, g, wcat, bcat, conv_w, conv_b, lam)


def _group_all(v, op):
    blk = 1
    while blk < EXPERTS_PER_GROUP:
        v = op(v, _swap_lanes(v, blk))
        blk *= 2
    return v


def _route_tile(scores, bias):
    neg = jnp.float32(-3.0e38)
    far = jnp.float32(1.0e9)
    lane = lax.broadcasted_iota(jnp.int32, scores.shape, 1)
    lane_f = lane.astype(F32)
    valid = lane < N_EXPERTS
    x = jnp.where(valid, scores + bias, neg)
    m1 = _group_all(x, jnp.maximum)
    i1 = _group_all(jnp.where(x == m1, lane_f, far), jnp.minimum)
    x2 = jnp.where(lane_f == i1, neg, x)
    m2 = _group_all(x2, jnp.maximum)
    i2 = _group_all(jnp.where(x2 == m2, lane_f, far), jnp.minimum)
    gs = jnp.where(valid, m1 + m2, neg)
    gmax = jnp.max(gs, axis=-1, keepdims=True)
    g0 = jnp.min(jnp.where(gs == gmax, lane_f, far), axis=-1, keepdims=True)
    in_best = jnp.where(lane_f >= g0, jnp.where(lane_f < g0 + EXPERTS_PER_GROUP, 1.0, 0.0), 0.0)
    sel1 = jnp.where(lane_f == i1, in_best, 0.0)
    sel2 = jnp.where(lane_f == i2, in_best, 0.0)
    both = sel1 + sel2
    wsum = jnp.sum(both * scores, axis=-1, keepdims=True)
    wmat = both * scores / wsum
    return wmat + pltpu.roll(sel1 + 2.0 * sel2, N_EXPERTS, 1)


def _merge_body(x_ref, mod_ref, g1_ref, g2_ref, ya_ref, yb_ref, yc_ref, yd_ref,
                wm_ref, bm_ref, wb_ref, wo_ref, rwh_ref, rwl_ref, xo_ref, h2_ref, sc_ref):
    d = x_ref.shape[1]
    x = x_ref[...]
    h = (_rms(x, g1_ref[...]) * (1.0 + mod_ref[1:2, :]) + mod_ref[0:1, :]).astype(BF16)
    acc = jnp.zeros(x.shape, F32)
    for n, y_ref in enumerate((ya_ref, yb_ref, yc_ref, yd_ref)):
        gate = jax.nn.sigmoid(jnp.dot(h, wm_ref[:, n * d:(n + 1) * d], preferred_element_type=F32)
                              + bm_ref[:, n * d:(n + 1) * d])
        acc = acc + gate * jnp.dot(y_ref[...], wb_ref[n], preferred_element_type=F32)
    y = jnp.dot(acc.astype(BF16), wo_ref[...], preferred_element_type=F32)
    xn = x + mod_ref[2:3, :] * y
    xo_ref[...] = xn
    h2 = _rms(xn, g2_ref[...]) * (1.0 + mod_ref[4:5, :]) + mod_ref[3:4, :]
    h2_ref[...] = h2
    h2_hi = h2.astype(BF16)
    h2_lo = (h2 - h2_hi.astype(F32)).astype(BF16)
    logits = (jnp.dot(h2_hi, rwh_ref[...], preferred_element_type=F32)
              + jnp.dot(h2_lo, rwh_ref[...], preferred_element_type=F32)
              + jnp.dot(h2_hi, rwl_ref[...], preferred_element_type=F32))
    sc_ref[...] = jax.nn.sigmoid(logits)


def _merge(x_all, mods, g1, g2, ys, wm, bm, wb, wo, rwh, rwl, n_ctx, with_ctx):
    b, s, d = x_all.shape
    tile0 = 0 if with_ctx else n_ctx // TM
    nt = s // TM - tile0
    tile = lambda w: pl.BlockSpec((None, TM, w), lambda bb, i: (bb, i + tile0, 0))
    sub = lambda w: pl.BlockSpec((None, TM, w), lambda bb, i: (bb, i, 0))
    so = nt * TM
    return pl.pallas_call(
        _merge_body,
        grid=(b, nt),
        in_specs=[tile(d),
                  pl.BlockSpec((None, None, 6, d), lambda bb, i: (bb, jnp.minimum(i + tile0, 1), 0, 0)),
                  _const_spec((1, d)), _const_spec((1, d)),
                  sub(BRANCH_W), tile(BRANCH_W), tile(BRANCH_W), sub(BRANCH_W),
                  _const_spec(wm.shape), _const_spec(bm.shape), _const_spec(wb.shape),
                  _const_spec(wo.shape), _const_spec(rwh.shape), _const_spec(rwl.shape)],
        out_specs=[sub(d), sub(d), sub(LANE)],
        out_shape=[jax.ShapeDtypeStruct((b, so, d), F32),
                   jax.ShapeDtypeStruct((b, so, d), F32),
                   jax.ShapeDtypeStruct((b, so, LANE), F32)],
        compiler_params=_cparams(("arbitrary", "arbitrary")),
        name="merge",
    )(x_all, mods, g1, g2, *ys, wm, bm, wb, wo, rwh, rwl)


def _expert_body(be_ref, nu_ref, x_ref, w1_ref, w3_ref, w2_ref, o_ref):
    j = pl.program_id(0)

    @pl.when(j < nu_ref[0])
    def _():
        x = x_ref[...].astype(BF16)
        a = jnp.dot(x, w1_ref[...].astype(BF16), preferred_element_type=F32)
        g = jnp.dot(x, w3_ref[...].astype(BF16), preferred_element_type=F32)
        hmid = (a * jax.nn.sigmoid(a) * g).astype(BF16)
        o_ref[...] = jnp.dot(hmid, w2_ref[...].astype(BF16), preferred_element_type=F32)

    @pl.when(j >= nu_ref[0])
    def _():
        o_ref[...] = jnp.zeros(o_ref.shape, o_ref.dtype)


def _expert_ffn(block_expert, n_used, xb, w1, w3, w2, layer):
    n_rows, d = xb.shape
    nb = n_rows // EXP_ROWS
    ff = w1.shape[-1]
    wspec = lambda r, c: pl.BlockSpec((None, None, r, c), lambda j, be, nu: (layer, be[j], 0, 0))
    grid_spec = pltpu.PrefetchScalarGridSpec(
        num_scalar_prefetch=2,
        grid=(nb,),
        in_specs=[pl.BlockSpec((EXP_ROWS, d), lambda j, be, nu: (jnp.minimum(j, nu[0] - 1), 0)),
                  wspec(d, ff), wspec(d, ff), wspec(ff, d)],
        out_specs=pl.BlockSpec((EXP_ROWS, d), lambda j, be, nu: (j, 0)),
    )
    return pl.pallas_call(
        _expert_body,
        grid_spec=grid_spec,
        out_shape=jax.ShapeDtypeStruct((n_rows, d), F32),
        compiler_params=_cparams(("arbitrary",)),
        name="expert_ffn",
    )(block_expert, n_used, xb, w1, w3, w2)


def _markers(route):
    lane = lax.broadcasted_iota(jnp.int32, route.shape, 1)
    return jnp.where(lane < N_EXPERTS, pltpu.roll(route, N_EXPERTS, 1), 0.0)


ROUTE_ROWS = 1024


def _route_body(pr_ref, rb_ref, rt_ref, cnt_ref):
    @pl.when(pl.program_id(0) == 0)
    def _():
        cnt_ref[...] = jnp.zeros(cnt_ref.shape, F32)

    rt = _route_tile(pr_ref[...], rb_ref[...])
    rt_ref[...] = rt
    cnt_ref[...] += jnp.sum(jnp.where(_markers(rt) > 0.0, 1.0, 0.0), axis=0, keepdims=True)


def _route(probs, rb):
    t = probs.shape[0]
    tile = pl.BlockSpec((ROUTE_ROWS, LANE), lambda i: (i, 0))
    return pl.pallas_call(
        _route_body,
        grid=(t // ROUTE_ROWS,),
        in_specs=[tile, _const_spec(rb.shape)],
        out_specs=[tile, pl.BlockSpec((SUBLANE, LANE), lambda i: (0, 0))],
        out_shape=[jax.ShapeDtypeStruct((t, LANE), F32), jax.ShapeDtypeStruct((SUBLANE, LANE), F32)],
        compiler_params=_cparams(("arbitrary",)),
        name="route",
    )(probs, rb)


def _rank_body(rt_ref, cnt_ref, slot_ref, run_scr, start_scr, tri_scr):
    rows = rt_ref.shape[0]

    @pl.when(pl.program_id(0) == 0)
    def _():
        cnt = cnt_ref[...]
        padded = jnp.floor((cnt + (EXP_ROWS - 1.0)) * (1.0 / EXP_ROWS)) * EXP_ROWS
        lane = lax.broadcasted_iota(jnp.int32, cnt.shape, 1)
        end = padded
        sh = 1
        while sh < LANE:
            end = end + jnp.where(lane >= sh, pltpu.roll(end, sh, 1), 0.0)
            sh *= 2
        start_scr[...] = end - padded
        run_scr[...] = jnp.zeros(run_scr.shape, F32)
        ri = lax.broadcasted_iota(jnp.int32, (rows, rows), 0)
        ci = lax.broadcasted_iota(jnp.int32, (rows, rows), 1)
        tri_scr[...] = jnp.where(ri > ci, 1.0, 0.0).astype(BF16)

    mk = _markers(rt_ref[...])
    p = jnp.where(mk > 0.0, 1.0, 0.0)
    before = jnp.dot(tri_scr[...], p.astype(BF16), preferred_element_type=F32)
    pos = before + run_scr[0:1, :] + start_scr[0:1, :]
    s1 = jnp.sum(jnp.where(mk == 1.0, pos, 0.0), axis=-1, keepdims=True)
    s2 = jnp.sum(jnp.where(mk == 2.0, pos, 0.0), axis=-1, keepdims=True)
    which = lax.broadcasted_iota(jnp.int32, (rows, TOP_K), 1)
    slot_ref[...] = jnp.where(which == 0, s1, s2).astype(jnp.int32)
    run_scr[...] += jnp.sum(p, axis=0, keepdims=True)


def _rank(route, counts):
    t = route.shape[0]
    return pl.pallas_call(
        _rank_body,
        grid=(t // ROUTE_ROWS,),
        in_specs=[pl.BlockSpec((ROUTE_ROWS, LANE), lambda i: (i, 0)), _const_spec((SUBLANE, LANE))],
        out_specs=pl.BlockSpec((ROUTE_ROWS, TOP_K), lambda i: (i, 0)),
        out_shape=jax.ShapeDtypeStruct((t, TOP_K), jnp.int32),
        scratch_shapes=[pltpu.VMEM((SUBLANE, LANE), F32), pltpu.VMEM((SUBLANE, LANE), F32),
                        pltpu.VMEM((ROUTE_ROWS, ROUTE_ROWS), BF16)],
        compiler_params=_cparams(("arbitrary",)),
        name="rank",
    )(route, counts)


def _block_experts(counts, n_blocks):
    cnt = counts[0, :N_EXPERTS].astype(jnp.int32)
    padded = (cnt + EXP_ROWS - 1) // EXP_ROWS * EXP_ROWS
    pad_end = jnp.cumsum(padded)
    first_row = jnp.arange(n_blocks, dtype=jnp.int32) * EXP_ROWS
    ended = jnp.sum((pad_end[None, :] <= first_row[:, None]).astype(jnp.int32), axis=1)
    block_expert = jnp.minimum(ended, N_EXPERTS - 1).astype(jnp.int32)
    n_used = (pad_end[-1] // EXP_ROWS).astype(jnp.int32).reshape(1)
    return block_expert, n_used, cnt, pad_end.astype(jnp.int32)


def _row_copy(src, src_row, dst, dst_row, sem):
    return pltpu.make_async_copy(src.at[pl.ds(src_row, 1)], dst.at[pl.ds(dst_row, 1)], sem)


def _dispatch_body(slot_ref, cnt_ref, end_ref, nu_ref, h_ref, xb_ref, zero_scr, sem, zsem):
    tm = h_ref.shape[0]

    @pl.when(pl.program_id(0) == 0)
    def _():
        zero_scr[...] = jnp.zeros(zero_scr.shape, zero_scr.dtype)

        def fill(e, carry):
            @pl.when(cnt_ref[e] > 0)
            def _():
                r0 = pl.multiple_of(end_ref[e] - EXP_ROWS, EXP_ROWS)
                pltpu.make_async_copy(zero_scr, xb_ref.at[pl.ds(r0, EXP_ROWS)], zsem).start()
            return carry

        def drain(e, carry):
            @pl.when(cnt_ref[e] > 0)
            def _():
                pltpu.make_async_copy(zero_scr, xb_ref.at[pl.ds(0, EXP_ROWS)], zsem).wait()
            return carry

        def fill_tail(j, carry):
            r0 = pl.multiple_of(j * EXP_ROWS, EXP_ROWS)
            pltpu.make_async_copy(zero_scr, xb_ref.at[pl.ds(r0, EXP_ROWS)], zsem).start()
            return carry

        def drain_tail(j, carry):
            pltpu.make_async_copy(zero_scr, xb_ref.at[pl.ds(0, EXP_ROWS)], zsem).wait()
            return carry

        n_blocks = xb_ref.shape[0] // EXP_ROWS
        lax.fori_loop(0, N_EXPERTS, fill, 0)
        lax.fori_loop(nu_ref[0], n_blocks, fill_tail, 0)
        lax.fori_loop(0, N_EXPERTS, drain, 0)
        lax.fori_loop(nu_ref[0], n_blocks, drain_tail, 0)

    def issue(t, carry):
        for k in range(TOP_K):
            _row_copy(h_ref, t, xb_ref, slot_ref[0, TOP_K * t + k], sem).start(priority=k % 2)
        return carry

    lax.fori_loop(0, tm, issue, 0, unroll=8)
    for k in range(TOP_K):
        pltpu.make_async_copy(h_ref, xb_ref.at[pl.ds(0, tm)], sem).wait()


def _dispatch(slot_tiles, cnt, pad_end, n_used, h2, n_rows):
    b, s, dh = h2.shape
    nt = b * s // TM
    smem = pl.BlockSpec(memory_space=pltpu.SMEM)
    return pl.pallas_call(
        _dispatch_body,
        grid=(nt,),
        in_specs=[pl.BlockSpec((None, 1, TOP_K * TM), lambda i: (i, 0, 0), memory_space=pltpu.SMEM),
                  smem, smem, smem,
                  pl.BlockSpec((TM, dh), lambda i: (i, 0))],
        out_specs=pl.BlockSpec(memory_space=pl.ANY),
        out_shape=jax.ShapeDtypeStruct((n_rows, dh), h2.dtype),
        scratch_shapes=[pltpu.VMEM((EXP_ROWS, dh), h2.dtype),
                        pltpu.SemaphoreType.DMA(()), pltpu.SemaphoreType.DMA(())],
        compiler_params=_cparams(("arbitrary",)),
        name="dispatch",
    )(slot_tiles, cnt, pad_end, n_used, h2.reshape(b * s, dh))


def _resid_body(slot_ref, x_ref, mod_ref, rt_ref, gf_ref, y_hbm, o_ref, ybuf0, ybuf1, sem, *, final):
    tm = x_ref.shape[0]
    ybuf = (ybuf0, ybuf1)

    def issue(t, carry):
        for k in range(TOP_K):
            _row_copy(y_hbm, slot_ref[0, TOP_K * t + k], ybuf[k], t, sem).start(priority=k % 2)
        return carry

    lax.fori_loop(0, tm, issue, 0, unroll=8)
    rt = rt_ref[...]
    mk = _markers(rt)
    w1 = jnp.sum(jnp.where(mk == 1.0, rt, 0.0), axis=-1, keepdims=True)
    w2 = jnp.sum(jnp.where(mk == 2.0, rt, 0.0), axis=-1, keepdims=True)
    for k in range(TOP_K):
        pltpu.make_async_copy(y_hbm.at[pl.ds(0, tm)], ybuf[k], sem).wait()
    f = w1 * ybuf0[...] + w2 * ybuf1[...]
    xn = x_ref[...] + mod_ref[5:6, :] * f
    if final:
        xn = _rms(xn, gf_ref[...])
    o_ref[...] = xn


def _moe_residual(slot_tiles, x_all, mods, route, gf, yblk, has_ctx, final):
    b, s, d = x_all.shape
    nt = s // TM
    tin = lambda ww: pl.BlockSpec((None, TM, ww), lambda bb, i: (bb, i, 0))
    mod_row = (lambda i: jnp.minimum(i, 1)) if has_ctx else (lambda i: 1)
    return pl.pallas_call(
        functools.partial(_resid_body, final=final),
        grid=(b, nt),
        in_specs=[pl.BlockSpec((None, 1, TOP_K * TM), lambda bb, i: (bb * nt + i, 0, 0),
                               memory_space=pltpu.SMEM),
                  tin(d),
                  pl.BlockSpec((None, None, 6, d), lambda bb, i: (bb, mod_row(i), 0, 0)),
                  tin(LANE), _const_spec((1, d)),
                  pl.BlockSpec(memory_space=pl.ANY)],
        out_specs=tin(d),
        out_shape=jax.ShapeDtypeStruct((b, s, d), F32),
        scratch_shapes=[pltpu.VMEM((TM, d), F32), pltpu.VMEM((TM, d), F32), pltpu.SemaphoreType.DMA(())],
        compiler_params=_cparams(("arbitrary", "arbitrary")),
        name="moe_residual",
    )(slot_tiles, x_all, mods, route, gf, yblk)


def kernel(x, c, ctx, c_ctx, ada_w, ada_b, norm1_g, norm2_g, w_in, mla_q_norm, mla_kv_norm, mla_w_uq,
           mla_w_ukv, ret_decay, ret_norm, lru_conv_w, lru_conv_b, lru_w_a, lru_b_a, lru_w_x, lru_b_x,
           lru_lambda, gqa_q_norm, gqa_k_norm, w_branch, w_merge, b_merge, w_out, router_w, router_bias,
           moe_w1, moe_w3, moe_w2, final_norm):
    b, seq, d = x.shape
    n_ctx = ctx.shape[1]
    depth = ada_w.shape[0]
    s = n_ctx + seq
    assert n_ctx == TM and seq % TM == 0 and seq % GRID_W == 0

    r_pad = -(-(b + 1) // SUBLANE) * SUBLANE
    cc = jnp.zeros((r_pad, d), F32).at[:b].set(c).at[b].set(c_ctx)
    mods_all = _ada_mods(cc, ada_w, ada_b)

    tabs = jnp.asarray(np.stack(_rope_slot_tables(n_ctx, seq, MLA_ROPE, MLA_ROT_LANES)
                                + _rope_slot_tables(n_ctx, seq, GQA_HEAD_DIM, HEAD_LANES)))

    in_cols = _in_proj_columns()
    uq_cols, ukv_cols = _mla_up_columns()
    rw = jnp.concatenate([router_w.astype(F32), jnp.zeros((d, LANE - N_EXPERTS), F32)], axis=1)
    rw_hi = rw.astype(BF16)
    rw_lo = (rw - rw_hi.astype(F32)).astype(BF16)
    rb = jnp.concatenate([router_bias.astype(F32), jnp.zeros((LANE - N_EXPERTS,), F32)])[None]

    x_all = jnp.concatenate([ctx, x], axis=1)
    out = None
    for l in range(depth):
        last = l == depth - 1
        m = mods_all[l].reshape(r_pad, 6, d)
        mods = jnp.stack([jnp.broadcast_to(m[b], (b, 6, d)), m[:b]], axis=1)

        head_gains = jnp.zeros((2, LANE), F32).at[:, HEAD_LANES].set(
            jnp.stack([gqa_q_norm[l], gqa_k_norm[l]]).astype(F32))

        win_p = _take_cols(w_in[l], in_cols).astype(BF16)
        wuq_p = _take_cols(mla_w_uq[l], uq_cols).astype(BF16)
        wukv_p = _take_cols(mla_w_ukv[l], ukv_cols).astype(BF16)

        (mq, mk, mv, rq, rk, rv, rg, lu, lg, gq, gk, gv) = _inproj(
            x_all, mods, norm1_g[l][None], tabs, win_p, wuq_p, wukv_p,
            mla_q_norm[l][None], mla_kv_norm[l][None], head_gains)

        ya = _attention(mq, mk, mv, MLA_HEADS, MLA_HEADS, n_ctx, not last)
        yd = _attention(gq, gk, gv, GQA_HEADS, GQA_KV_HEADS, n_ctx, not last)

        log_g = -jax.nn.softplus(-ret_decay[l].astype(F32))
        yb = _retention(log_g, rq, rk, rv, rg, ret_norm[l][None], n_ctx)

        eye = jnp.eye(LRU_BLOCKS, dtype=F32)

        def block_diag(wblk):
            return jnp.einsum('ncd,nm->ncmd', wblk, eye).reshape(LRU_WIDTH, LRU_WIDTH)

        wcat = jnp.concatenate([block_diag(lru_w_a[l, 0]), block_diag(lru_w_x[l, 0]),
                                block_diag(lru_w_a[l, 1]), block_diag(lru_w_x[l, 1])], axis=1).astype(BF16)
        bcat = jnp.concatenate([lru_b_a[l, 0], lru_b_x[l, 0], lru_b_a[l, 1], lru_b_x[l, 1]])[None]
        yc = _rglru(lu, lg, wcat, bcat, lru_conv_w[l], lru_conv_b[l][None], lru_lambda[l], n_ctx)

        x_all, h2, probs = _merge(
            x_all, mods, norm1_g[l][None], norm2_g[l][None], (ya, yb, yc, yd),
            w_merge[l].astype(BF16), b_merge[l][None], w_branch[l].astype(BF16), w_out[l].astype(BF16),
            rw_hi, rw_lo, n_ctx, not last)

        t = b * (seq if last else s)
        assert t % ROUTE_ROWS == 0
        n_blocks = -(-t * TOP_K // EXP_ROWS) + N_EXPERTS
        route, counts = _route(probs.reshape(t, LANE), rb)
        slot_tiles = _rank(route, counts).reshape(t // TM, 1, TOP_K * TM)
        route = route.reshape(b, t // b, LANE)
        block_expert, n_used, cnt, pad_end = _block_experts(counts, n_blocks)
        xb = _dispatch(slot_tiles, cnt, pad_end, n_used, h2, n_blocks * EXP_ROWS)
        yblk = _expert_ffn(block_expert, n_used, xb, moe_w1, moe_w3, moe_w2, l)
        out = _moe_residual(slot_tiles, x_all, mods, route, final_norm[None], yblk, not last, last)
        x_all = out
    return out
```

```python
import functools

import numpy as np
import jax
import jax.numpy as jnp
from jax import lax
from jax.experimental import pallas as pl
from jax.experimental.pallas import tpu as pltpu

F32 = jnp.float32
BF16 = jnp.bfloat16

LANE = 128
SUBLANE = 8
VMEM_LIMIT = 56 * 1024 * 1024

GRID_W = 64
ROPE_BASE = 10000.0
EPS = 1e-6
MLA_HEADS, MLA_NOPE, MLA_ROPE, MLA_V = 8, 64, 32, 64
MLA_Q_LORA, MLA_KV_LORA = 256, 128
RET_HEADS, RET_DK, RET_DV, RET_CHUNK = 4, 64, 128, 128
LRU_WIDTH, LRU_BLOCKS, LRU_C, CONV_W, CONV_PAD_LEFT = 512, 8, 8.0, 4, 2
LRU_BLOCK_W = LRU_WIDTH // LRU_BLOCKS
GQA_HEADS, GQA_KV_HEADS, GQA_HEAD_DIM = 8, 2, 64
N_BRANCH, BRANCH_W = 4, 512
N_EXPERTS, N_GROUPS, TOP_K, EXPERT_FF, MOE_BLOCK = 64, 8, 2, 256, 128
EXPERTS_PER_GROUP = N_EXPERTS // N_GROUPS

IN_SPLITS = (MLA_Q_LORA, MLA_KV_LORA, MLA_ROPE,
             RET_HEADS * RET_DK, RET_HEADS * RET_DK, RET_HEADS * RET_DV, RET_HEADS * RET_DV,
             LRU_WIDTH, LRU_WIDTH,
             GQA_HEADS * GQA_HEAD_DIM, GQA_KV_HEADS * GQA_HEAD_DIM, GQA_KV_HEADS * GQA_HEAD_DIM)
IN_OFF = tuple(int(o) for o in np.cumsum((0,) + IN_SPLITS))
D_IN = IN_OFF[-1]

TM = 256
EXP_ROWS = 256
HEAD_SLOT = LANE
ONE_LANE = 64

ZP_CQ, ZP_CKV, ZP_KR = 0, 256, 384
ZP_RQ, ZP_RK, ZP_RV, ZP_RG = 512, 1024, 1536, 2048
ZP_LU, ZP_LG = 2560, 3072
ZP_GQ, ZP_GK, ZP_GV = 3584, 4608, 4864
ZP_W = 5120


def _cparams(sem):
    return pltpu.CompilerParams(dimension_semantics=sem, vmem_limit_bytes=VMEM_LIMIT)


def _const_spec(shape):
    nd = len(shape)
    return pl.BlockSpec(shape, lambda *_: (0,) * nd, pipeline_mode=pl.Buffered(1))


PARTNER = LANE // 2


def _rot_lanes(rot_dim):
    q = rot_dim // 4
    e = np.arange(rot_dim)
    blk, o = e // q, e % q
    return np.where(blk == 0, o, np.where(blk == 1, PARTNER + o, np.where(blk == 2, q + o, PARTNER + q + o)))


HEAD_LANES = _rot_lanes(GQA_HEAD_DIM)
MLA_ROT_LANES = _rot_lanes(MLA_ROPE)
MLA_NOPE_LANES = np.array([l for l in range(MLA_NOPE + MLA_ROPE)
                           if l not in set(MLA_ROT_LANES.tolist())])


def _in_proj_columns():
    idx = np.full((ZP_W,), D_IN, np.int64)
    o = IN_OFF
    idx[ZP_CQ:ZP_CQ + 256] = o[0] + np.arange(256)
    idx[ZP_CKV:ZP_CKV + 128] = o[1] + np.arange(128)
    idx[ZP_KR + MLA_ROT_LANES] = o[2] + np.arange(MLA_ROPE)
    for h in range(RET_HEADS):
        idx[ZP_RQ + h * 128 + HEAD_LANES] = o[3] + h * 64 + np.arange(64)
        idx[ZP_RK + h * 128 + HEAD_LANES] = o[4] + h * 64 + np.arange(64)
    idx[ZP_RV:ZP_RV + 512] = o[5] + np.arange(512)
    idx[ZP_RG:ZP_RG + 512] = o[6] + np.arange(512)
    idx[ZP_LU:ZP_LU + 512] = o[7] + np.arange(512)
    idx[ZP_LG:ZP_LG + 512] = o[8] + np.arange(512)
    for h in range(GQA_HEADS):
        idx[ZP_GQ + h * 128 + HEAD_LANES] = o[9] + h * 64 + np.arange(64)
    for h in range(GQA_KV_HEADS):
        idx[ZP_GK + h * 128 + HEAD_LANES] = o[10] + h * 64 + np.arange(64)
        idx[ZP_GV + h * 128:ZP_GV + h * 128 + 64] = o[11] + h * 64 + np.arange(64)
    return idx


def _mla_up_columns():
    dq = MLA_NOPE + MLA_ROPE
    dkv = MLA_NOPE + MLA_V
    qi = np.full((MLA_HEADS * 128,), MLA_HEADS * dq, np.int64)
    ki = np.full((MLA_HEADS * 128,), MLA_HEADS * dkv, np.int64)
    vi = np.full((MLA_HEADS * 128,), MLA_HEADS * dkv, np.int64)
    for h in range(MLA_HEADS):
        qi[h * 128 + MLA_NOPE_LANES] = h * dq + np.arange(MLA_NOPE)
        qi[h * 128 + MLA_ROT_LANES] = h * dq + MLA_NOPE + np.arange(MLA_ROPE)
        ki[h * 128 + MLA_NOPE_LANES] = h * dkv + np.arange(MLA_NOPE)
        vi[h * 128:h * 128 + MLA_V] = h * dkv + MLA_NOPE + np.arange(MLA_V)
    return qi, np.concatenate([ki, vi])


def _take_cols(w, idx):
    wz = jnp.concatenate([w, jnp.zeros((w.shape[0], 1), w.dtype)], axis=1)
    return jnp.take(wz, jnp.asarray(idx, jnp.int32), axis=1)


def _rope_slot_tables(n_ctx, seq, rot_dim, lanes):
    half = rot_dim // 2
    pos = np.arange(seq)
    inv = ROPE_BASE ** (-np.arange(0, half, 2, dtype=np.float64) / half)
    ar = (pos // GRID_W)[:, None] * inv
    ac = (pos % GRID_W)[:, None] * inv
    cos = np.concatenate([np.cos(ar), np.cos(ar), np.cos(ac), np.cos(ac)], axis=1)
    sin = np.concatenate([-np.sin(ar), np.sin(ar), -np.sin(ac), np.sin(ac)], axis=1)
    cos_t = np.ones((n_ctx + seq, LANE), np.float32)
    sin_t = np.zeros((n_ctx + seq, LANE), np.float32)
    cos_t[n_ctx:, lanes] = cos
    sin_t[n_ctx:, lanes] = sin
    return cos_t, sin_t


def _swap_lanes(x, blk):
    n = x.shape[-1]
    lane = lax.broadcasted_iota(jnp.int32, x.shape, x.ndim - 1)
    up = pltpu.roll(x, n - blk, x.ndim - 1)
    dn = pltpu.roll(x, blk, x.ndim - 1)
    return jnp.where((lane % (2 * blk)) < blk, up, dn)


def _rms(x, g):
    return x * lax.rsqrt(jnp.mean(x * x, axis=-1, keepdims=True) + EPS) * g


def _ada_body(c_ref, w_ref, b_ref, o_ref):
    c = c_ref[...]
    s = (c * jax.nn.sigmoid(c)).astype(BF16)
    o_ref[...] = jnp.dot(s, w_ref[...].astype(BF16), preferred_element_type=F32) + b_ref[...]


def _ada_mods(cc, ada_w, ada_b):
    depth, d, n = ada_w.shape
    r = cc.shape[0]
    tn = 1536
    return pl.pallas_call(
        _ada_body,
        grid=(depth, n // tn),
        in_specs=[pl.BlockSpec((r, d), lambda l, j: (0, 0)),
                  pl.BlockSpec((None, d, tn), lambda l, j: (l, 0, j)),
                  pl.BlockSpec((None, 1, tn), lambda l, j: (l, 0, j))],
        out_specs=pl.BlockSpec((None, r, tn), lambda l, j: (l, 0, j)),
        out_shape=jax.ShapeDtypeStruct((depth, r, n), F32),
        compiler_params=_cparams(("arbitrary", "arbitrary")),
        name="ada_mods",
    )(cc, ada_w, ada_b.reshape(depth, 1, n))


def _inproj_body(x_ref, mod_ref, g1_ref, tab_ref, win_ref, wuq_ref, wukv_ref, gq_ref, gkv_ref, hg_ref,
                 mq_ref, mk_ref, mv_ref, rq_ref, rk_ref, rv_ref, rg_ref, lu_ref, lg_ref,
                 gq_out, gk_out, gv_out):
    x = x_ref[...]
    shift = mod_ref[0:1, :]
    scale = mod_ref[1:2, :]
    h = _rms(x, g1_ref[...]) * (1.0 + scale) + shift
    hb = h.astype(BF16)

    def proj(c0, c1):
        return jnp.dot(hb, win_ref[:, c0:c1], preferred_element_type=F32)

    lane = lax.broadcasted_iota(jnp.int32, (x.shape[0], LANE), 1)
    one_col = jnp.where(lane == ONE_LANE, 1.0, 0.0).astype(F32)

    def rope(v, ci):
        return v * tab_ref[ci] + pltpu.roll(v, PARTNER, 1) * tab_ref[ci + 1]

    mla_scale = (MLA_NOPE + MLA_ROPE) ** -0.5
    gqa_scale = GQA_HEAD_DIM ** -0.5

    cq = proj(ZP_CQ, ZP_CQ + 256)
    qn = _rms(cq, gq_ref[...]).astype(BF16)
    q = jnp.dot(qn, wuq_ref[...], preferred_element_type=F32)
    for hh in range(MLA_HEADS):
        sl = slice(hh * 128, (hh + 1) * 128)
        mq_ref[:, sl] = rope(q[:, sl] * mla_scale, 0).astype(BF16)
    ckv = proj(ZP_CKV, ZP_CKV + 128)
    kvn = _rms(ckv, gkv_ref[...]).astype(BF16)
    kv = jnp.dot(kvn, wukv_ref[...], preferred_element_type=F32)
    kr = rope(proj(ZP_KR, ZP_KR + 128), 0)
    for hh in range(MLA_HEADS):
        sl = slice(hh * 128, (hh + 1) * 128)
        mk_ref[sl, :] = (kv[:, sl] + kr).T.astype(BF16)
        mv_ref[:, sl] = (kv[:, MLA_HEADS * 128 + hh * 128:MLA_HEADS * 128 + (hh + 1) * 128] + one_col).astype(BF16)

    rq = proj(ZP_RQ, ZP_RQ + 512)
    rk = proj(ZP_RK, ZP_RK + 512)
    for hh in range(RET_HEADS):
        sl = slice(hh * 128, (hh + 1) * 128)
        rq_ref[:, sl] = rope(rq[:, sl], 2).astype(BF16)
        rk_ref[:, sl] = rope(rk[:, sl] * RET_DK ** -0.5, 2).astype(BF16)
    rv_ref[...] = proj(ZP_RV, ZP_RV + 512).astype(BF16)
    rg_ref[...] = proj(ZP_RG, ZP_RG + 512).astype(BF16)

    lu_ref[...] = proj(ZP_LU, ZP_LU + 512).astype(BF16)
    lg_ref[...] = proj(ZP_LG, ZP_LG + 512).astype(BF16)

    gq = proj(ZP_GQ, ZP_GQ + 1024)
    for hh in range(GQA_HEADS):
        sl = slice(hh * 128, (hh + 1) * 128)
        v = gq[:, sl]
        v = v * lax.rsqrt(jnp.sum(v * v, axis=-1, keepdims=True) * (1.0 / GQA_HEAD_DIM) + EPS)
        gq_out[:, sl] = rope(v * (hg_ref[0:1, :] * gqa_scale), 2).astype(BF16)
    gk = proj(ZP_GK, ZP_GK + 256)
    gv = proj(ZP_GV, ZP_GV + 256)
    for hh in range(GQA_KV_HEADS):
        sl = slice(hh * 128, (hh + 1) * 128)
        v = gk[:, sl]
        v = v * lax.rsqrt(jnp.sum(v * v, axis=-1, keepdims=True) * (1.0 / GQA_HEAD_DIM) + EPS)
        gk_out[sl, :] = rope(v * hg_ref[1:2, :], 2).T.astype(BF16)
        gv_out[:, sl] = (gv[:, sl] + one_col).astype(BF16)


def _inproj(x_all, mods, g1, tabs, win_p, wuq_p, wukv_p, gq, gkv, head_gains):
    b, s, d = x_all.shape
    nt = s // TM
    widths = (1024, 1024, 1024, 512, 512, 512, 512, 512, 512, 1024, 256, 256)
    tile = lambda w: pl.BlockSpec((None, TM, w), lambda i, bb: (bb, i, 0))
    ttile = lambda w: pl.BlockSpec((None, w, TM), lambda i, bb: (bb, 0, i))
    transposed = (1, 10)
    return pl.pallas_call(
        _inproj_body,
        grid=(nt, b),
        in_specs=[tile(d),
                  pl.BlockSpec((None, None, 6, d), lambda i, bb: (bb, jnp.minimum(i, 1), 0, 0)),
                  _const_spec((1, d)),
                  pl.BlockSpec((4, TM, LANE), lambda i, bb: (0, i, 0)),
                  _const_spec(win_p.shape), _const_spec(wuq_p.shape), _const_spec(wukv_p.shape),
                  _const_spec((1, MLA_Q_LORA)), _const_spec((1, MLA_KV_LORA)), _const_spec((2, LANE))],
        out_specs=[ttile(w) if j in transposed else tile(w) for j, w in enumerate(widths)],
        out_shape=[jax.ShapeDtypeStruct((b, w, s) if j in transposed else (b, s, w), BF16)
                   for j, w in enumerate(widths)],
        compiler_params=_cparams(("arbitrary", "arbitrary")),
        name="in_proj",
    )(x_all, mods, g1, tabs, win_p, wuq_p, wukv_p, gq, gkv, head_gains)


ATT_TILES = 2


def _attn_body(*refs, heads, kv_heads, n_q):
    q_refs, (kt_ref, v_ref, o_ref) = refs[:n_q], refs[n_q:]
    grp = heads // kv_heads
    for hp in range(heads // 2):
        outs = []
        for h in (2 * hp, 2 * hp + 1):
            g = h // grp
            sl = slice(h * 128, (h + 1) * 128)
            q = q_refs[0][:, sl] if n_q == 1 else jnp.concatenate([r[:, sl] for r in q_refs], axis=0)
            kt = kt_ref[g * 128:(g + 1) * 128, :]
            v = v_ref[:, g * 128:(g + 1) * 128]
            s = jnp.dot(q, kt, preferred_element_type=F32).astype(BF16)
            m = jnp.max(s, axis=-1, keepdims=True)
            p = jnp.exp(s - m)
            o = jnp.dot(p, v, preferred_element_type=F32)
            outs.append(o[:, :64] / o[:, ONE_LANE:ONE_LANE + 1])
        o_ref[:, hp * 128:(hp + 1) * 128] = jnp.concatenate(outs, axis=1).astype(o_ref.dtype)


def _attention_lat(q, kt, v, heads, kv_heads, n_ctx):
    b, s, _ = q.shape
    off = n_ctx // TM
    rows = ATT_TILES * TM
    nq = (s - n_ctx) // rows
    qspec = lambda r: pl.BlockSpec((None, TM, heads * 128), lambda bb, j: (bb, off + ATT_TILES * j + r, 0))
    return pl.pallas_call(
        functools.partial(_attn_body, heads=heads, kv_heads=kv_heads, n_q=ATT_TILES),
        grid=(b, nq),
        in_specs=[qspec(r) for r in range(ATT_TILES)] + [
            pl.BlockSpec((None, kv_heads * 128, s), lambda bb, j: (bb, 0, 0)),
            pl.BlockSpec((None, s, kv_heads * 128), lambda bb, j: (bb, 0, 0))],
        out_specs=pl.BlockSpec((None, rows, heads * 64), lambda bb, j: (bb, j, 0)),
        out_shape=jax.ShapeDtypeStruct((b, nq * rows, heads * 64), BF16),
        compiler_params=_cparams(("arbitrary", "arbitrary")),
        name="attention_lat_h%d_kv%d" % (heads, kv_heads),
    )(*([q] * ATT_TILES), kt, v)


def _attention_ctx(q, kt, v, heads, kv_heads, n_ctx):
    b = q.shape[0]
    return pl.pallas_call(
        functools.partial(_attn_body, heads=heads, kv_heads=kv_heads, n_q=1),
        grid=(b,),
        in_specs=[pl.BlockSpec((None, n_ctx, heads * 128), lambda bb: (bb, 0, 0)),
                  pl.BlockSpec((None, kv_heads * 128, n_ctx), lambda bb: (bb, 0, 0)),
                  pl.BlockSpec((None, n_ctx, kv_heads * 128), lambda bb: (bb, 0, 0))],
        out_specs=pl.BlockSpec((None, n_ctx, heads * 64), lambda bb: (bb, 0, 0)),
        out_shape=jax.ShapeDtypeStruct((b, n_ctx, heads * 64), BF16),
        compiler_params=_cparams(("arbitrary",)),
        name="attention_ctx_h%d_kv%d" % (heads, kv_heads),
    )(q, kt, v)


def _attention(q, kt, v, heads, kv_heads, n_ctx, with_ctx):
    y = _attention_lat(q, kt, v, heads, kv_heads, n_ctx)
    if with_ctx:
        y = jnp.concatenate([_attention_ctx(q, kt, v, heads, kv_heads, n_ctx), y], axis=1)
    return y


RET_ROWS = 256


def _retention_body(lg_ref, q_ref, k_ref, v_ref, g_ref, gn_ref, o_ref,
                    kv_scr, st_scr, dec_scr, m_scr, *, n_ctx):
    c = RET_ROWS
    s = q_ref.shape[0]
    nc = s // c
    nctx = n_ctx // c
    back_order = list(range(nctx - 1, -1, -1)) + list(range(nc - 1, nctx - 1, -1))
    pos = lax.broadcasted_iota(jnp.int32, (c, LANE), 0).astype(F32)
    ri = lax.broadcasted_iota(jnp.int32, (c, c), 0)
    ci = lax.broadcasted_iota(jnp.int32, (c, c), 1)
    diff = (ri - ci).astype(F32)
    heads = [(h, slice(h * 128, (h + 1) * 128)) for h in range(RET_HEADS)]

    for h, _ in heads:
        lgf = lg_ref[0, h]
        lgb = lg_ref[1, h]
        dec_scr[h, 0] = jnp.exp(lgf * (c - 1.0 - pos))
        dec_scr[h, 1] = jnp.exp(lgb * pos)
        dec_scr[h, 2] = jnp.exp(lgf * (pos + 1.0))
        dec_scr[h, 3] = jnp.exp(lgb * (c - pos))
        m_scr[h] = jnp.where(diff >= 0, jnp.exp(lgf * jnp.maximum(diff, 0.0)),
                             jnp.exp(lgb * jnp.maximum(-diff, 0.0)))

    def kv_step(j, carry):
        r0 = pl.multiple_of(j * c, c)
        for h, sl in heads:
            kc = k_ref[pl.ds(r0, c), sl].astype(F32)
            vc = v_ref[pl.ds(r0, c), sl]
            kk = jnp.concatenate([(kc * dec_scr[h, 0]).astype(BF16), (kc * dec_scr[h, 1]).astype(BF16)], axis=1)
            kv_scr[h, j] = lax.dot_general(kk, vc, (((0,), (0,)), ((), ())), preferred_element_type=F32)
        return carry

    lax.fori_loop(0, nc, kv_step, 0, unroll=3)

    for h, _ in heads:
        gcf = jnp.exp(lg_ref[0, h] * c)
        gcb = jnp.exp(lg_ref[1, h] * c)
        sf = jnp.zeros((128, 128), F32)
        for j in range(nc):
            st_scr[h, j, 0:128, :] = sf.astype(BF16)
            sf = sf * gcf + kv_scr[h, j, 0:128, :]
        sb = jnp.zeros((128, 128), F32)
        for j in back_order:
            st_scr[h, j, 128:256, :] = sb.astype(BF16)
            sb = sb * gcb + kv_scr[h, j, 128:256, :]

    def out_step(j, carry):
        r0 = pl.multiple_of(j * c, c)
        for h, sl in heads:
            qb = q_ref[pl.ds(r0, c), sl]
            kb = k_ref[pl.ds(r0, c), sl]
            vc = v_ref[pl.ds(r0, c), sl]
            sc = lax.dot_general(qb, kb, (((1,), (1,)), ((), ())), preferred_element_type=F32) * m_scr[h]
            o = jnp.dot(sc.astype(BF16), vc, preferred_element_type=F32)
            qf = qb.astype(F32)
            qd = jnp.concatenate([(qf * dec_scr[h, 2]).astype(BF16), (qf * dec_scr[h, 3]).astype(BF16)], axis=1)
            o = o + jnp.dot(qd, st_scr[h, j], preferred_element_type=F32)
            mu = jnp.mean(o, axis=-1, keepdims=True)
            oc = o - mu
            var = jnp.mean(oc * oc, axis=-1, keepdims=True)
            y = oc * lax.rsqrt(var + EPS) * gn_ref[:, sl]
            gate = g_ref[pl.ds(r0, c), sl].astype(F32)
            o_ref[pl.ds(r0, c), sl] = (gate * jax.nn.sigmoid(gate) * y).astype(o_ref.dtype)
        return carry

    lax.fori_loop(0, nc, out_step, 0, unroll=3)


def _retention(log_g, q, k, v, g, gn, n_ctx):
    b, s, w = v.shape
    nc = s // RET_ROWS
    blk = lambda ww: pl.BlockSpec((None, s, ww), lambda bb: (bb, 0, 0))
    return pl.pallas_call(
        functools.partial(_retention_body, n_ctx=n_ctx),
        grid=(b,),
        in_specs=[pl.BlockSpec(memory_space=pltpu.SMEM),
                  blk(512), blk(512), blk(512), blk(512), _const_spec((1, 512))],
        out_specs=blk(512),
        out_shape=jax.ShapeDtypeStruct((b, s, 512), BF16),
        scratch_shapes=[pltpu.VMEM((RET_HEADS, nc, 256, 128), F32),
                        pltpu.VMEM((RET_HEADS, nc, 256, 128), BF16),
                        pltpu.VMEM((RET_HEADS, 4, RET_ROWS, LANE), F32),
                        pltpu.VMEM((RET_HEADS, RET_ROWS, RET_ROWS), F32)],
        compiler_params=_cparams(("arbitrary",)),
        name="retention",
    )(log_g, q, k, v, g, gn)


LRU_ROWS = 256
LRU_HALO = 16


def _tile_scan(a, bv, carry, reverse):
    row = lax.broadcasted_iota(jnp.int32, a.shape, 0)
    for dlt in (1, 2, 4):
        if reverse:
            a_s = pltpu.roll(a, SUBLANE - dlt, 0)
            b_s = pltpu.roll(bv, SUBLANE - dlt, 0)
            ok = row < SUBLANE - dlt
        else:
            a_s = pltpu.roll(a, dlt, 0)
            b_s = pltpu.roll(bv, dlt, 0)
            ok = row >= dlt
        bv = jnp.where(ok, a * b_s + bv, bv)
        a = jnp.where(ok, a * a_s, a)
    h = a * carry + bv
    new_carry = h[0:1, :] if reverse else h[SUBLANE - 1:SUBLANE, :]
    return h, new_carry


def _rglru_body(u_ref, g_ref, wc_ref, bc_ref, cw_ref, cb_ref, lam_ref, o_ref, a_scr, b_scr, *, n_ctx):
    s = u_ref.shape[0]
    w = LRU_WIDTH
    r = LRU_ROWS
    cdec = [-LRU_C * jax.nn.softplus(-lam_ref[d:d + 1, :]) for d in range(2)]
    zeros = jnp.zeros((LRU_HALO, w), F32)

    for ch in range(s // r):
        r0 = ch * r
        seg_start = r0 == 0 or r0 == n_ctx
        seg_end = r0 + r == n_ctx or r0 + r == s
        lo = r0 if seg_start else r0 - LRU_HALO
        hi = r0 + r if seg_end else r0 + r + LRU_HALO
        parts = [u_ref[lo:hi, :].astype(F32)]
        if seg_start:
            parts = [zeros] + parts
        if seg_end:
            parts = parts + [zeros]
        ext = jnp.concatenate(parts, axis=0) if len(parts) > 1 else parts[0]
        n = ext.shape[0]
        u = cb_ref[...]
        for j in range(CONV_W):
            sh = (CONV_PAD_LEFT - j) % n
            tap = ext if sh == 0 else pltpu.roll(ext, sh, 0)
            u = u + tap[LRU_HALO:LRU_HALO + r, :] * cw_ref[j:j + 1, :]
        gates = jnp.dot(u.astype(BF16), wc_ref[...], preferred_element_type=F32) + bc_ref[...]
        for d in range(2):
            rg = jax.nn.sigmoid(gates[:, (2 * d) * w:(2 * d + 1) * w])
            ig = jax.nn.sigmoid(gates[:, (2 * d + 1) * w:(2 * d + 2) * w])
            a = jnp.exp(rg * cdec[d])
            a_scr[d, r0:r0 + r, :] = a
            b_scr[d, r0:r0 + r, :] = jnp.sqrt(1.0 - a * a) * (ig * u)

    def fwd(t, carry):
        r0 = pl.multiple_of(t * SUBLANE, SUBLANE)
        h, carry = _tile_scan(a_scr[0, pl.ds(r0, SUBLANE), :], b_scr[0, pl.ds(r0, SUBLANE), :], carry, False)
        b_scr[0, pl.ds(r0, SUBLANE), :] = h
        return carry

    lax.fori_loop(0, s // SUBLANE, fwd, jnp.zeros((1, w), F32), unroll=4)

    def bwd(t0):
        def step(t, carry):
            r0 = pl.multiple_of((t0 - t) * SUBLANE, SUBLANE)
            h, carry = _tile_scan(a_scr[1, pl.ds(r0, SUBLANE), :], b_scr[1, pl.ds(r0, SUBLANE), :], carry, True)
            b_scr[1, pl.ds(r0, SUBLANE), :] = h
            return carry
        return step

    nct = n_ctx // SUBLANE
    carry = lax.fori_loop(0, nct, bwd(nct - 1), jnp.zeros((1, w), F32), unroll=4)
    lax.fori_loop(0, s // SUBLANE - nct, bwd(s // SUBLANE - 1), carry, unroll=4)

    def fin(j, carry):
        r0 = pl.multiple_of(j * r, r)
        hsum = b_scr[0, pl.ds(r0, r), :] + b_scr[1, pl.ds(r0, r), :]
        gate = g_ref[pl.ds(r0, r), :].astype(F32)
        o_ref[pl.ds(r0, r), :] = (hsum * jax.nn.gelu(gate)).astype(o_ref.dtype)
        return carry

    lax.fori_loop(0, s // r, fin, 0)


def _rglru(u, g, wcat, bcat, conv_w, conv_b, lam, n_ctx):
    b, s, w = u.shape
    blk = pl.BlockSpec((None, s, w), lambda bb: (bb, 0, 0))
    return pl.pallas_call(
        functools.partial(_rglru_body, n_ctx=n_ctx),
        grid=(b,),
        in_specs=[blk, blk, _const_spec(wcat.shape), _const_spec(bcat.shape),
                  _const_spec(conv_w.shape), _const_spec(conv_b.shape), _const_spec(lam.shape)],
        out_specs=blk,
        out_shape=jax.ShapeDtypeStruct((b, s, w), BF16),
        scratch_shapes=[pltpu.VMEM((2, s, w), F32), pltpu.VMEM((2, s, w), F32)],
        compiler_params=_cparams(("arbitrary",)),
        name="rglru",
    )(u, g, wcat, bcat, conv_w, conv_b, lam)


def _group_all(v, op):
    blk = 1
    while blk < EXPERTS_PER_GROUP:
        v = op(v, _swap_lanes(v, blk))
        blk *= 2
    return v


def _route_tile(scores, bias):
    neg = jnp.float32(-3.0e38)
    far = jnp.float32(1.0e9)
    lane = lax.broadcasted_iota(jnp.int32, scores.shape, 1)
    lane_f = lane.astype(F32)
    valid = lane < N_EXPERTS
    x = jnp.where(valid, scores + bias, neg)
    m1 = _group_all(x, jnp.maximum)
    i1 = _group_all(jnp.where(x == m1, lane_f, far), jnp.minimum)
    x2 = jnp.where(lane_f == i1, neg, x)
    m2 = _group_all(x2, jnp.maximum)
    i2 = _group_all(jnp.where(x2 == m2, lane_f, far), jnp.minimum)
    gs = jnp.where(valid, m1 + m2, neg)
    gmax = jnp.max(gs, axis=-1, keepdims=True)
    g0 = jnp.min(jnp.where(gs == gmax, lane_f, far), axis=-1, keepdims=True)
    in_best = jnp.where(lane_f >= g0, jnp.where(lane_f < g0 + EXPERTS_PER_GROUP, 1.0, 0.0), 0.0)
    sel1 = jnp.where(lane_f == i1, in_best, 0.0)
    sel2 = jnp.where(lane_f == i2, in_best, 0.0)
    both = sel1 + sel2
    wsum = jnp.sum(both * scores, axis=-1, keepdims=True)
    wmat = both * scores / wsum
    return wmat + pltpu.roll(sel1 + 2.0 * sel2, N_EXPERTS, 1)


def _merge_body(x_ref, mod_ref, g1_ref, g2_ref, ya_ref, yb_ref, yc_ref, yd_ref,
                wm_ref, bm_ref, wb_ref, wo_ref, rwh_ref, rwl_ref, xo_ref, h2_ref, sc_ref):
    d = x_ref.shape[1]
    x = x_ref[...]
    h = (_rms(x, g1_ref[...]) * (1.0 + mod_ref[1:2, :]) + mod_ref[0:1, :]).astype(BF16)
    acc = jnp.zeros(x.shape, F32)
    for n, y_ref in enumerate((ya_ref, yb_ref, yc_ref, yd_ref)):
        gate = jax.nn.sigmoid(jnp.dot(h, wm_ref[:, n * d:(n + 1) * d], preferred_element_type=F32)
                              + bm_ref[:, n * d:(n + 1) * d])
        acc = acc + gate * jnp.dot(y_ref[...], wb_ref[n], preferred_element_type=F32)
    y = jnp.dot(acc.astype(BF16), wo_ref[...], preferred_element_type=F32)
    xn = x + mod_ref[2:3, :] * y
    xo_ref[...] = xn
    h2 = _rms(xn, g2_ref[...]) * (1.0 + mod_ref[4:5, :]) + mod_ref[3:4, :]
    h2_ref[...] = h2
    h2_hi = h2.astype(BF16)
    h2_lo = (h2 - h2_hi.astype(F32)).astype(BF16)
    logits = (jnp.dot(h2_hi, rwh_ref[...], preferred_element_type=F32)
              + jnp.dot(h2_lo, rwh_ref[...], preferred_element_type=F32)
              + jnp.dot(h2_hi, rwl_ref[...], preferred_element_type=F32))
    sc_ref[...] = jax.nn.sigmoid(logits)


def _merge(x_all, mods, g1, g2, ys, wm, bm, wb, wo, rwh, rwl, n_ctx, with_ctx):
    b, s, d = x_all.shape
    tile0 = 0 if with_ctx else n_ctx // TM
    nt = s // TM - tile0
    tile = lambda w: pl.BlockSpec((None, TM, w), lambda bb, i: (bb, i + tile0, 0))
    sub = lambda w: pl.BlockSpec((None, TM, w), lambda bb, i: (bb, i, 0))
    so = nt * TM
    return pl.pallas_call(
        _merge_body,
        grid=(b, nt),
        in_specs=[tile(d),
                  pl.BlockSpec((None, None, 6, d), lambda bb, i: (bb, jnp.minimum(i + tile0, 1), 0, 0)),
                  _const_spec((1, d)), _const_spec((1, d)),
                  sub(BRANCH_W), tile(BRANCH_W), tile(BRANCH_W), sub(BRANCH_W),
                  _const_spec(wm.shape), _const_spec(bm.shape), _const_spec(wb.shape),
                  _const_spec(wo.shape), _const_spec(rwh.shape), _const_spec(rwl.shape)],
        out_specs=[sub(d), sub(d), sub(LANE)],
        out_shape=[jax.ShapeDtypeStruct((b, so, d), F32),
                   jax.ShapeDtypeStruct((b, so, d), F32),
                   jax.ShapeDtypeStruct((b, so, LANE), F32)],
        compiler_params=_cparams(("arbitrary", "arbitrary")),
        name="merge",
    )(x_all, mods, g1, g2, *ys, wm, bm, wb, wo, rwh, rwl)


def _expert_body(be_ref, nu_ref, x_ref, w1_ref, w3_ref, w2_ref, o_ref):
    j = pl.program_id(0)

    @pl.when(j < nu_ref[0])
    def _():
        x = x_ref[...].astype(BF16)
        a = jnp.dot(x, w1_ref[...].astype(BF16), preferred_element_type=F32)
        g = jnp.dot(x, w3_ref[...].astype(BF16), preferred_element_type=F32)
        hmid = (a * jax.nn.sigmoid(a) * g).astype(BF16)
        o_ref[...] = jnp.dot(hmid, w2_ref[...].astype(BF16), preferred_element_type=F32)

    @pl.when(j >= nu_ref[0])
    def _():
        o_ref[...] = jnp.zeros(o_ref.shape, o_ref.dtype)


def _expert_ffn(block_expert, n_used, xb, w1, w3, w2, layer):
    n_rows, d = xb.shape
    nb = n_rows // EXP_ROWS
    ff = w1.shape[-1]
    wspec = lambda r, c: pl.BlockSpec((None, None, r, c), lambda j, be, nu: (layer, be[j], 0, 0))
    grid_spec = pltpu.PrefetchScalarGridSpec(
        num_scalar_prefetch=2,
        grid=(nb,),
        in_specs=[pl.BlockSpec((EXP_ROWS, d), lambda j, be, nu: (jnp.minimum(j, nu[0] - 1), 0)),
                  wspec(d, ff), wspec(d, ff), wspec(ff, d)],
        out_specs=pl.BlockSpec((EXP_ROWS, d), lambda j, be, nu: (j, 0)),
    )
    return pl.pallas_call(
        _expert_body,
        grid_spec=grid_spec,
        out_shape=jax.ShapeDtypeStruct((n_rows, d), F32),
        compiler_params=_cparams(("arbitrary",)),
        name="expert_ffn",
    )(block_expert, n_used, xb, w1, w3, w2)


def _markers(route):
    lane = lax.broadcasted_iota(jnp.int32, route.shape, 1)
    return jnp.where(lane < N_EXPERTS, pltpu.roll(route, N_EXPERTS, 1), 0.0)


ROUTE_ROWS = 1024


def _route_body(pr_ref, rb_ref, rt_ref, cnt_ref):
    @pl.when(pl.program_id(0) == 0)
    def _():
        cnt_ref[...] = jnp.zeros(cnt_ref.shape, F32)

    rt = _route_tile(pr_ref[...], rb_ref[...])
    rt_ref[...] = rt
    cnt_ref[...] += jnp.sum(jnp.where(_markers(rt) > 0.0, 1.0, 0.0), axis=0, keepdims=True)


def _route(probs, rb):
    t = probs.shape[0]
    tile = pl.BlockSpec((ROUTE_ROWS, LANE), lambda i: (i, 0))
    return pl.pallas_call(
        _route_body,
        grid=(t // ROUTE_ROWS,),
        in_specs=[tile, _const_spec(rb.shape)],
        out_specs=[tile, pl.BlockSpec((SUBLANE, LANE), lambda i: (0, 0))],
        out_shape=[jax.ShapeDtypeStruct((t, LANE), F32), jax.ShapeDtypeStruct((SUBLANE, LANE), F32)],
        compiler_params=_cparams(("arbitrary",)),
        name="route",
    )(probs, rb)


def _rank_body(rt_ref, cnt_ref, slot_ref, run_scr, start_scr, tri_scr):
    rows = rt_ref.shape[0]

    @pl.when(pl.program_id(0) == 0)
    def _():
        cnt = cnt_ref[...]
        padded = jnp.floor((cnt + (EXP_ROWS - 1.0)) * (1.0 / EXP_ROWS)) * EXP_ROWS
        lane = lax.broadcasted_iota(jnp.int32, cnt.shape, 1)
        end = padded
        sh = 1
        while sh < LANE:
            end = end + jnp.where(lane >= sh, pltpu.roll(end, sh, 1), 0.0)
            sh *= 2
        start_scr[...] = end - padded
        run_scr[...] = jnp.zeros(run_scr.shape, F32)
        ri = lax.broadcasted_iota(jnp.int32, (rows, rows), 0)
        ci = lax.broadcasted_iota(jnp.int32, (rows, rows), 1)
        tri_scr[...] = jnp.where(ri > ci, 1.0, 0.0).astype(BF16)

    mk = _markers(rt_ref[...])
    p = jnp.where(mk > 0.0, 1.0, 0.0)
    before = jnp.dot(tri_scr[...], p.astype(BF16), preferred_element_type=F32)
    pos = before + run_scr[0:1, :] + start_scr[0:1, :]
    s1 = jnp.sum(jnp.where(mk == 1.0, pos, 0.0), axis=-1, keepdims=True)
    s2 = jnp.sum(jnp.where(mk == 2.0, pos, 0.0), axis=-1, keepdims=True)
    which = lax.broadcasted_iota(jnp.int32, (rows, TOP_K), 1)
    slot_ref[...] = jnp.where(which == 0, s1, s2).astype(jnp.int32)
    run_scr[...] += jnp.sum(p, axis=0, keepdims=True)


def _rank(route, counts):
    t = route.shape[0]
    return pl.pallas_call(
        _rank_body,
        grid=(t // ROUTE_ROWS,),
        in_specs=[pl.BlockSpec((ROUTE_ROWS, LANE), lambda i: (i, 0)), _const_spec((SUBLANE, LANE))],
        out_specs=pl.BlockSpec((ROUTE_ROWS, TOP_K), lambda i: (i, 0)),
        out_shape=jax.ShapeDtypeStruct((t, TOP_K), jnp.int32),
        scratch_shapes=[pltpu.VMEM((SUBLANE, LANE), F32), pltpu.VMEM((SUBLANE, LANE), F32),
                        pltpu.VMEM((ROUTE_ROWS, ROUTE_ROWS), BF16)],
        compiler_params=_cparams(("arbitrary",)),
        name="rank",
    )(route, counts)


def _block_experts(counts, n_blocks):
    cnt = counts[0, :N_EXPERTS].astype(jnp.int32)
    padded = (cnt + EXP_ROWS - 1) // EXP_ROWS * EXP_ROWS
    pad_end = jnp.cumsum(padded)
    first_row = jnp.arange(n_blocks, dtype=jnp.int32) * EXP_ROWS
    ended = jnp.sum((pad_end[None, :] <= first_row[:, None]).astype(jnp.int32), axis=1)
    block_expert = jnp.minimum(ended, N_EXPERTS - 1).astype(jnp.int32)
    n_used = (pad_end[-1] // EXP_ROWS).astype(jnp.int32).reshape(1)
    return block_expert, n_used, cnt, pad_end.astype(jnp.int32)


def _row_copy(src, src_row, dst, dst_row, sem):
    return pltpu.make_async_copy(src.at[pl.ds(src_row, 1)], dst.at[pl.ds(dst_row, 1)], sem)


def _dispatch_body(slot_ref, cnt_ref, end_ref, nu_ref, h_ref, xb_ref, zero_scr, sem, zsem):
    tm = h_ref.shape[0]

    @pl.when(pl.program_id(0) == 0)
    def _():
        zero_scr[...] = jnp.zeros(zero_scr.shape, zero_scr.dtype)

        def fill(e, carry):
            @pl.when(cnt_ref[e] > 0)
            def _():
                r0 = pl.multiple_of(end_ref[e] - EXP_ROWS, EXP_ROWS)
                pltpu.make_async_copy(zero_scr, xb_ref.at[pl.ds(r0, EXP_ROWS)], zsem).start()
            return carry

        def drain(e, carry):
            @pl.when(cnt_ref[e] > 0)
            def _():
                pltpu.make_async_copy(zero_scr, xb_ref.at[pl.ds(0, EXP_ROWS)], zsem).wait()
            return carry

        def fill_tail(j, carry):
            r0 = pl.multiple_of(j * EXP_ROWS, EXP_ROWS)
            pltpu.make_async_copy(zero_scr, xb_ref.at[pl.ds(r0, EXP_ROWS)], zsem).start()
            return carry

        def drain_tail(j, carry):
            pltpu.make_async_copy(zero_scr, xb_ref.at[pl.ds(0, EXP_ROWS)], zsem).wait()
            return carry

        n_blocks = xb_ref.shape[0] // EXP_ROWS
        lax.fori_loop(0, N_EXPERTS, fill, 0)
        lax.fori_loop(nu_ref[0], n_blocks, fill_tail, 0)
        lax.fori_loop(0, N_EXPERTS, drain, 0)
        lax.fori_loop(nu_ref[0], n_blocks, drain_tail, 0)

    for t in range(tm):
        for k in range(TOP_K):
            _row_copy(h_ref, t, xb_ref, slot_ref[0, TOP_K * t + k], sem).start(priority=k % 2)
    for k in range(TOP_K):
        pltpu.make_async_copy(h_ref, xb_ref.at[pl.ds(0, tm)], sem).wait()


def _dispatch(slot_tiles, cnt, pad_end, n_used, h2, n_rows):
    b, s, dh = h2.shape
    nt = b * s // TM
    smem = pl.BlockSpec(memory_space=pltpu.SMEM)
    return pl.pallas_call(
        _dispatch_body,
        grid=(nt,),
        in_specs=[pl.BlockSpec((None, 1, TOP_K * TM), lambda i: (i, 0, 0), memory_space=pltpu.SMEM),
                  smem, smem, smem,
                  pl.BlockSpec((TM, dh), lambda i: (i, 0))],
        out_specs=pl.BlockSpec(memory_space=pl.ANY),
        out_shape=jax.ShapeDtypeStruct((n_rows, dh), h2.dtype),
        scratch_shapes=[pltpu.VMEM((EXP_ROWS, dh), h2.dtype),
                        pltpu.SemaphoreType.DMA(()), pltpu.SemaphoreType.DMA(())],
        compiler_params=_cparams(("arbitrary",)),
        name="dispatch",
    )(slot_tiles, cnt, pad_end, n_used, h2.reshape(b * s, dh))


def _resid_body(slot_ref, x_ref, mod_ref, rt_ref, gf_ref, y_hbm, o_ref, ybuf0, ybuf1, sem, *, final):
    tm = x_ref.shape[0]
    ybuf = (ybuf0, ybuf1)

    for t in range(tm):
        for k in range(TOP_K):
            _row_copy(y_hbm, slot_ref[0, TOP_K * t + k], ybuf[k], t, sem).start(priority=k % 2)
    rt = rt_ref[...]
    mk = _markers(rt)
    w1 = jnp.sum(jnp.where(mk == 1.0, rt, 0.0), axis=-1, keepdims=True)
    w2 = jnp.sum(jnp.where(mk == 2.0, rt, 0.0), axis=-1, keepdims=True)
    for k in range(TOP_K):
        pltpu.make_async_copy(y_hbm.at[pl.ds(0, tm)], ybuf[k], sem).wait()
    f = w1 * ybuf0[...] + w2 * ybuf1[...]
    xn = x_ref[...] + mod_ref[5:6, :] * f
    if final:
        xn = _rms(xn, gf_ref[...])
    o_ref[...] = xn


def _moe_residual(slot_tiles, x_all, mods, route, gf, yblk, has_ctx, final):
    b, s, d = x_all.shape
    nt = s // TM
    tin = lambda ww: pl.BlockSpec((None, TM, ww), lambda bb, i: (bb, i, 0))
    mod_row = (lambda i: jnp.minimum(i, 1)) if has_ctx else (lambda i: 1)
    return pl.pallas_call(
        functools.partial(_resid_body, final=final),
        grid=(b, nt),
        in_specs=[pl.BlockSpec((None, 1, TOP_K * TM), lambda bb, i: (bb * nt + i, 0, 0),
                               memory_space=pltpu.SMEM),
                  tin(d),
                  pl.BlockSpec((None, None, 6, d), lambda bb, i: (bb, mod_row(i), 0, 0)),
                  tin(LANE), _const_spec((1, d)),
                  pl.BlockSpec(memory_space=pl.ANY)],
        out_specs=tin(d),
        out_shape=jax.ShapeDtypeStruct((b, s, d), F32),
        scratch_shapes=[pltpu.VMEM((TM, d), F32), pltpu.VMEM((TM, d), F32), pltpu.SemaphoreType.DMA(())],
        compiler_params=_cparams(("arbitrary", "arbitrary")),
        name="moe_residual",
    )(slot_tiles, x_all, mods, route, gf, yblk)


def kernel(x, c, ctx, c_ctx, ada_w, ada_b, norm1_g, norm2_g, w_in, mla_q_norm, mla_kv_norm, mla_w_uq,
           mla_w_ukv, ret_decay, ret_norm, lru_conv_w, lru_conv_b, lru_w_a, lru_b_a, lru_w_x, lru_b_x,
           lru_lambda, gqa_q_norm, gqa_k_norm, w_branch, w_merge, b_merge, w_out, router_w, router_bias,
           moe_w1, moe_w3, moe_w2, final_norm):
    b, seq, d = x.shape
    n_ctx = ctx.shape[1]
    depth = ada_w.shape[0]
    s = n_ctx + seq
    assert n_ctx == TM and seq % TM == 0 and seq % GRID_W == 0

    r_pad = -(-(b + 1) // SUBLANE) * SUBLANE
    cc = jnp.zeros((r_pad, d), F32).at[:b].set(c).at[b].set(c_ctx)
    mods_all = _ada_mods(cc, ada_w, ada_b)

    tabs = jnp.asarray(np.stack(_rope_slot_tables(n_ctx, seq, MLA_ROPE, MLA_ROT_LANES)
                                + _rope_slot_tables(n_ctx, seq, GQA_HEAD_DIM, HEAD_LANES)))

    in_cols = _in_proj_columns()
    uq_cols, ukv_cols = _mla_up_columns()
    rw = jnp.concatenate([router_w.astype(F32), jnp.zeros((d, LANE - N_EXPERTS), F32)], axis=1)
    rw_hi = rw.astype(BF16)
    rw_lo = (rw - rw_hi.astype(F32)).astype(BF16)
    rb = jnp.concatenate([router_bias.astype(F32), jnp.zeros((LANE - N_EXPERTS,), F32)])[None]

    x_all = jnp.concatenate([ctx, x], axis=1)
    out = None
    for l in range(depth):
        last = l == depth - 1
        m = mods_all[l].reshape(r_pad, 6, d)
        mods = jnp.stack([jnp.broadcast_to(m[b], (b, 6, d)), m[:b]], axis=1)

        head_gains = jnp.zeros((2, LANE), F32).at[:, HEAD_LANES].set(
            jnp.stack([gqa_q_norm[l], gqa_k_norm[l]]).astype(F32))

        win_p = _take_cols(w_in[l], in_cols).astype(BF16)
        wuq_p = _take_cols(mla_w_uq[l], uq_cols).astype(BF16)
        wukv_p = _take_cols(mla_w_ukv[l], ukv_cols).astype(BF16)

        (mq, mk, mv, rq, rk, rv, rg, lu, lg, gq, gk, gv) = _inproj(
            x_all, mods, norm1_g[l][None], tabs, win_p, wuq_p, wukv_p,
            mla_q_norm[l][None], mla_kv_norm[l][None], head_gains)

        ya = _attention(mq, mk, mv, MLA_HEADS, MLA_HEADS, n_ctx, not last)
        yd = _attention(gq, gk, gv, GQA_HEADS, GQA_KV_HEADS, n_ctx, not last)

        log_g = -jax.nn.softplus(-ret_decay[l].astype(F32))
        yb = _retention(log_g, rq, rk, rv, rg, ret_norm[l][None], n_ctx)

        eye = jnp.eye(LRU_BLOCKS, dtype=F32)

        def block_diag(wblk):
            return jnp.einsum('ncd,nm->ncmd', wblk, eye).reshape(LRU_WIDTH, LRU_WIDTH)

        wcat = jnp.concatenate([block_diag(lru_w_a[l, 0]), block_diag(lru_w_x[l, 0]),
                                block_diag(lru_w_a[l, 1]), block_diag(lru_w_x[l, 1])], axis=1).astype(BF16)
        bcat = jnp.concatenate([lru_b_a[l, 0], lru_b_x[l, 0], lru_b_a[l, 1], lru_b_x[l, 1]])[None]
        yc = _rglru(lu, lg, wcat, bcat, lru_conv_w[l], lru_conv_b[l][None], lru_lambda[l], n_ctx)

        x_all, h2, probs = _merge(
            x_all, mods, norm1_g[l][None], norm2_g[l][None], (ya, yb, yc, yd),
            w_merge[l].astype(BF16), b_merge[l][None], w_branch[l].astype(BF16), w_out[l].astype(BF16),
            rw_hi, rw_lo, n_ctx, not last)

        t = b * (seq if last else s)
        assert t % ROUTE_ROWS == 0
        n_blocks = -(-t * TOP_K // EXP_ROWS) + N_EXPERTS
        route, counts = _route(probs.reshape(t, LANE), rb)
        slot_tiles = _rank(route, counts).reshape(t // TM, 1, TOP_K * TM)
        route = route.reshape(b, t // b, LANE)
        block_expert, n_used, cnt, pad_end = _block_experts(counts, n_blocks)
        xb = _dispatch(slot_tiles, cnt, pad_end, n_used, h2, n_blocks * EXP_ROWS)
        yblk = _expert_ffn(block_expert, n_used, xb, moe_w1, moe_w3, moe_w2, l)
        out = _moe_residual(slot_tiles, x_all, mods, route, final_norm[None], yblk, not last, last)
        x_all = out
    return out
```

```python
import functools

import numpy as np
import jax
import jax.numpy as jnp
from jax import lax
from jax.experimental import pallas as pl
from jax.experimental.pallas import tpu as pltpu

F32 = jnp.float32
BF16 = jnp.bfloat16

LANE = 128
SUBLANE = 8
VMEM_LIMIT = 56 * 1024 * 1024

GRID_W = 64
ROPE_BASE = 10000.0
EPS = 1e-6
MLA_HEADS, MLA_NOPE, MLA_ROPE, MLA_V = 8, 64, 32, 64
MLA_Q_LORA, MLA_KV_LORA = 256, 128
RET_HEADS, RET_DK, RET_DV, RET_CHUNK = 4, 64, 128, 128
LRU_WIDTH, LRU_BLOCKS, LRU_C, CONV_W, CONV_PAD_LEFT = 512, 8, 8.0, 4, 2
LRU_BLOCK_W = LRU_WIDTH // LRU_BLOCKS
GQA_HEADS, GQA_KV_HEADS, GQA_HEAD_DIM = 8, 2, 64
N_BRANCH, BRANCH_W = 4, 512
N_EXPERTS, N_GROUPS, TOP_K, EXPERT_FF, MOE_BLOCK = 64, 8, 2, 256, 128
EXPERTS_PER_GROUP = N_EXPERTS // N_GROUPS

IN_SPLITS = (MLA_Q_LORA, MLA_KV_LORA, MLA_ROPE,
             RET_HEADS * RET_DK, RET_HEADS * RET_DK, RET_HEADS * RET_DV, RET_HEADS * RET_DV,
             LRU_WIDTH, LRU_WIDTH,
             GQA_HEADS * GQA_HEAD_DIM, GQA_KV_HEADS * GQA_HEAD_DIM, GQA_KV_HEADS * GQA_HEAD_DIM)
IN_OFF = tuple(int(o) for o in np.cumsum((0,) + IN_SPLITS))
D_IN = IN_OFF[-1]

TM = 256
EXP_ROWS = 256
HEAD_SLOT = LANE
ONE_LANE = 64

ZP_CQ, ZP_CKV, ZP_KR = 0, 256, 384
ZP_RQ, ZP_RK, ZP_RV, ZP_RG = 512, 1024, 1536, 2048
ZP_LU, ZP_LG = 2560, 3072
ZP_GQ, ZP_GK, ZP_GV = 3584, 4608, 4864
ZP_W = 5120


def _cparams(sem):
    return pltpu.CompilerParams(dimension_semantics=sem, vmem_limit_bytes=VMEM_LIMIT)


def _const_spec(shape):
    nd = len(shape)
    return pl.BlockSpec(shape, lambda *_: (0,) * nd, pipeline_mode=pl.Buffered(1))


PARTNER = LANE // 2


def _rot_lanes(rot_dim):
    q = rot_dim // 4
    e = np.arange(rot_dim)
    blk, o = e // q, e % q
    return np.where(blk == 0, o, np.where(blk == 1, PARTNER + o, np.where(blk == 2, q + o, PARTNER + q + o)))


HEAD_LANES = _rot_lanes(GQA_HEAD_DIM)
MLA_ROT_LANES = _rot_lanes(MLA_ROPE)
MLA_NOPE_LANES = np.array([l for l in range(MLA_NOPE + MLA_ROPE)
                           if l not in set(MLA_ROT_LANES.tolist())])


def _in_proj_columns():
    idx = np.full((ZP_W,), D_IN, np.int64)
    o = IN_OFF
    idx[ZP_CQ:ZP_CQ + 256] = o[0] + np.arange(256)
    idx[ZP_CKV:ZP_CKV + 128] = o[1] + np.arange(128)
    idx[ZP_KR + MLA_ROT_LANES] = o[2] + np.arange(MLA_ROPE)
    for h in range(RET_HEADS):
        idx[ZP_RQ + h * 128 + HEAD_LANES] = o[3] + h * 64 + np.arange(64)
        idx[ZP_RK + h * 128 + HEAD_LANES] = o[4] + h * 64 + np.arange(64)
    idx[ZP_RV:ZP_RV + 512] = o[5] + np.arange(512)
    idx[ZP_RG:ZP_RG + 512] = o[6] + np.arange(512)
    idx[ZP_LU:ZP_LU + 512] = o[7] + np.arange(512)
    idx[ZP_LG:ZP_LG + 512] = o[8] + np.arange(512)
    for h in range(GQA_HEADS):
        idx[ZP_GQ + h * 128 + HEAD_LANES] = o[9] + h * 64 + np.arange(64)
    for h in range(GQA_KV_HEADS):
        idx[ZP_GK + h * 128 + HEAD_LANES] = o[10] + h * 64 + np.arange(64)
        idx[ZP_GV + h * 128:ZP_GV + h * 128 + 64] = o[11] + h * 64 + np.arange(64)
    return idx


def _mla_up_columns():
    dq = MLA_NOPE + MLA_ROPE
    dkv = MLA_NOPE + MLA_V
    qi = np.full((MLA_HEADS * 128,), MLA_HEADS * dq, np.int64)
    ki = np.full((MLA_HEADS * 128,), MLA_HEADS * dkv, np.int64)
    vi = np.full((MLA_HEADS * 128,), MLA_HEADS * dkv, np.int64)
    for h in range(MLA_HEADS):
        qi[h * 128 + MLA_NOPE_LANES] = h * dq + np.arange(MLA_NOPE)
        qi[h * 128 + MLA_ROT_LANES] = h * dq + MLA_NOPE + np.arange(MLA_ROPE)
        ki[h * 128 + MLA_NOPE_LANES] = h * dkv + np.arange(MLA_NOPE)
        vi[h * 128:h * 128 + MLA_V] = h * dkv + MLA_NOPE + np.arange(MLA_V)
    return qi, np.concatenate([ki, vi])


def _take_cols(w, idx):
    wz = jnp.concatenate([w, jnp.zeros((w.shape[0], 1), w.dtype)], axis=1)
    return jnp.take(wz, jnp.asarray(idx, jnp.int32), axis=1)


def _rope_slot_tables(n_ctx, seq, rot_dim, lanes):
    half = rot_dim // 2
    pos = np.arange(seq)
    inv = ROPE_BASE ** (-np.arange(0, half, 2, dtype=np.float64) / half)
    ar = (pos // GRID_W)[:, None] * inv
    ac = (pos % GRID_W)[:, None] * inv
    cos = np.concatenate([np.cos(ar), np.cos(ar), np.cos(ac), np.cos(ac)], axis=1)
    sin = np.concatenate([-np.sin(ar), np.sin(ar), -np.sin(ac), np.sin(ac)], axis=1)
    cos_t = np.ones((n_ctx + seq, LANE), np.float32)
    sin_t = np.zeros((n_ctx + seq, LANE), np.float32)
    cos_t[n_ctx:, lanes] = cos
    sin_t[n_ctx:, lanes] = sin
    return cos_t, sin_t


def _swap_lanes(x, blk):
    n = x.shape[-1]
    lane = lax.broadcasted_iota(jnp.int32, x.shape, x.ndim - 1)
    up = pltpu.roll(x, n - blk, x.ndim - 1)
    dn = pltpu.roll(x, blk, x.ndim - 1)
    return jnp.where((lane % (2 * blk)) < blk, up, dn)


def _rms(x, g):
    return x * lax.rsqrt(jnp.mean(x * x, axis=-1, keepdims=True) + EPS) * g


def _ada_body(c_ref, w_ref, b_ref, o_ref):
    c = c_ref[...]
    s = (c * jax.nn.sigmoid(c)).astype(BF16)
    o_ref[...] = jnp.dot(s, w_ref[...].astype(BF16), preferred_element_type=F32) + b_ref[...]


def _ada_mods(cc, ada_w, ada_b):
    depth, d, n = ada_w.shape
    r = cc.shape[0]
    tn = 1536
    return pl.pallas_call(
        _ada_body,
        grid=(depth, n // tn),
        in_specs=[pl.BlockSpec((r, d), lambda l, j: (0, 0)),
                  pl.BlockSpec((None, d, tn), lambda l, j: (l, 0, j)),
                  pl.BlockSpec((None, 1, tn), lambda l, j: (l, 0, j))],
        out_specs=pl.BlockSpec((None, r, tn), lambda l, j: (l, 0, j)),
        out_shape=jax.ShapeDtypeStruct((depth, r, n), F32),
        compiler_params=_cparams(("arbitrary", "arbitrary")),
        name="ada_mods",
    )(cc, ada_w, ada_b.reshape(depth, 1, n))


def _inproj_body(x_ref, mod_ref, g1_ref, tab_ref, win_ref, wuq_ref, wukv_ref, gq_ref, gkv_ref, hg_ref,
                 mq_ref, mk_ref, mv_ref, rq_ref, rk_ref, rv_ref, rg_ref, lu_ref, lg_ref,
                 gq_out, gk_out, gv_out):
    x = x_ref[...]
    shift = mod_ref[0:1, :]
    scale = mod_ref[1:2, :]
    h = _rms(x, g1_ref[...]) * (1.0 + scale) + shift
    hb = h.astype(BF16)

    def proj(c0, c1):
        return jnp.dot(hb, win_ref[:, c0:c1], preferred_element_type=F32)

    lane = lax.broadcasted_iota(jnp.int32, (x.shape[0], LANE), 1)
    one_col = jnp.where(lane == ONE_LANE, 1.0, 0.0).astype(F32)

    def rope(v, ci):
        return v * tab_ref[ci] + pltpu.roll(v, PARTNER, 1) * tab_ref[ci + 1]

    mla_scale = (MLA_NOPE + MLA_ROPE) ** -0.5
    gqa_scale = GQA_HEAD_DIM ** -0.5

    cq = proj(ZP_CQ, ZP_CQ + 256)
    qn = _rms(cq, gq_ref[...]).astype(BF16)
    q = jnp.dot(qn, wuq_ref[...], preferred_element_type=F32)
    for hh in range(MLA_HEADS):
        sl = slice(hh * 128, (hh + 1) * 128)
        mq_ref[:, sl] = rope(q[:, sl] * mla_scale, 0).astype(BF16)
    ckv = proj(ZP_CKV, ZP_CKV + 128)
    kvn = _rms(ckv, gkv_ref[...]).astype(BF16)
    kv = jnp.dot(kvn, wukv_ref[...], preferred_element_type=F32)
    kr = rope(proj(ZP_KR, ZP_KR + 128), 0)
    for hh in range(MLA_HEADS):
        sl = slice(hh * 128, (hh + 1) * 128)
        mk_ref[sl, :] = (kv[:, sl] + kr).T.astype(BF16)
        mv_ref[:, sl] = (kv[:, MLA_HEADS * 128 + hh * 128:MLA_HEADS * 128 + (hh + 1) * 128] + one_col).astype(BF16)

    rq = proj(ZP_RQ, ZP_RQ + 512)
    rk = proj(ZP_RK, ZP_RK + 512)
    for hh in range(RET_HEADS):
        sl = slice(hh * 128, (hh + 1) * 128)
        rq_ref[:, sl] = rope(rq[:, sl], 2).astype(BF16)
        rk_ref[:, sl] = rope(rk[:, sl] * RET_DK ** -0.5, 2).astype(BF16)
    rv_ref[...] = proj(ZP_RV, ZP_RV + 512).astype(BF16)
    rg_ref[...] = proj(ZP_RG, ZP_RG + 512).astype(BF16)

    lu_ref[...] = proj(ZP_LU, ZP_LU + 512).astype(BF16)
    lg_ref[...] = proj(ZP_LG, ZP_LG + 512).astype(BF16)

    gq = proj(ZP_GQ, ZP_GQ + 1024)
    for hh in range(GQA_HEADS):
        sl = slice(hh * 128, (hh + 1) * 128)
        v = gq[:, sl]
        v = v * lax.rsqrt(jnp.sum(v * v, axis=-1, keepdims=True) * (1.0 / GQA_HEAD_DIM) + EPS)
        gq_out[:, sl] = rope(v * (hg_ref[0:1, :] * gqa_scale), 2).astype(BF16)
    gk = proj(ZP_GK, ZP_GK + 256)
    gv = proj(ZP_GV, ZP_GV + 256)
    for hh in range(GQA_KV_HEADS):
        sl = slice(hh * 128, (hh + 1) * 128)
        v = gk[:, sl]
        v = v * lax.rsqrt(jnp.sum(v * v, axis=-1, keepdims=True) * (1.0 / GQA_HEAD_DIM) + EPS)
        gk_out[sl, :] = rope(v * hg_ref[1:2, :], 2).T.astype(BF16)
        gv_out[:, sl] = (gv[:, sl] + one_col).astype(BF16)


def _inproj(x_all, mods, g1, tabs, win_p, wuq_p, wukv_p, gq, gkv, head_gains):
    b, s, d = x_all.shape
    nt = s // TM
    widths = (1024, 1024, 1024, 512, 512, 512, 512, 512, 512, 1024, 256, 256)
    tile = lambda w: pl.BlockSpec((None, TM, w), lambda i, bb: (bb, i, 0))
    ttile = lambda w: pl.BlockSpec((None, w, TM), lambda i, bb: (bb, 0, i))
    transposed = (1, 10)
    return pl.pallas_call(
        _inproj_body,
        grid=(nt, b),
        in_specs=[tile(d),
                  pl.BlockSpec((None, None, 6, d), lambda i, bb: (bb, jnp.minimum(i, 1), 0, 0)),
                  _const_spec((1, d)),
                  pl.BlockSpec((4, TM, LANE), lambda i, bb: (0, i, 0)),
                  _const_spec(win_p.shape), _const_spec(wuq_p.shape), _const_spec(wukv_p.shape),
                  _const_spec((1, MLA_Q_LORA)), _const_spec((1, MLA_KV_LORA)), _const_spec((2, LANE))],
        out_specs=[ttile(w) if j in transposed else tile(w) for j, w in enumerate(widths)],
        out_shape=[jax.ShapeDtypeStruct((b, w, s) if j in transposed else (b, s, w), BF16)
                   for j, w in enumerate(widths)],
        compiler_params=_cparams(("arbitrary", "arbitrary")),
        name="in_proj",
    )(x_all, mods, g1, tabs, win_p, wuq_p, wukv_p, gq, gkv, head_gains)


ATT_TILES = 2


def _attn_body(*refs, heads, kv_heads, n_q):
    q_refs, (kt_ref, v_ref, o_ref) = refs[:n_q], refs[n_q:]
    grp = heads // kv_heads
    for hp in range(heads // 2):
        outs = []
        for h in (2 * hp, 2 * hp + 1):
            g = h // grp
            sl = slice(h * 128, (h + 1) * 128)
            q = q_refs[0][:, sl] if n_q == 1 else jnp.concatenate([r[:, sl] for r in q_refs], axis=0)
            kt = kt_ref[g * 128:(g + 1) * 128, :]
            v = v_ref[:, g * 128:(g + 1) * 128]
            s = jnp.dot(q, kt, preferred_element_type=F32).astype(BF16)
            m = jnp.max(s, axis=-1, keepdims=True)
            p = jnp.exp(s - m)
            o = jnp.dot(p, v, preferred_element_type=F32)
            outs.append(o[:, :64] / o[:, ONE_LANE:ONE_LANE + 1])
        o_ref[:, hp * 128:(hp + 1) * 128] = jnp.concatenate(outs, axis=1).astype(o_ref.dtype)


def _attention_lat(q, kt, v, heads, kv_heads, n_ctx):
    b, s, _ = q.shape
    off = n_ctx // TM
    rows = ATT_TILES * TM
    nq = (s - n_ctx) // rows
    qspec = lambda r: pl.BlockSpec((None, TM, heads * 128), lambda bb, j: (bb, off + ATT_TILES * j + r, 0))
    return pl.pallas_call(
        functools.partial(_attn_body, heads=heads, kv_heads=kv_heads, n_q=ATT_TILES),
        grid=(b, nq),
        in_specs=[qspec(r) for r in range(ATT_TILES)] + [
            pl.BlockSpec((None, kv_heads * 128, s), lambda bb, j: (bb, 0, 0)),
            pl.BlockSpec((None, s, kv_heads * 128), lambda bb, j: (bb, 0, 0))],
        out_specs=pl.BlockSpec((None, rows, heads * 64), lambda bb, j: (bb, j, 0)),
        out_shape=jax.ShapeDtypeStruct((b, nq * rows, heads * 64), BF16),
        compiler_params=_cparams(("arbitrary", "arbitrary")),
        name="attention_lat_h%d_kv%d" % (heads, kv_heads),
    )(*([q] * ATT_TILES), kt, v)


def _attention_ctx(q, kt, v, heads, kv_heads, n_ctx):
    b = q.shape[0]
    return pl.pallas_call(
        functools.partial(_attn_body, heads=heads, kv_heads=kv_heads, n_q=1),
        grid=(b,),
        in_specs=[pl.BlockSpec((None, n_ctx, heads * 128), lambda bb: (bb, 0, 0)),
                  pl.BlockSpec((None, kv_heads * 128, n_ctx), lambda bb: (bb, 0, 0)),
                  pl.BlockSpec((None, n_ctx, kv_heads * 128), lambda bb: (bb, 0, 0))],
        out_specs=pl.BlockSpec((None, n_ctx, heads * 64), lambda bb: (bb, 0, 0)),
        out_shape=jax.ShapeDtypeStruct((b, n_ctx, heads * 64), BF16),
        compiler_params=_cparams(("arbitrary",)),
        name="attention_ctx_h%d_kv%d" % (heads, kv_heads),
    )(q, kt, v)


def _attention(q, kt, v, heads, kv_heads, n_ctx, with_ctx):
    y = _attention_lat(q, kt, v, heads, kv_heads, n_ctx)
    if with_ctx:
        y = jnp.concatenate([_attention_ctx(q, kt, v, heads, kv_heads, n_ctx), y], axis=1)
    return y


RET_ROWS = 256


def _retention_body(lg_ref, q_ref, k_ref, v_ref, g_ref, gn_ref, o_ref,
                    kv_scr, st_scr, dec_scr, m_scr, *, n_ctx):
    c = RET_ROWS
    s = q_ref.shape[0]
    nc = s // c
    nctx = n_ctx // c
    back_order = list(range(nctx - 1, -1, -1)) + list(range(nc - 1, nctx - 1, -1))
    pos = lax.broadcasted_iota(jnp.int32, (c, LANE), 0).astype(F32)
    ri = lax.broadcasted_iota(jnp.int32, (c, c), 0)
    ci = lax.broadcasted_iota(jnp.int32, (c, c), 1)
    diff = (ri - ci).astype(F32)
    heads = [(h, slice(h * 128, (h + 1) * 128)) for h in range(RET_HEADS)]

    for h, _ in heads:
        lgf = lg_ref[0, h]
        lgb = lg_ref[1, h]
        dec_scr[h, 0] = jnp.exp(lgf * (c - 1.0 - pos))
        dec_scr[h, 1] = jnp.exp(lgb * pos)
        dec_scr[h, 2] = jnp.exp(lgf * (pos + 1.0))
        dec_scr[h, 3] = jnp.exp(lgb * (c - pos))
        m_scr[h] = jnp.where(diff >= 0, jnp.exp(lgf * jnp.maximum(diff, 0.0)),
                             jnp.exp(lgb * jnp.maximum(-diff, 0.0)))

    def kv_step(j, carry):
        r0 = pl.multiple_of(j * c, c)
        for h, sl in heads:
            kc = k_ref[pl.ds(r0, c), sl].astype(F32)
            vc = v_ref[pl.ds(r0, c), sl]
            kk = jnp.concatenate([(kc * dec_scr[h, 0]).astype(BF16), (kc * dec_scr[h, 1]).astype(BF16)], axis=1)
            kv_scr[h, j] = lax.dot_general(kk, vc, (((0,), (0,)), ((), ())), preferred_element_type=F32)
        return carry

    lax.fori_loop(0, nc, kv_step, 0, unroll=3)

    for h, _ in heads:
        gcf = jnp.exp(lg_ref[0, h] * c)
        gcb = jnp.exp(lg_ref[1, h] * c)
        sf = jnp.zeros((128, 128), F32)
        for j in range(nc):
            st_scr[h, j, 0:128, :] = sf.astype(BF16)
            sf = sf * gcf + kv_scr[h, j, 0:128, :]
        sb = jnp.zeros((128, 128), F32)
        for j in back_order:
            st_scr[h, j, 128:256, :] = sb.astype(BF16)
            sb = sb * gcb + kv_scr[h, j, 128:256, :]

    def out_step(j, carry):
        r0 = pl.multiple_of(j * c, c)
        for h, sl in heads:
            qb = q_ref[pl.ds(r0, c), sl]
            kb = k_ref[pl.ds(r0, c), sl]
            vc = v_ref[pl.ds(r0, c), sl]
            sc = lax.dot_general(qb, kb, (((1,), (1,)), ((), ())), preferred_element_type=F32) * m_scr[h]
            o = jnp.dot(sc.astype(BF16), vc, preferred_element_type=F32)
            qf = qb.astype(F32)
            qd = jnp.concatenate([(qf * dec_scr[h, 2]).astype(BF16), (qf * dec_scr[h, 3]).astype(BF16)], axis=1)
            o = o + jnp.dot(qd, st_scr[h, j], preferred_element_type=F32)
            mu = jnp.mean(o, axis=-1, keepdims=True)
            oc = o - mu
            var = jnp.mean(oc * oc, axis=-1, keepdims=True)
            y = oc * lax.rsqrt(var + EPS) * gn_ref[:, sl]
            gate = g_ref[pl.ds(r0, c), sl].astype(F32)
            o_ref[pl.ds(r0, c), sl] = (gate * jax.nn.sigmoid(gate) * y).astype(o_ref.dtype)
        return carry

    lax.fori_loop(0, nc, out_step, 0, unroll=3)


def _retention(log_g, q, k, v, g, gn, n_ctx):
    b, s, w = v.shape
    nc = s // RET_ROWS
    blk = lambda ww: pl.BlockSpec((None, s, ww), lambda bb: (bb, 0, 0))
    return pl.pallas_call(
        functools.partial(_retention_body, n_ctx=n_ctx),
        grid=(b,),
        in_specs=[pl.BlockSpec(memory_space=pltpu.SMEM),
                  blk(512), blk(512), blk(512), blk(512), _const_spec((1, 512))],
        out_specs=blk(512),
        out_shape=jax.ShapeDtypeStruct((b, s, 512), BF16),
        scratch_shapes=[pltpu.VMEM((RET_HEADS, nc, 256, 128), F32),
                        pltpu.VMEM((RET_HEADS, nc, 256, 128), BF16),
                        pltpu.VMEM((RET_HEADS, 4, RET_ROWS, LANE), F32),
                        pltpu.VMEM((RET_HEADS, RET_ROWS, RET_ROWS), F32)],
        compiler_params=_cparams(("arbitrary",)),
        name="retention",
    )(log_g, q, k, v, g, gn)


LRU_ROWS = 256
LRU_HALO = 16


def _tile_scan(a, bv, carry, reverse):
    row = lax.broadcasted_iota(jnp.int32, a.shape, 0)
    for dlt in (1, 2, 4):
        if reverse:
            a_s = pltpu.roll(a, SUBLANE - dlt, 0)
            b_s = pltpu.roll(bv, SUBLANE - dlt, 0)
            ok = row < SUBLANE - dlt
        else:
            a_s = pltpu.roll(a, dlt, 0)
            b_s = pltpu.roll(bv, dlt, 0)
            ok = row >= dlt
        bv = jnp.where(ok, a * b_s + bv, bv)
        a = jnp.where(ok, a * a_s, a)
    h = a * carry + bv
    new_carry = h[0:1, :] if reverse else h[SUBLANE - 1:SUBLANE, :]
    return h, new_carry


def _rglru_body(u_ref, g_ref, wc_ref, bc_ref, cw_ref, cb_ref, lam_ref, o_ref, a_scr, b_scr, *, n_ctx):
    s = u_ref.shape[0]
    w = LRU_WIDTH
    r = LRU_ROWS
    cdec = [-LRU_C * jax.nn.softplus(-lam_ref[d:d + 1, :]) for d in range(2)]
    zeros = jnp.zeros((LRU_HALO, w), F32)

    for ch in range(s // r):
        r0 = ch * r
        seg_start = r0 == 0 or r0 == n_ctx
        seg_end = r0 + r == n_ctx or r0 + r == s
        lo = r0 if seg_start else r0 - LRU_HALO
        hi = r0 + r if seg_end else r0 + r + LRU_HALO
        parts = [u_ref[lo:hi, :].astype(F32)]
        if seg_start:
            parts = [zeros] + parts
        if seg_end:
            parts = parts + [zeros]
        ext = jnp.concatenate(parts, axis=0) if len(parts) > 1 else parts[0]
        n = ext.shape[0]
        u = cb_ref[...]
        for j in range(CONV_W):
            sh = (CONV_PAD_LEFT - j) % n
            tap = ext if sh == 0 else pltpu.roll(ext, sh, 0)
            u = u + tap[LRU_HALO:LRU_HALO + r, :] * cw_ref[j:j + 1, :]
        gates = jnp.dot(u.astype(BF16), wc_ref[...], preferred_element_type=F32) + bc_ref[...]
        for d in range(2):
            rg = jax.nn.sigmoid(gates[:, (2 * d) * w:(2 * d + 1) * w])
            ig = jax.nn.sigmoid(gates[:, (2 * d + 1) * w:(2 * d + 2) * w])
            a = jnp.exp(rg * cdec[d])
            a_scr[d, r0:r0 + r, :] = a
            b_scr[d, r0:r0 + r, :] = jnp.sqrt(1.0 - a * a) * (ig * u)

    def fwd(t, carry):
        r0 = pl.multiple_of(t * SUBLANE, SUBLANE)
        h, carry = _tile_scan(a_scr[0, pl.ds(r0, SUBLANE), :], b_scr[0, pl.ds(r0, SUBLANE), :], carry, False)
        b_scr[0, pl.ds(r0, SUBLANE), :] = h
        return carry

    lax.fori_loop(0, s // SUBLANE, fwd, jnp.zeros((1, w), F32), unroll=4)

    def bwd(t0):
        def step(t, carry):
            r0 = pl.multiple_of((t0 - t) * SUBLANE, SUBLANE)
            h, carry = _tile_scan(a_scr[1, pl.ds(r0, SUBLANE), :], b_scr[1, pl.ds(r0, SUBLANE), :], carry, True)
            b_scr[1, pl.ds(r0, SUBLANE), :] = h
            return carry
        return step

    nct = n_ctx // SUBLANE
    carry = lax.fori_loop(0, nct, bwd(nct - 1), jnp.zeros((1, w), F32), unroll=4)
    lax.fori_loop(0, s // SUBLANE - nct, bwd(s // SUBLANE - 1), carry, unroll=4)

    def fin(j, carry):
        r0 = pl.multiple_of(j * r, r)
        hsum = b_scr[0, pl.ds(r0, r), :] + b_scr[1, pl.ds(r0, r), :]
        gate = g_ref[pl.ds(r0, r), :].astype(F32)
        o_ref[pl.ds(r0, r), :] = (hsum * jax.nn.gelu(gate)).astype(o_ref.dtype)
        return carry

    lax.fori_loop(0, s // r, fin, 0)


def _rglru(u, g, wcat, bcat, conv_w, conv_b, lam, n_ctx):
    b, s, w = u.shape
    blk = pl.BlockSpec((None, s, w), lambda bb: (bb, 0, 0))
    return pl.pallas_call(
        functools.partial(_rglru_body, n_ctx=n_ctx),
        grid=(b,),
        in_specs=[blk, blk, _const_spec(wcat.shape), _const_spec(bcat.shape),
                  _const_spec(conv_w.shape), _const_spec(conv_b.shape), _const_spec(lam.shape)],
        out_specs=blk,
        out_shape=jax.ShapeDtypeStruct((b, s, w), BF16),
        scratch_shapes=[pltpu.VMEM((2, s, w), F32), pltpu.VMEM((2, s, w), F32)],
        compiler_params=_cparams(("arbitrary",)),
        name="rglru",
    )(u, g, wcat, bcat, conv_w, conv_b, lam)


def _route_tile(scores, bias):
    r = scores.shape[0]
    neg = jnp.float32(-1.0e30)
    far = jnp.float32(1.0e9)
    ng = LANE // EXPERTS_PER_GROUP
    shape = (ng, EXPERTS_PER_GROUP, r)
    lane = lax.broadcasted_iota(jnp.int32, scores.shape, 1)
    x = jnp.where(lane < N_EXPERTS, scores + bias, neg).T.reshape(shape)
    st = scores.T.reshape(shape)
    sub = lax.broadcasted_iota(jnp.int32, shape, 1).astype(F32)
    grp = lax.broadcasted_iota(jnp.int32, shape, 0).astype(F32)
    m1 = jnp.max(x, axis=1, keepdims=True)
    i1 = jnp.min(jnp.where(x == m1, sub, far), axis=1, keepdims=True)
    x2 = jnp.where(sub == i1, neg, x)
    m2 = jnp.max(x2, axis=1, keepdims=True)
    i2 = jnp.min(jnp.where(x2 == m2, sub, far), axis=1, keepdims=True)
    gs = m1 + m2
    gmax = jnp.max(gs, axis=0, keepdims=True)
    grp1 = lax.broadcasted_iota(jnp.int32, (ng, 1, r), 0).astype(F32)
    g0 = jnp.min(jnp.where(gs == gmax, grp1, far), axis=0, keepdims=True)
    in_best = jnp.where(grp == g0, 1.0, 0.0)
    sel1 = jnp.where(sub == i1, in_best, 0.0)
    sel2 = jnp.where(sub == i2, in_best, 0.0)
    both = sel1 + sel2
    wsum = jnp.sum(jnp.sum(both * st, axis=1, keepdims=True), axis=0, keepdims=True)
    wmat = both * st / wsum
    mark = sel1 + 2.0 * sel2
    real = N_EXPERTS // EXPERTS_PER_GROUP
    return jnp.concatenate([wmat[:real], mark[:real]], axis=0).reshape(LANE, r).T


def _merge_body(x_ref, mod_ref, g1_ref, g2_ref, ya_ref, yb_ref, yc_ref, yd_ref,
                wm_ref, bm_ref, wb_ref, wo_ref, rwh_ref, rwl_ref, xo_ref, h2_ref, sc_ref):
    d = x_ref.shape[1]
    x = x_ref[...]
    h = (_rms(x, g1_ref[...]) * (1.0 + mod_ref[1:2, :]) + mod_ref[0:1, :]).astype(BF16)
    acc = jnp.zeros(x.shape, F32)
    for n, y_ref in enumerate((ya_ref, yb_ref, yc_ref, yd_ref)):
        gate = jax.nn.sigmoid(jnp.dot(h, wm_ref[:, n * d:(n + 1) * d], preferred_element_type=F32)
                              + bm_ref[:, n * d:(n + 1) * d])
        acc = acc + gate * jnp.dot(y_ref[...], wb_ref[n], preferred_element_type=F32)
    y = jnp.dot(acc.astype(BF16), wo_ref[...], preferred_element_type=F32)
    xn = x + mod_ref[2:3, :] * y
    xo_ref[...] = xn
    h2 = _rms(xn, g2_ref[...]) * (1.0 + mod_ref[4:5, :]) + mod_ref[3:4, :]
    h2_ref[...] = h2
    h2_hi = h2.astype(BF16)
    h2_lo = (h2 - h2_hi.astype(F32)).astype(BF16)
    logits = (jnp.dot(h2_hi, rwh_ref[...], preferred_element_type=F32)
              + jnp.dot(h2_lo, rwh_ref[...], preferred_element_type=F32)
              + jnp.dot(h2_hi, rwl_ref[...], preferred_element_type=F32))
    sc_ref[...] = jax.nn.sigmoid(logits)


def _merge(x_all, mods, g1, g2, ys, wm, bm, wb, wo, rwh, rwl, n_ctx, with_ctx):
    b, s, d = x_all.shape
    tile0 = 0 if with_ctx else n_ctx // TM
    nt = s // TM - tile0
    tile = lambda w: pl.BlockSpec((None, TM, w), lambda bb, i: (bb, i + tile0, 0))
    sub = lambda w: pl.BlockSpec((None, TM, w), lambda bb, i: (bb, i, 0))
    so = nt * TM
    return pl.pallas_call(
        _merge_body,
        grid=(b, nt),
        in_specs=[tile(d),
                  pl.BlockSpec((None, None, 6, d), lambda bb, i: (bb, jnp.minimum(i + tile0, 1), 0, 0)),
                  _const_spec((1, d)), _const_spec((1, d)),
                  sub(BRANCH_W), tile(BRANCH_W), tile(BRANCH_W), sub(BRANCH_W),
                  _const_spec(wm.shape), _const_spec(bm.shape), _const_spec(wb.shape),
                  _const_spec(wo.shape), _const_spec(rwh.shape), _const_spec(rwl.shape)],
        out_specs=[sub(d), sub(d), sub(LANE)],
        out_shape=[jax.ShapeDtypeStruct((b, so, d), F32),
                   jax.ShapeDtypeStruct((b, so, d), F32),
                   jax.ShapeDtypeStruct((b, so, LANE), F32)],
        compiler_params=_cparams(("arbitrary", "arbitrary")),
        name="merge",
    )(x_all, mods, g1, g2, *ys, wm, bm, wb, wo, rwh, rwl)


def _expert_body(be_ref, nu_ref, x_ref, w1_ref, w3_ref, w2_ref, o_ref):
    j = pl.program_id(0)

    @pl.when(j < nu_ref[0])
    def _():
        x = x_ref[...].astype(BF16)
        a = jnp.dot(x, w1_ref[...].astype(BF16), preferred_element_type=F32)
        g = jnp.dot(x, w3_ref[...].astype(BF16), preferred_element_type=F32)
        hmid = (a * jax.nn.sigmoid(a) * g).astype(BF16)
        o_ref[...] = jnp.dot(hmid, w2_ref[...].astype(BF16), preferred_element_type=F32)

    @pl.when(j >= nu_ref[0])
    def _():
        o_ref[...] = jnp.zeros(o_ref.shape, o_ref.dtype)


def _expert_ffn(block_expert, n_used, xb, w1, w3, w2, layer):
    n_rows, d = xb.shape
    nb = n_rows // EXP_ROWS
    ff = w1.shape[-1]
    wspec = lambda r, c: pl.BlockSpec((None, None, r, c), lambda j, be, nu: (layer, be[j], 0, 0))
    grid_spec = pltpu.PrefetchScalarGridSpec(
        num_scalar_prefetch=2,
        grid=(nb,),
        in_specs=[pl.BlockSpec((EXP_ROWS, d), lambda j, be, nu: (jnp.minimum(j, nu[0] - 1), 0)),
                  wspec(d, ff), wspec(d, ff), wspec(ff, d)],
        out_specs=pl.BlockSpec((EXP_ROWS, d), lambda j, be, nu: (j, 0)),
    )
    return pl.pallas_call(
        _expert_body,
        grid_spec=grid_spec,
        out_shape=jax.ShapeDtypeStruct((n_rows, d), F32),
        compiler_params=_cparams(("arbitrary",)),
        name="expert_ffn",
    )(block_expert, n_used, xb, w1, w3, w2)


def _markers(route):
    lane = lax.broadcasted_iota(jnp.int32, route.shape, 1)
    return jnp.where(lane < N_EXPERTS, pltpu.roll(route, N_EXPERTS, 1), 0.0)


ROUTE_ROWS = 1024


def _route_body(pr_ref, rb_ref, rt_ref, cnt_ref):
    @pl.when(pl.program_id(0) == 0)
    def _():
        cnt_ref[...] = jnp.zeros(cnt_ref.shape, F32)

    rt = _route_tile(pr_ref[...], rb_ref[...])
    rt_ref[...] = rt
    cnt_ref[...] += jnp.sum(jnp.where(_markers(rt) > 0.0, 1.0, 0.0), axis=0, keepdims=True)


def _route(probs, rb):
    t = probs.shape[0]
    tile = pl.BlockSpec((ROUTE_ROWS, LANE), lambda i: (i, 0))
    return pl.pallas_call(
        _route_body,
        grid=(t // ROUTE_ROWS,),
        in_specs=[tile, _const_spec(rb.shape)],
        out_specs=[tile, pl.BlockSpec((SUBLANE, LANE), lambda i: (0, 0))],
        out_shape=[jax.ShapeDtypeStruct((t, LANE), F32), jax.ShapeDtypeStruct((SUBLANE, LANE), F32)],
        compiler_params=_cparams(("arbitrary",)),
        name="route",
    )(probs, rb)


def _rank_body(rt_ref, cnt_ref, slot_ref, run_scr, start_scr, tri_scr):
    rows = rt_ref.shape[0]

    @pl.when(pl.program_id(0) == 0)
    def _():
        cnt = cnt_ref[...]
        padded = jnp.floor((cnt + (EXP_ROWS - 1.0)) * (1.0 / EXP_ROWS)) * EXP_ROWS
        lane = lax.broadcasted_iota(jnp.int32, cnt.shape, 1)
        end = padded
        sh = 1
        while sh < LANE:
            end = end + jnp.where(lane >= sh, pltpu.roll(end, sh, 1), 0.0)
            sh *= 2
        start_scr[...] = end - padded
        run_scr[...] = jnp.zeros(run_scr.shape, F32)
        ri = lax.broadcasted_iota(jnp.int32, (rows, rows), 0)
        ci = lax.broadcasted_iota(jnp.int32, (rows, rows), 1)
        tri_scr[...] = jnp.where(ri > ci, 1.0, 0.0).astype(BF16)

    mk = _markers(rt_ref[...])
    p = jnp.where(mk > 0.0, 1.0, 0.0)
    before = jnp.dot(tri_scr[...], p.astype(BF16), preferred_element_type=F32)
    pos = before + run_scr[0:1, :] + start_scr[0:1, :]
    s1 = jnp.sum(jnp.where(mk == 1.0, pos, 0.0), axis=-1, keepdims=True)
    s2 = jnp.sum(jnp.where(mk == 2.0, pos, 0.0), axis=-1, keepdims=True)
    which = lax.broadcasted_iota(jnp.int32, (rows, TOP_K), 1)
    slot_ref[...] = jnp.where(which == 0, s1, s2).astype(jnp.int32)
    run_scr[...] += jnp.sum(p, axis=0, keepdims=True)


def _rank(route, counts):
    t = route.shape[0]
    return pl.pallas_call(
        _rank_body,
        grid=(t // ROUTE_ROWS,),
        in_specs=[pl.BlockSpec((ROUTE_ROWS, LANE), lambda i: (i, 0)), _const_spec((SUBLANE, LANE))],
        out_specs=pl.BlockSpec((ROUTE_ROWS, TOP_K), lambda i: (i, 0)),
        out_shape=jax.ShapeDtypeStruct((t, TOP_K), jnp.int32),
        scratch_shapes=[pltpu.VMEM((SUBLANE, LANE), F32), pltpu.VMEM((SUBLANE, LANE), F32),
                        pltpu.VMEM((ROUTE_ROWS, ROUTE_ROWS), BF16)],
        compiler_params=_cparams(("arbitrary",)),
        name="rank",
    )(route, counts)


def _block_experts(counts, n_blocks):
    cnt = counts[0, :N_EXPERTS].astype(jnp.int32)
    padded = (cnt + EXP_ROWS - 1) // EXP_ROWS * EXP_ROWS
    pad_end = jnp.cumsum(padded)
    first_row = jnp.arange(n_blocks, dtype=jnp.int32) * EXP_ROWS
    ended = jnp.sum((pad_end[None, :] <= first_row[:, None]).astype(jnp.int32), axis=1)
    block_expert = jnp.minimum(ended, N_EXPERTS - 1).astype(jnp.int32)
    n_used = (pad_end[-1] // EXP_ROWS).astype(jnp.int32).reshape(1)
    return block_expert, n_used, cnt, pad_end.astype(jnp.int32)


def _row_copy(src, src_row, dst, dst_row, sem):
    return pltpu.make_async_copy(src.at[pl.ds(src_row, 1)], dst.at[pl.ds(dst_row, 1)], sem)


def _dispatch_body(slot_ref, cnt_ref, end_ref, nu_ref, h_ref, xb_ref, zero_scr, sem, zsem):
    tm = h_ref.shape[0]

    @pl.when(pl.program_id(0) == 0)
    def _():
        zero_scr[...] = jnp.zeros(zero_scr.shape, zero_scr.dtype)

        def fill(e, carry):
            @pl.when(cnt_ref[e] > 0)
            def _():
                r0 = pl.multiple_of(end_ref[e] - EXP_ROWS, EXP_ROWS)
                pltpu.make_async_copy(zero_scr, xb_ref.at[pl.ds(r0, EXP_ROWS)], zsem).start()
            return carry

        def drain(e, carry):
            @pl.when(cnt_ref[e] > 0)
            def _():
                pltpu.make_async_copy(zero_scr, xb_ref.at[pl.ds(0, EXP_ROWS)], zsem).wait()
            return carry

        def fill_tail(j, carry):
            r0 = pl.multiple_of(j * EXP_ROWS, EXP_ROWS)
            pltpu.make_async_copy(zero_scr, xb_ref.at[pl.ds(r0, EXP_ROWS)], zsem).start()
            return carry

        def drain_tail(j, carry):
            pltpu.make_async_copy(zero_scr, xb_ref.at[pl.ds(0, EXP_ROWS)], zsem).wait()
            return carry

        n_blocks = xb_ref.shape[0] // EXP_ROWS
        lax.fori_loop(0, N_EXPERTS, fill, 0)
        lax.fori_loop(nu_ref[0], n_blocks, fill_tail, 0)
        lax.fori_loop(0, N_EXPERTS, drain, 0)
        lax.fori_loop(nu_ref[0], n_blocks, drain_tail, 0)

    for t in range(tm):
        for k in range(TOP_K):
            _row_copy(h_ref, t, xb_ref, slot_ref[0, TOP_K * t + k], sem).start(priority=k % 2)
    for k in range(TOP_K):
        pltpu.make_async_copy(h_ref, xb_ref.at[pl.ds(0, tm)], sem).wait()


def _dispatch(slot_tiles, cnt, pad_end, n_used, h2, n_rows):
    b, s, dh = h2.shape
    nt = b * s // TM
    smem = pl.BlockSpec(memory_space=pltpu.SMEM)
    return pl.pallas_call(
        _dispatch_body,
        grid=(nt,),
        in_specs=[pl.BlockSpec((None, 1, TOP_K * TM), lambda i: (i, 0, 0), memory_space=pltpu.SMEM),
                  smem, smem, smem,
                  pl.BlockSpec((TM, dh), lambda i: (i, 0))],
        out_specs=pl.BlockSpec(memory_space=pl.ANY),
        out_shape=jax.ShapeDtypeStruct((n_rows, dh), h2.dtype),
        scratch_shapes=[pltpu.VMEM((EXP_ROWS, dh), h2.dtype),
                        pltpu.SemaphoreType.DMA(()), pltpu.SemaphoreType.DMA(())],
        compiler_params=_cparams(("arbitrary",)),
        name="dispatch",
    )(slot_tiles, cnt, pad_end, n_used, h2.reshape(b * s, dh))


def _resid_body(slot_ref, x_ref, mod_ref, rt_ref, gf_ref, y_hbm, o_ref, ybuf0, ybuf1, sem, *, final):
    tm = x_ref.shape[0]
    ybuf = (ybuf0, ybuf1)

    for t in range(tm):
        for k in range(TOP_K):
            _row_copy(y_hbm, slot_ref[0, TOP_K * t + k], ybuf[k], t, sem).start(priority=k % 2)
    rt = rt_ref[...]
    mk = _markers(rt)
    w1 = jnp.sum(jnp.where(mk == 1.0, rt, 0.0), axis=-1, keepdims=True)
    w2 = jnp.sum(jnp.where(mk == 2.0, rt, 0.0), axis=-1, keepdims=True)
    for k in range(TOP_K):
        pltpu.make_async_copy(y_hbm.at[pl.ds(0, tm)], ybuf[k], sem).wait()
    f = w1 * ybuf0[...] + w2 * ybuf1[...]
    xn = x_ref[...] + mod_ref[5:6, :] * f
    if final:
        xn = _rms(xn, gf_ref[...])
    o_ref[...] = xn


def _moe_residual(slot_tiles, x_all, mods, route, gf, yblk, has_ctx, final):
    b, s, d = x_all.shape
    nt = s // TM
    tin = lambda ww: pl.BlockSpec((None, TM, ww), lambda bb, i: (bb, i, 0))
    mod_row = (lambda i: jnp.minimum(i, 1)) if has_ctx else (lambda i: 1)
    return pl.pallas_call(
        functools.partial(_resid_body, final=final),
        grid=(b, nt),
        in_specs=[pl.BlockSpec((None, 1, TOP_K * TM), lambda bb, i: (bb * nt + i, 0, 0),
                               memory_space=pltpu.SMEM),
                  tin(d),
                  pl.BlockSpec((None, None, 6, d), lambda bb, i: (bb, mod_row(i), 0, 0)),
                  tin(LANE), _const_spec((1, d)),
                  pl.BlockSpec(memory_space=pl.ANY)],
        out_specs=tin(d),
        out_shape=jax.ShapeDtypeStruct((b, s, d), F32),
        scratch_shapes=[pltpu.VMEM((TM, d), F32), pltpu.VMEM((TM, d), F32), pltpu.SemaphoreType.DMA(())],
        compiler_params=_cparams(("arbitrary", "arbitrary")),
        name="moe_residual",
    )(slot_tiles, x_all, mods, route, gf, yblk)


def kernel(x, c, ctx, c_ctx, ada_w, ada_b, norm1_g, norm2_g, w_in, mla_q_norm, mla_kv_norm, mla_w_uq,
           mla_w_ukv, ret_decay, ret_norm, lru_conv_w, lru_conv_b, lru_w_a, lru_b_a, lru_w_x, lru_b_x,
           lru_lambda, gqa_q_norm, gqa_k_norm, w_branch, w_merge, b_merge, w_out, router_w, router_bias,
           moe_w1, moe_w3, moe_w2, final_norm):
    b, seq, d = x.shape
    n_ctx = ctx.shape[1]
    depth = ada_w.shape[0]
    s = n_ctx + seq
    assert n_ctx == TM and seq % TM == 0 and seq % GRID_W == 0

    r_pad = -(-(b + 1) // SUBLANE) * SUBLANE
    cc = jnp.zeros((r_pad, d), F32).at[:b].set(c).at[b].set(c_ctx)
    mods_all = _ada_mods(cc, ada_w, ada_b)

    tabs = jnp.asarray(np.stack(_rope_slot_tables(n_ctx, seq, MLA_ROPE, MLA_ROT_LANES)
                                + _rope_slot_tables(n_ctx, seq, GQA_HEAD_DIM, HEAD_LANES)))

    in_cols = _in_proj_columns()
    uq_cols, ukv_cols = _mla_up_columns()
    rw = jnp.concatenate([router_w.astype(F32), jnp.zeros((d, LANE - N_EXPERTS), F32)], axis=1)
    rw_hi = rw.astype(BF16)
    rw_lo = (rw - rw_hi.astype(F32)).astype(BF16)
    rb = jnp.concatenate([router_bias.astype(F32), jnp.zeros((LANE - N_EXPERTS,), F32)])[None]

    x_all = jnp.concatenate([ctx, x], axis=1)
    out = None
    for l in range(depth):
        last = l == depth - 1
        m = mods_all[l].reshape(r_pad, 6, d)
        mods = jnp.stack([jnp.broadcast_to(m[b], (b, 6, d)), m[:b]], axis=1)

        head_gains = jnp.zeros((2, LANE), F32).at[:, HEAD_LANES].set(
            jnp.stack([gqa_q_norm[l], gqa_k_norm[l]]).astype(F32))

        win_p = _take_cols(w_in[l], in_cols).astype(BF16)
        wuq_p = _take_cols(mla_w_uq[l], uq_cols).astype(BF16)
        wukv_p = _take_cols(mla_w_ukv[l], ukv_cols).astype(BF16)

        (mq, mk, mv, rq, rk, rv, rg, lu, lg, gq, gk, gv) = _inproj(
            x_all, mods, norm1_g[l][None], tabs, win_p, wuq_p, wukv_p,
            mla_q_norm[l][None], mla_kv_norm[l][None], head_gains)

        ya = _attention(mq, mk, mv, MLA_HEADS, MLA_HEADS, n_ctx, not last)
        yd = _attention(gq, gk, gv, GQA_HEADS, GQA_KV_HEADS, n_ctx, not last)

        log_g = -jax.nn.softplus(-ret_decay[l].astype(F32))
        yb = _retention(log_g, rq, rk, rv, rg, ret_norm[l][None], n_ctx)

        eye = jnp.eye(LRU_BLOCKS, dtype=F32)

        def block_diag(wblk):
            return jnp.einsum('ncd,nm->ncmd', wblk, eye).reshape(LRU_WIDTH, LRU_WIDTH)

        wcat = jnp.concatenate([block_diag(lru_w_a[l, 0]), block_diag(lru_w_x[l, 0]),
                                block_diag(lru_w_a[l, 1]), block_diag(lru_w_x[l, 1])], axis=1).astype(BF16)
        bcat = jnp.concatenate([lru_b_a[l, 0], lru_b_x[l, 0], lru_b_a[l, 1], lru_b_x[l, 1]])[None]
        yc = _rglru(lu, lg, wcat, bcat, lru_conv_w[l], lru_conv_b[l][None], lru_lambda[l], n_ctx)

        x_all, h2, probs = _merge(
            x_all, mods, norm1_g[l][None], norm2_g[l][None], (ya, yb, yc, yd),
            w_merge[l].astype(BF16), b_merge[l][None], w_branch[l].astype(BF16), w_out[l].astype(BF16),
            rw_hi, rw_lo, n_ctx, not last)

        t = b * (seq if last else s)
        assert t % ROUTE_ROWS == 0
        n_blocks = -(-t * TOP_K // EXP_ROWS) + N_EXPERTS
        route, counts = _route(probs.reshape(t, LANE), rb)
        slot_tiles = _rank(route, counts).reshape(t // TM, 1, TOP_K * TM)
        route = route.reshape(b, t // b, LANE)
        block_expert, n_used, cnt, pad_end = _block_experts(counts, n_blocks)
        xb = _dispatch(slot_tiles, cnt, pad_end, n_used, h2, n_blocks * EXP_ROWS)
        yblk = _expert_ffn(block_expert, n_used, xb, moe_w1, moe_w3, moe_w2, l)
        out = _moe_residual(slot_tiles, x_all, mods, route, final_norm[None], yblk, not last, last)
        x_all = out
    return out
```

```python
import functools

import numpy as np
import jax
import jax.numpy as jnp
from jax import lax
from jax.experimental import pallas as pl
from jax.experimental.pallas import tpu as pltpu

F32 = jnp.float32
BF16 = jnp.bfloat16

LANE = 128
SUBLANE = 8
VMEM_LIMIT = 56 * 1024 * 1024

GRID_W = 64
ROPE_BASE = 10000.0
EPS = 1e-6
MLA_HEADS, MLA_NOPE, MLA_ROPE, MLA_V = 8, 64, 32, 64
MLA_Q_LORA, MLA_KV_LORA = 256, 128
RET_HEADS, RET_DK, RET_DV, RET_CHUNK = 4, 64, 128, 128
LRU_WIDTH, LRU_BLOCKS, LRU_C, CONV_W, CONV_PAD_LEFT = 512, 8, 8.0, 4, 2
LRU_BLOCK_W = LRU_WIDTH // LRU_BLOCKS
GQA_HEADS, GQA_KV_HEADS, GQA_HEAD_DIM = 8, 2, 64
N_BRANCH, BRANCH_W = 4, 512
N_EXPERTS, N_GROUPS, TOP_K, EXPERT_FF, MOE_BLOCK = 64, 8, 2, 256, 128
EXPERTS_PER_GROUP = N_EXPERTS // N_GROUPS

IN_SPLITS = (MLA_Q_LORA, MLA_KV_LORA, MLA_ROPE,
             RET_HEADS * RET_DK, RET_HEADS * RET_DK, RET_HEADS * RET_DV, RET_HEADS * RET_DV,
             LRU_WIDTH, LRU_WIDTH,
             GQA_HEADS * GQA_HEAD_DIM, GQA_KV_HEADS * GQA_HEAD_DIM, GQA_KV_HEADS * GQA_HEAD_DIM)
IN_OFF = tuple(int(o) for o in np.cumsum((0,) + IN_SPLITS))
D_IN = IN_OFF[-1]

TM = 256
EXP_ROWS = 256
HEAD_SLOT = LANE
ONE_LANE = 64

ZP_CQ, ZP_CKV, ZP_KR = 0, 256, 384
ZP_RQ, ZP_RK, ZP_RV, ZP_RG = 512, 1024, 1536, 2048
ZP_LU, ZP_LG = 2560, 3072
ZP_GQ, ZP_GK, ZP_GV = 3584, 4608, 4864
ZP_W = 5120


def _cparams(sem):
    return pltpu.CompilerParams(dimension_semantics=sem, vmem_limit_bytes=VMEM_LIMIT)


def _const_spec(shape):
    nd = len(shape)
    return pl.BlockSpec(shape, lambda *_: (0,) * nd, pipeline_mode=pl.Buffered(1))


PARTNER = LANE // 2


def _rot_lanes(rot_dim):
    q = rot_dim // 4
    e = np.arange(rot_dim)
    blk, o = e // q, e % q
    return np.where(blk == 0, o, np.where(blk == 1, PARTNER + o, np.where(blk == 2, q + o, PARTNER + q + o)))


HEAD_LANES = _rot_lanes(GQA_HEAD_DIM)
MLA_ROT_LANES = _rot_lanes(MLA_ROPE)
MLA_NOPE_LANES = np.array([l for l in range(MLA_NOPE + MLA_ROPE)
                           if l not in set(MLA_ROT_LANES.tolist())])


def _in_proj_columns():
    idx = np.full((ZP_W,), D_IN, np.int64)
    o = IN_OFF
    idx[ZP_CQ:ZP_CQ + 256] = o[0] + np.arange(256)
    idx[ZP_CKV:ZP_CKV + 128] = o[1] + np.arange(128)
    idx[ZP_KR + MLA_ROT_LANES] = o[2] + np.arange(MLA_ROPE)
    for h in range(RET_HEADS):
        idx[ZP_RQ + h * 128 + HEAD_LANES] = o[3] + h * 64 + np.arange(64)
        idx[ZP_RK + h * 128 + HEAD_LANES] = o[4] + h * 64 + np.arange(64)
    idx[ZP_RV:ZP_RV + 512] = o[5] + np.arange(512)
    idx[ZP_RG:ZP_RG + 512] = o[6] + np.arange(512)
    idx[ZP_LU:ZP_LU + 512] = o[7] + np.arange(512)
    idx[ZP_LG:ZP_LG + 512] = o[8] + np.arange(512)
    for h in range(GQA_HEADS):
        idx[ZP_GQ + h * 128 + HEAD_LANES] = o[9] + h * 64 + np.arange(64)
    for h in range(GQA_KV_HEADS):
        idx[ZP_GK + h * 128 + HEAD_LANES] = o[10] + h * 64 + np.arange(64)
        idx[ZP_GV + h * 128:ZP_GV + h * 128 + 64] = o[11] + h * 64 + np.arange(64)
    return idx


def _mla_up_columns():
    dq = MLA_NOPE + MLA_ROPE
    dkv = MLA_NOPE + MLA_V
    qi = np.full((MLA_HEADS * 128,), MLA_HEADS * dq, np.int64)
    ki = np.full((MLA_HEADS * 128,), MLA_HEADS * dkv, np.int64)
    vi = np.full((MLA_HEADS * 128,), MLA_HEADS * dkv, np.int64)
    for h in range(MLA_HEADS):
        qi[h * 128 + MLA_NOPE_LANES] = h * dq + np.arange(MLA_NOPE)
        qi[h * 128 + MLA_ROT_LANES] = h * dq + MLA_NOPE + np.arange(MLA_ROPE)
        ki[h * 128 + MLA_NOPE_LANES] = h * dkv + np.arange(MLA_NOPE)
        vi[h * 128:h * 128 + MLA_V] = h * dkv + MLA_NOPE + np.arange(MLA_V)
    return qi, np.concatenate([ki, vi])


def _take_cols(w, idx):
    wz = jnp.concatenate([w, jnp.zeros((w.shape[0], 1), w.dtype)], axis=1)
    return jnp.take(wz, jnp.asarray(idx, jnp.int32), axis=1)


def _rope_slot_tables(n_ctx, seq, rot_dim, lanes):
    half = rot_dim // 2
    pos = np.arange(seq)
    inv = ROPE_BASE ** (-np.arange(0, half, 2, dtype=np.float64) / half)
    ar = (pos // GRID_W)[:, None] * inv
    ac = (pos % GRID_W)[:, None] * inv
    cos = np.concatenate([np.cos(ar), np.cos(ar), np.cos(ac), np.cos(ac)], axis=1)
    sin = np.concatenate([-np.sin(ar), np.sin(ar), -np.sin(ac), np.sin(ac)], axis=1)
    cos_t = np.ones((n_ctx + seq, LANE), np.float32)
    sin_t = np.zeros((n_ctx + seq, LANE), np.float32)
    cos_t[n_ctx:, lanes] = cos
    sin_t[n_ctx:, lanes] = sin
    return cos_t, sin_t


def _rms(x, g):
    return x * lax.rsqrt(jnp.mean(x * x, axis=-1, keepdims=True) + EPS) * g


def _ada_body(c_ref, w_ref, b_ref, o_ref):
    c = c_ref[...]
    s = (c * jax.nn.sigmoid(c)).astype(BF16)
    o_ref[...] = jnp.dot(s, w_ref[...].astype(BF16), preferred_element_type=F32) + b_ref[...]


def _ada_mods(cc, ada_w, ada_b):
    depth, d, n = ada_w.shape
    r = cc.shape[0]
    tn = 1536
    return pl.pallas_call(
        _ada_body,
        grid=(depth, n // tn),
        in_specs=[pl.BlockSpec((r, d), lambda l, j: (0, 0)),
                  pl.BlockSpec((None, d, tn), lambda l, j: (l, 0, j)),
                  pl.BlockSpec((None, 1, tn), lambda l, j: (l, 0, j))],
        out_specs=pl.BlockSpec((None, r, tn), lambda l, j: (l, 0, j)),
        out_shape=jax.ShapeDtypeStruct((depth, r, n), F32),
        compiler_params=_cparams(("arbitrary", "arbitrary")),
        name="ada_mods",
    )(cc, ada_w, ada_b.reshape(depth, 1, n))


def _inproj_body(x_ref, mod_ref, g1_ref, tab_ref, win_ref, wuq_ref, wukv_ref, gq_ref, gkv_ref, hg_ref,
                 mq_ref, mk_ref, mv_ref, rq_ref, rk_ref, rv_ref, rg_ref, lu_ref, lg_ref,
                 gq_out, gk_out, gv_out):
    x = x_ref[...]
    shift = mod_ref[0:1, :]
    scale = mod_ref[1:2, :]
    h = _rms(x, g1_ref[...]) * (1.0 + scale) + shift
    hb = h.astype(BF16)

    def proj(c0, c1):
        return jnp.dot(hb, win_ref[:, c0:c1], preferred_element_type=F32)

    lane = lax.broadcasted_iota(jnp.int32, (x.shape[0], LANE), 1)
    one_col = jnp.where(lane == ONE_LANE, 1.0, 0.0).astype(F32)

    def rope(v, ci):
        return v * tab_ref[ci] + pltpu.roll(v, PARTNER, 1) * tab_ref[ci + 1]

    mla_scale = (MLA_NOPE + MLA_ROPE) ** -0.5
    gqa_scale = GQA_HEAD_DIM ** -0.5

    cq = proj(ZP_CQ, ZP_CQ + 256)
    qn = _rms(cq, gq_ref[...]).astype(BF16)
    q = jnp.dot(qn, wuq_ref[...], preferred_element_type=F32)
    for hh in range(MLA_HEADS):
        sl = slice(hh * 128, (hh + 1) * 128)
        mq_ref[:, sl] = rope(q[:, sl] * mla_scale, 0).astype(BF16)
    ckv = proj(ZP_CKV, ZP_CKV + 128)
    kvn = _rms(ckv, gkv_ref[...]).astype(BF16)
    kv = jnp.dot(kvn, wukv_ref[...], preferred_element_type=F32)
    kr = rope(proj(ZP_KR, ZP_KR + 128), 0)
    for hh in range(MLA_HEADS):
        sl = slice(hh * 128, (hh + 1) * 128)
        mk_ref[sl, :] = (kv[:, sl] + kr).T.astype(BF16)
        mv_ref[:, sl] = (kv[:, MLA_HEADS * 128 + hh * 128:MLA_HEADS * 128 + (hh + 1) * 128] + one_col).astype(BF16)

    rq = proj(ZP_RQ, ZP_RQ + 512)
    rk = proj(ZP_RK, ZP_RK + 512)
    for hh in range(RET_HEADS):
        sl = slice(hh * 128, (hh + 1) * 128)
        rq_ref[:, sl] = rope(rq[:, sl], 2).astype(BF16)
        rk_ref[:, sl] = rope(rk[:, sl] * RET_DK ** -0.5, 2).astype(BF16)
    rv_ref[...] = proj(ZP_RV, ZP_RV + 512).astype(BF16)
    rg_ref[...] = proj(ZP_RG, ZP_RG + 512).astype(BF16)

    lu_ref[...] = proj(ZP_LU, ZP_LU + 512).astype(BF16)
    lg_ref[...] = proj(ZP_LG, ZP_LG + 512).astype(BF16)

    gq = proj(ZP_GQ, ZP_GQ + 1024)
    for hh in range(GQA_HEADS):
        sl = slice(hh * 128, (hh + 1) * 128)
        v = gq[:, sl]
        v = v * lax.rsqrt(jnp.sum(v * v, axis=-1, keepdims=True) * (1.0 / GQA_HEAD_DIM) + EPS)
        gq_out[:, sl] = rope(v * (hg_ref[0:1, :] * gqa_scale), 2).astype(BF16)
    gk = proj(ZP_GK, ZP_GK + 256)
    gv = proj(ZP_GV, ZP_GV + 256)
    for hh in range(GQA_KV_HEADS):
        sl = slice(hh * 128, (hh + 1) * 128)
        v = gk[:, sl]
        v = v * lax.rsqrt(jnp.sum(v * v, axis=-1, keepdims=True) * (1.0 / GQA_HEAD_DIM) + EPS)
        gk_out[sl, :] = rope(v * hg_ref[1:2, :], 2).T.astype(BF16)
        gv_out[:, sl] = (gv[:, sl] + one_col).astype(BF16)


def _inproj(x_all, mods, g1, tabs, win_p, wuq_p, wukv_p, gq, gkv, head_gains):
    b, s, d = x_all.shape
    nt = s // TM
    widths = (1024, 1024, 1024, 512, 512, 512, 512, 512, 512, 1024, 256, 256)
    tile = lambda w: pl.BlockSpec((None, TM, w), lambda i, bb: (bb, i, 0))
    ttile = lambda w: pl.BlockSpec((None, w, TM), lambda i, bb: (bb, 0, i))
    transposed = (1, 10)
    return pl.pallas_call(
        _inproj_body,
        grid=(nt, b),
        in_specs=[tile(d),
                  pl.BlockSpec((None, None, 6, d), lambda i, bb: (bb, jnp.minimum(i, 1), 0, 0)),
                  _const_spec((1, d)),
                  pl.BlockSpec((4, TM, LANE), lambda i, bb: (0, i, 0)),
                  _const_spec(win_p.shape), _const_spec(wuq_p.shape), _const_spec(wukv_p.shape),
                  _const_spec((1, MLA_Q_LORA)), _const_spec((1, MLA_KV_LORA)), _const_spec((2, LANE))],
        out_specs=[ttile(w) if j in transposed else tile(w) for j, w in enumerate(widths)],
        out_shape=[jax.ShapeDtypeStruct((b, w, s) if j in transposed else (b, s, w), BF16)
                   for j, w in enumerate(widths)],
        compiler_params=_cparams(("arbitrary", "arbitrary")),
        name="in_proj",
    )(x_all, mods, g1, tabs, win_p, wuq_p, wukv_p, gq, gkv, head_gains)


ATT_TILES = 4


def _attn_body(*refs, heads, kv_heads, n_q):
    q_refs, (kt_ref, v_ref, o_ref) = refs[:n_q], refs[n_q:]
    grp = heads // kv_heads
    for hp in range(heads // 2):
        outs = []
        for h in (2 * hp, 2 * hp + 1):
            g = h // grp
            sl = slice(h * 128, (h + 1) * 128)
            q = q_refs[0][:, sl] if n_q == 1 else jnp.concatenate([r[:, sl] for r in q_refs], axis=0)
            kt = kt_ref[g * 128:(g + 1) * 128, :]
            v = v_ref[:, g * 128:(g + 1) * 128]
            s = jnp.dot(q, kt, preferred_element_type=F32).astype(BF16)
            m = jnp.max(s, axis=-1, keepdims=True)
            p = jnp.exp(s - m)
            o = jnp.dot(p, v, preferred_element_type=F32)
            outs.append(o[:, :64] / o[:, ONE_LANE:ONE_LANE + 1])
        o_ref[:, hp * 128:(hp + 1) * 128] = jnp.concatenate(outs, axis=1).astype(o_ref.dtype)


def _attention_lat(q, kt, v, heads, kv_heads, n_ctx):
    b, s, _ = q.shape
    off = n_ctx // TM
    rows = ATT_TILES * TM
    nq = (s - n_ctx) // rows
    qspec = lambda r: pl.BlockSpec((None, TM, heads * 128), lambda bb, j: (bb, off + ATT_TILES * j + r, 0))
    return pl.pallas_call(
        functools.partial(_attn_body, heads=heads, kv_heads=kv_heads, n_q=ATT_TILES),
        grid=(b, nq),
        in_specs=[qspec(r) for r in range(ATT_TILES)] + [
            pl.BlockSpec((None, kv_heads * 128, s), lambda bb, j: (bb, 0, 0)),
            pl.BlockSpec((None, s, kv_heads * 128), lambda bb, j: (bb, 0, 0))],
        out_specs=pl.BlockSpec((None, rows, heads * 64), lambda bb, j: (bb, j, 0)),
        out_shape=jax.ShapeDtypeStruct((b, nq * rows, heads * 64), BF16),
        compiler_params=_cparams(("arbitrary", "arbitrary")),
        name="attention_lat_h%d_kv%d" % (heads, kv_heads),
    )(*([q] * ATT_TILES), kt, v)


def _attention_ctx(q, kt, v, heads, kv_heads, n_ctx):
    b = q.shape[0]
    return pl.pallas_call(
        functools.partial(_attn_body, heads=heads, kv_heads=kv_heads, n_q=1),
        grid=(b,),
        in_specs=[pl.BlockSpec((None, n_ctx, heads * 128), lambda bb: (bb, 0, 0)),
                  pl.BlockSpec((None, kv_heads * 128, n_ctx), lambda bb: (bb, 0, 0)),
                  pl.BlockSpec((None, n_ctx, kv_heads * 128), lambda bb: (bb, 0, 0))],
        out_specs=pl.BlockSpec((None, n_ctx, heads * 64), lambda bb: (bb, 0, 0)),
        out_shape=jax.ShapeDtypeStruct((b, n_ctx, heads * 64), BF16),
        compiler_params=_cparams(("arbitrary",)),
        name="attention_ctx_h%d_kv%d" % (heads, kv_heads),
    )(q, kt, v)


def _attention(q, kt, v, heads, kv_heads, n_ctx, with_ctx):
    y = _attention_lat(q, kt, v, heads, kv_heads, n_ctx)
    if with_ctx:
        y = jnp.concatenate([_attention_ctx(q, kt, v, heads, kv_heads, n_ctx), y], axis=1)
    return y


RET_ROWS = 256


def _retention_body(lg_ref, q_ref, k_ref, v_ref, g_ref, gn_ref, o_ref,
                    kv_scr, st_scr, dec_scr, m_scr, *, n_ctx):
    c = RET_ROWS
    s = q_ref.shape[0]
    nc = s // c
    nctx = n_ctx // c
    back_order = list(range(nctx - 1, -1, -1)) + list(range(nc - 1, nctx - 1, -1))
    pos = lax.broadcasted_iota(jnp.int32, (c, LANE), 0).astype(F32)
    ri = lax.broadcasted_iota(jnp.int32, (c, c), 0)
    ci = lax.broadcasted_iota(jnp.int32, (c, c), 1)
    diff = (ri - ci).astype(F32)
    heads = [(h, slice(h * 128, (h + 1) * 128)) for h in range(RET_HEADS)]

    for h, _ in heads:
        lgf = lg_ref[0, h]
        lgb = lg_ref[1, h]
        dec_scr[h, 0] = jnp.exp(lgf * (c - 1.0 - pos))
        dec_scr[h, 1] = jnp.exp(lgb * pos)
        dec_scr[h, 2] = jnp.exp(lgf * (pos + 1.0))
        dec_scr[h, 3] = jnp.exp(lgb * (c - pos))
        m_scr[h] = jnp.where(diff >= 0, jnp.exp(lgf * jnp.maximum(diff, 0.0)),
                             jnp.exp(lgb * jnp.maximum(-diff, 0.0)))

    def kv_step(j, carry):
        r0 = pl.multiple_of(j * c, c)
        for h, sl in heads:
            kc = k_ref[pl.ds(r0, c), sl].astype(F32)
            vc = v_ref[pl.ds(r0, c), sl]
            kk = jnp.concatenate([(kc * dec_scr[h, 0]).astype(BF16), (kc * dec_scr[h, 1]).astype(BF16)], axis=1)
            kv_scr[h, j] = lax.dot_general(kk, vc, (((0,), (0,)), ((), ())), preferred_element_type=F32)
        return carry

    lax.fori_loop(0, nc, kv_step, 0, unroll=3)

    for h, _ in heads:
        gcf = jnp.exp(lg_ref[0, h] * c)
        gcb = jnp.exp(lg_ref[1, h] * c)
        sf = jnp.zeros((128, 128), F32)
        for j in range(nc):
            st_scr[h, j, 0:128, :] = sf.astype(BF16)
            sf = sf * gcf + kv_scr[h, j, 0:128, :]
        sb = jnp.zeros((128, 128), F32)
        for j in back_order:
            st_scr[h, j, 128:256, :] = sb.astype(BF16)
            sb = sb * gcb + kv_scr[h, j, 128:256, :]

    def out_step(j, carry):
        r0 = pl.multiple_of(j * c, c)
        for h, sl in heads:
            qb = q_ref[pl.ds(r0, c), sl]
            kb = k_ref[pl.ds(r0, c), sl]
            vc = v_ref[pl.ds(r0, c), sl]
            sc = lax.dot_general(qb, kb, (((1,), (1,)), ((), ())), preferred_element_type=F32) * m_scr[h]
            o = jnp.dot(sc.astype(BF16), vc, preferred_element_type=F32)
            qf = qb.astype(F32)
            qd = jnp.concatenate([(qf * dec_scr[h, 2]).astype(BF16), (qf * dec_scr[h, 3]).astype(BF16)], axis=1)
            o = o + jnp.dot(qd, st_scr[h, j], preferred_element_type=F32)
            mu = jnp.mean(o, axis=-1, keepdims=True)
            oc = o - mu
            var = jnp.mean(oc * oc, axis=-1, keepdims=True)
            y = oc * lax.rsqrt(var + EPS) * gn_ref[:, sl]
            gate = g_ref[pl.ds(r0, c), sl].astype(F32)
            o_ref[pl.ds(r0, c), sl] = (gate * jax.nn.sigmoid(gate) * y).astype(o_ref.dtype)
        return carry

    lax.fori_loop(0, nc, out_step, 0, unroll=3)


def _retention(log_g, q, k, v, g, gn, n_ctx):
    b, s, w = v.shape
    nc = s // RET_ROWS
    blk = lambda ww: pl.BlockSpec((None, s, ww), lambda bb: (bb, 0, 0))
    return pl.pallas_call(
        functools.partial(_retention_body, n_ctx=n_ctx),
        grid=(b,),
        in_specs=[pl.BlockSpec(memory_space=pltpu.SMEM),
                  blk(512), blk(512), blk(512), blk(512), _const_spec((1, 512))],
        out_specs=blk(512),
        out_shape=jax.ShapeDtypeStruct((b, s, 512), BF16),
        scratch_shapes=[pltpu.VMEM((RET_HEADS, nc, 256, 128), F32),
                        pltpu.VMEM((RET_HEADS, nc, 256, 128), BF16),
                        pltpu.VMEM((RET_HEADS, 4, RET_ROWS, LANE), F32),
                        pltpu.VMEM((RET_HEADS, RET_ROWS, RET_ROWS), F32)],
        compiler_params=_cparams(("arbitrary",)),
        name="retention",
    )(log_g, q, k, v, g, gn)


LRU_ROWS = 256
LRU_HALO = 16


def _tile_scan(a, bv, carry, reverse):
    row = lax.broadcasted_iota(jnp.int32, a.shape, 0)
    for dlt in (1, 2, 4):
        if reverse:
            a_s = pltpu.roll(a, SUBLANE - dlt, 0)
            b_s = pltpu.roll(bv, SUBLANE - dlt, 0)
            ok = row < SUBLANE - dlt
        else:
            a_s = pltpu.roll(a, dlt, 0)
            b_s = pltpu.roll(bv, dlt, 0)
            ok = row >= dlt
        bv = jnp.where(ok, a * b_s + bv, bv)
        a = jnp.where(ok, a * a_s, a)
    h = a * carry + bv
    new_carry = h[0:1, :] if reverse else h[SUBLANE - 1:SUBLANE, :]
    return h, new_carry


def _rglru_body(u_ref, g_ref, wc_ref, bc_ref, cw_ref, cb_ref, lam_ref, o_ref, a_scr, b_scr, *, n_ctx):
    s = u_ref.shape[0]
    w = LRU_WIDTH
    r = LRU_ROWS
    cdec = [-LRU_C * jax.nn.softplus(-lam_ref[d:d + 1, :]) for d in range(2)]
    zeros = jnp.zeros((LRU_HALO, w), F32)

    for ch in range(s // r):
        r0 = ch * r
        seg_start = r0 == 0 or r0 == n_ctx
        seg_end = r0 + r == n_ctx or r0 + r == s
        lo = r0 if seg_start else r0 - LRU_HALO
        hi = r0 + r if seg_end else r0 + r + LRU_HALO
        parts = [u_ref[lo:hi, :].astype(F32)]
        if seg_start:
            parts = [zeros] + parts
        if seg_end:
            parts = parts + [zeros]
        ext = jnp.concatenate(parts, axis=0) if len(parts) > 1 else parts[0]
        n = ext.shape[0]
        u = cb_ref[...]
        for j in range(CONV_W):
            sh = (CONV_PAD_LEFT - j) % n
            tap = ext if sh == 0 else pltpu.roll(ext, sh, 0)
            u = u + tap[LRU_HALO:LRU_HALO + r, :] * cw_ref[j:j + 1, :]
        gates = jnp.dot(u.astype(BF16), wc_ref[...], preferred_element_type=F32) + bc_ref[...]
        for d in range(2):
            rg = jax.nn.sigmoid(gates[:, (2 * d) * w:(2 * d + 1) * w])
            ig = jax.nn.sigmoid(gates[:, (2 * d + 1) * w:(2 * d + 2) * w])
            a = jnp.exp(rg * cdec[d])
            a_scr[d, r0:r0 + r, :] = a
            b_scr[d, r0:r0 + r, :] = jnp.sqrt(1.0 - a * a) * (ig * u)

    def fwd(t, carry):
        r0 = pl.multiple_of(t * SUBLANE, SUBLANE)
        h, carry = _tile_scan(a_scr[0, pl.ds(r0, SUBLANE), :], b_scr[0, pl.ds(r0, SUBLANE), :], carry, False)
        b_scr[0, pl.ds(r0, SUBLANE), :] = h
        return carry

    lax.fori_loop(0, s // SUBLANE, fwd, jnp.zeros((1, w), F32), unroll=4)

    def bwd(t0):
        def step(t, carry):
            r0 = pl.multiple_of((t0 - t) * SUBLANE, SUBLANE)
            h, carry = _tile_scan(a_scr[1, pl.ds(r0, SUBLANE), :], b_scr[1, pl.ds(r0, SUBLANE), :], carry, True)
            b_scr[1, pl.ds(r0, SUBLANE), :] = h
            return carry
        return step

    nct = n_ctx // SUBLANE
    carry = lax.fori_loop(0, nct, bwd(nct - 1), jnp.zeros((1, w), F32), unroll=4)
    lax.fori_loop(0, s // SUBLANE - nct, bwd(s // SUBLANE - 1), carry, unroll=4)

    def fin(j, carry):
        r0 = pl.multiple_of(j * r, r)
        hsum = b_scr[0, pl.ds(r0, r), :] + b_scr[1, pl.ds(r0, r), :]
        gate = g_ref[pl.ds(r0, r), :].astype(F32)
        o_ref[pl.ds(r0, r), :] = (hsum * jax.nn.gelu(gate)).astype(o_ref.dtype)
        return carry

    lax.fori_loop(0, s // r, fin, 0)


def _rglru(u, g, wcat, bcat, conv_w, conv_b, lam, n_ctx):
    b, s, w = u.shape
    blk = pl.BlockSpec((None, s, w), lambda bb: (bb, 0, 0))
    return pl.pallas_call(
        functools.partial(_rglru_body, n_ctx=n_ctx),
        grid=(b,),
        in_specs=[blk, blk, _const_spec(wcat.shape), _const_spec(bcat.shape),
                  _const_spec(conv_w.shape), _const_spec(conv_b.shape), _const_spec(lam.shape)],
        out_specs=blk,
        out_shape=jax.ShapeDtypeStruct((b, s, w), BF16),
        scratch_shapes=[pltpu.VMEM((2, s, w), F32), pltpu.VMEM((2, s, w), F32)],
        compiler_params=_cparams(("arbitrary",)),
        name="rglru",
    )(u, g, wcat, bcat, conv_w, conv_b, lam)


def _route_tile(scores, bias):
    r = scores.shape[0]
    neg = jnp.float32(-1.0e30)
    far = jnp.float32(1.0e9)
    ng = LANE // EXPERTS_PER_GROUP
    shape = (ng, EXPERTS_PER_GROUP, r)
    lane = lax.broadcasted_iota(jnp.int32, scores.shape, 1)
    x = jnp.where(lane < N_EXPERTS, scores + bias, neg).T.reshape(shape)
    st = scores.T.reshape(shape)
    sub = lax.broadcasted_iota(jnp.int32, shape, 1).astype(F32)
    grp = lax.broadcasted_iota(jnp.int32, shape, 0).astype(F32)
    m1 = jnp.max(x, axis=1, keepdims=True)
    i1 = jnp.min(jnp.where(x == m1, sub, far), axis=1, keepdims=True)
    x2 = jnp.where(sub == i1, neg, x)
    m2 = jnp.max(x2, axis=1, keepdims=True)
    i2 = jnp.min(jnp.where(x2 == m2, sub, far), axis=1, keepdims=True)
    gs = m1 + m2
    gmax = jnp.max(gs, axis=0, keepdims=True)
    grp1 = lax.broadcasted_iota(jnp.int32, (ng, 1, r), 0).astype(F32)
    g0 = jnp.min(jnp.where(gs == gmax, grp1, far), axis=0, keepdims=True)
    in_best = jnp.where(grp == g0, 1.0, 0.0)
    sel1 = jnp.where(sub == i1, in_best, 0.0)
    sel2 = jnp.where(sub == i2, in_best, 0.0)
    both = sel1 + sel2
    wsum = jnp.sum(jnp.sum(both * st, axis=1, keepdims=True), axis=0, keepdims=True)
    wmat = both * st / wsum
    mark = sel1 + 2.0 * sel2
    real = N_EXPERTS // EXPERTS_PER_GROUP
    return jnp.concatenate([wmat[:real], mark[:real]], axis=0).reshape(LANE, r).T


def _merge_body(x_ref, mod_ref, g1_ref, g2_ref, ya_ref, yb_ref, yc_ref, yd_ref,
                wm_ref, bm_ref, wb_ref, wo_ref, rwh_ref, rwl_ref, xo_ref, h2_ref, sc_ref):
    d = x_ref.shape[1]
    x = x_ref[...]
    h = (_rms(x, g1_ref[...]) * (1.0 + mod_ref[1:2, :]) + mod_ref[0:1, :]).astype(BF16)
    acc = jnp.zeros(x.shape, F32)
    for n, y_ref in enumerate((ya_ref, yb_ref, yc_ref, yd_ref)):
        gate = jax.nn.sigmoid(jnp.dot(h, wm_ref[:, n * d:(n + 1) * d], preferred_element_type=F32)
                              + bm_ref[:, n * d:(n + 1) * d])
        acc = acc + gate * jnp.dot(y_ref[...], wb_ref[n], preferred_element_type=F32)
    y = jnp.dot(acc.astype(BF16), wo_ref[...], preferred_element_type=F32)
    xn = x + mod_ref[2:3, :] * y
    xo_ref[...] = xn
    h2 = _rms(xn, g2_ref[...]) * (1.0 + mod_ref[4:5, :]) + mod_ref[3:4, :]
    h2_ref[...] = h2
    h2_hi = h2.astype(BF16)
    h2_lo = (h2 - h2_hi.astype(F32)).astype(BF16)
    logits = (jnp.dot(h2_hi, rwh_ref[...], preferred_element_type=F32)
              + jnp.dot(h2_lo, rwh_ref[...], preferred_element_type=F32)
              + jnp.dot(h2_hi, rwl_ref[...], preferred_element_type=F32))
    sc_ref[...] = jax.nn.sigmoid(logits)


def _merge(x_all, mods, g1, g2, ys, wm, bm, wb, wo, rwh, rwl, n_ctx, with_ctx):
    b, s, d = x_all.shape
    tile0 = 0 if with_ctx else n_ctx // TM
    nt = s // TM - tile0
    tile = lambda w: pl.BlockSpec((None, TM, w), lambda bb, i: (bb, i + tile0, 0))
    sub = lambda w: pl.BlockSpec((None, TM, w), lambda bb, i: (bb, i, 0))
    so = nt * TM
    return pl.pallas_call(
        _merge_body,
        grid=(b, nt),
        in_specs=[tile(d),
                  pl.BlockSpec((None, None, 6, d), lambda bb, i: (bb, jnp.minimum(i + tile0, 1), 0, 0)),
                  _const_spec((1, d)), _const_spec((1, d)),
                  sub(BRANCH_W), tile(BRANCH_W), tile(BRANCH_W), sub(BRANCH_W),
                  _const_spec(wm.shape), _const_spec(bm.shape), _const_spec(wb.shape),
                  _const_spec(wo.shape), _const_spec(rwh.shape), _const_spec(rwl.shape)],
        out_specs=[sub(d), sub(d), sub(LANE)],
        out_shape=[jax.ShapeDtypeStruct((b, so, d), F32),
                   jax.ShapeDtypeStruct((b, so, d), F32),
                   jax.ShapeDtypeStruct((b, so, LANE), F32)],
        compiler_params=_cparams(("arbitrary", "arbitrary")),
        name="merge",
    )(x_all, mods, g1, g2, *ys, wm, bm, wb, wo, rwh, rwl)


def _expert_body(be_ref, nu_ref, x_ref, w1_ref, w3_ref, w2_ref, o_ref):
    j = pl.program_id(0)

    @pl.when(j < nu_ref[0])
    def _():
        x = x_ref[...].astype(BF16)
        a = jnp.dot(x, w1_ref[...].astype(BF16), preferred_element_type=F32)
        g = jnp.dot(x, w3_ref[...].astype(BF16), preferred_element_type=F32)
        hmid = (a * jax.nn.sigmoid(a) * g).astype(BF16)
        o_ref[...] = jnp.dot(hmid, w2_ref[...].astype(BF16), preferred_element_type=F32)

    @pl.when(j >= nu_ref[0])
    def _():
        o_ref[...] = jnp.zeros(o_ref.shape, o_ref.dtype)


def _expert_ffn(block_expert, n_used, xb, w1, w3, w2, layer):
    n_rows, d = xb.shape
    nb = n_rows // EXP_ROWS
    ff = w1.shape[-1]
    wspec = lambda r, c: pl.BlockSpec((None, None, r, c), lambda j, be, nu: (layer, be[j], 0, 0))
    grid_spec = pltpu.PrefetchScalarGridSpec(
        num_scalar_prefetch=2,
        grid=(nb,),
        in_specs=[pl.BlockSpec((EXP_ROWS, d), lambda j, be, nu: (jnp.minimum(j, nu[0] - 1), 0)),
                  wspec(d, ff), wspec(d, ff), wspec(ff, d)],
        out_specs=pl.BlockSpec((EXP_ROWS, d), lambda j, be, nu: (j, 0)),
    )
    return pl.pallas_call(
        _expert_body,
        grid_spec=grid_spec,
        out_shape=jax.ShapeDtypeStruct((n_rows, d), F32),
        compiler_params=_cparams(("arbitrary",)),
        name="expert_ffn",
    )(block_expert, n_used, xb, w1, w3, w2)


def _markers(route):
    lane = lax.broadcasted_iota(jnp.int32, route.shape, 1)
    return jnp.where(lane < N_EXPERTS, pltpu.roll(route, N_EXPERTS, 1), 0.0)


ROUTE_ROWS = 1024


def _route_body(pr_ref, rb_ref, rt_ref, cnt_ref):
    @pl.when(pl.program_id(0) == 0)
    def _():
        cnt_ref[...] = jnp.zeros(cnt_ref.shape, F32)

    rt = _route_tile(pr_ref[...], rb_ref[...])
    rt_ref[...] = rt
    cnt_ref[...] += jnp.sum(jnp.where(_markers(rt) > 0.0, 1.0, 0.0), axis=0, keepdims=True)


def _route(probs, rb):
    t = probs.shape[0]
    tile = pl.BlockSpec((ROUTE_ROWS, LANE), lambda i: (i, 0))
    return pl.pallas_call(
        _route_body,
        grid=(t // ROUTE_ROWS,),
        in_specs=[tile, _const_spec(rb.shape)],
        out_specs=[tile, pl.BlockSpec((SUBLANE, LANE), lambda i: (0, 0))],
        out_shape=[jax.ShapeDtypeStruct((t, LANE), F32), jax.ShapeDtypeStruct((SUBLANE, LANE), F32)],
        compiler_params=_cparams(("arbitrary",)),
        name="route",
    )(probs, rb)


def _rank_body(rt_ref, cnt_ref, slot_ref, run_scr, start_scr, tri_scr):
    rows = rt_ref.shape[0]

    @pl.when(pl.program_id(0) == 0)
    def _():
        cnt = cnt_ref[...]
        padded = jnp.floor((cnt + (EXP_ROWS - 1.0)) * (1.0 / EXP_ROWS)) * EXP_ROWS
        lane = lax.broadcasted_iota(jnp.int32, cnt.shape, 1)
        end = padded
        sh = 1
        while sh < LANE:
            end = end + jnp.where(lane >= sh, pltpu.roll(end, sh, 1), 0.0)
            sh *= 2
        start_scr[...] = end - padded
        run_scr[...] = jnp.zeros(run_scr.shape, F32)
        ri = lax.broadcasted_iota(jnp.int32, (rows, rows), 0)
        ci = lax.broadcasted_iota(jnp.int32, (rows, rows), 1)
        tri_scr[...] = jnp.where(ri > ci, 1.0, 0.0).astype(BF16)

    mk = _markers(rt_ref[...])
    p = jnp.where(mk > 0.0, 1.0, 0.0)
    before = jnp.dot(tri_scr[...], p.astype(BF16), preferred_element_type=F32)
    pos = before + run_scr[0:1, :] + start_scr[0:1, :]
    s1 = jnp.sum(jnp.where(mk == 1.0, pos, 0.0), axis=-1, keepdims=True)
    s2 = jnp.sum(jnp.where(mk == 2.0, pos, 0.0), axis=-1, keepdims=True)
    which = lax.broadcasted_iota(jnp.int32, (rows, TOP_K), 1)
    slot_ref[...] = jnp.where(which == 0, s1, s2).astype(jnp.int32)
    run_scr[...] += jnp.sum(p, axis=0, keepdims=True)


def _rank(route, counts):
    t = route.shape[0]
    return pl.pallas_call(
        _rank_body,
        grid=(t // ROUTE_ROWS,),
        in_specs=[pl.BlockSpec((ROUTE_ROWS, LANE), lambda i: (i, 0)), _const_spec((SUBLANE, LANE))],
        out_specs=pl.BlockSpec((ROUTE_ROWS, TOP_K), lambda i: (i, 0)),
        out_shape=jax.ShapeDtypeStruct((t, TOP_K), jnp.int32),
        scratch_shapes=[pltpu.VMEM((SUBLANE, LANE), F32), pltpu.VMEM((SUBLANE, LANE), F32),
                        pltpu.VMEM((ROUTE_ROWS, ROUTE_ROWS), BF16)],
        compiler_params=_cparams(("arbitrary",)),
        name="rank",
    )(route, counts)


def _block_experts(counts, n_blocks):
    cnt = counts[0, :N_EXPERTS].astype(jnp.int32)
    padded = (cnt + EXP_ROWS - 1) // EXP_ROWS * EXP_ROWS
    pad_end = jnp.cumsum(padded)
    first_row = jnp.arange(n_blocks, dtype=jnp.int32) * EXP_ROWS
    ended = jnp.sum((pad_end[None, :] <= first_row[:, None]).astype(jnp.int32), axis=1)
    block_expert = jnp.minimum(ended, N_EXPERTS - 1).astype(jnp.int32)
    n_used = (pad_end[-1] // EXP_ROWS).astype(jnp.int32).reshape(1)
    return block_expert, n_used, cnt, pad_end.astype(jnp.int32)


def _row_copy(src, src_row, dst, dst_row, sem):
    return pltpu.make_async_copy(src.at[pl.ds(src_row, 1)], dst.at[pl.ds(dst_row, 1)], sem)


def _dispatch_body(slot_ref, cnt_ref, end_ref, nu_ref, h_ref, xb_ref, zero_scr, sem, zsem):
    tm = h_ref.shape[0]

    @pl.when(pl.program_id(0) == 0)
    def _():
        zero_scr[...] = jnp.zeros(zero_scr.shape, zero_scr.dtype)

        def fill(e, carry):
            @pl.when(cnt_ref[e] > 0)
            def _():
                r0 = pl.multiple_of(end_ref[e] - EXP_ROWS, EXP_ROWS)
                pltpu.make_async_copy(zero_scr, xb_ref.at[pl.ds(r0, EXP_ROWS)], zsem).start()
            return carry

        def drain(e, carry):
            @pl.when(cnt_ref[e] > 0)
            def _():
                pltpu.make_async_copy(zero_scr, xb_ref.at[pl.ds(0, EXP_ROWS)], zsem).wait()
            return carry

        def fill_tail(j, carry):
            r0 = pl.multiple_of(j * EXP_ROWS, EXP_ROWS)
            pltpu.make_async_copy(zero_scr, xb_ref.at[pl.ds(r0, EXP_ROWS)], zsem).start()
            return carry

        def drain_tail(j, carry):
            pltpu.make_async_copy(zero_scr, xb_ref.at[pl.ds(0, EXP_ROWS)], zsem).wait()
            return carry

        n_blocks = xb_ref.shape[0] // EXP_ROWS
        lax.fori_loop(0, N_EXPERTS, fill, 0)
        lax.fori_loop(nu_ref[0], n_blocks, fill_tail, 0)
        lax.fori_loop(0, N_EXPERTS, drain, 0)
        lax.fori_loop(nu_ref[0], n_blocks, drain_tail, 0)

    for t in range(tm):
        for k in range(TOP_K):
            _row_copy(h_ref, t, xb_ref, slot_ref[0, TOP_K * t + k], sem).start(priority=k % 2)
    for k in range(TOP_K):
        pltpu.make_async_copy(h_ref, xb_ref.at[pl.ds(0, tm)], sem).wait()


def _dispatch(slot_tiles, cnt, pad_end, n_used, h2, n_rows):
    b, s, dh = h2.shape
    nt = b * s // TM
    smem = pl.BlockSpec(memory_space=pltpu.SMEM)
    return pl.pallas_call(
        _dispatch_body,
        grid=(nt,),
        in_specs=[pl.BlockSpec((None, 1, TOP_K * TM), lambda i: (i, 0, 0), memory_space=pltpu.SMEM),
                  smem, smem, smem,
                  pl.BlockSpec((TM, dh), lambda i: (i, 0))],
        out_specs=pl.BlockSpec(memory_space=pl.ANY),
        out_shape=jax.ShapeDtypeStruct((n_rows, dh), h2.dtype),
        scratch_shapes=[pltpu.VMEM((EXP_ROWS, dh), h2.dtype),
                        pltpu.SemaphoreType.DMA(()), pltpu.SemaphoreType.DMA(())],
        compiler_params=_cparams(("arbitrary",)),
        name="dispatch",
    )(slot_tiles, cnt, pad_end, n_used, h2.reshape(b * s, dh))


def _resid_body(slot_ref, x_ref, mod_ref, rt_ref, gf_ref, y_hbm, o_ref, ybuf0, ybuf1, sem, *, final):
    tm = x_ref.shape[0]
    ybuf = (ybuf0, ybuf1)

    for t in range(tm):
        for k in range(TOP_K):
            _row_copy(y_hbm, slot_ref[0, TOP_K * t + k], ybuf[k], t, sem).start(priority=k % 2)
    rt = rt_ref[...]
    mk = _markers(rt)
    w1 = jnp.sum(jnp.where(mk == 1.0, rt, 0.0), axis=-1, keepdims=True)
    w2 = jnp.sum(jnp.where(mk == 2.0, rt, 0.0), axis=-1, keepdims=True)
    for k in range(TOP_K):
        pltpu.make_async_copy(y_hbm.at[pl.ds(0, tm)], ybuf[k], sem).wait()
    f = w1 * ybuf0[...] + w2 * ybuf1[...]
    xn = x_ref[...] + mod_ref[5:6, :] * f
    if final:
        xn = _rms(xn, gf_ref[...])
    o_ref[...] = xn


def _moe_residual(slot_tiles, x_all, mods, route, gf, yblk, has_ctx, final):
    b, s, d = x_all.shape
    nt = s // TM
    tin = lambda ww: pl.BlockSpec((None, TM, ww), lambda bb, i: (bb, i, 0))
    mod_row = (lambda i: jnp.minimum(i, 1)) if has_ctx else (lambda i: 1)
    return pl.pallas_call(
        functools.partial(_resid_body, final=final),
        grid=(b, nt),
        in_specs=[pl.BlockSpec((None, 1, TOP_K * TM), lambda bb, i: (bb * nt + i, 0, 0),
                               memory_space=pltpu.SMEM),
                  tin(d),
                  pl.BlockSpec((None, None, 6, d), lambda bb, i: (bb, mod_row(i), 0, 0)),
                  tin(LANE), _const_spec((1, d)),
                  pl.BlockSpec(memory_space=pl.ANY)],
        out_specs=tin(d),
        out_shape=jax.ShapeDtypeStruct((b, s, d), F32),
        scratch_shapes=[pltpu.VMEM((TM, d), F32), pltpu.VMEM((TM, d), F32), pltpu.SemaphoreType.DMA(())],
        compiler_params=_cparams(("arbitrary", "arbitrary")),
        name="moe_residual",
    )(slot_tiles, x_all, mods, route, gf, yblk)


def kernel(x, c, ctx, c_ctx, ada_w, ada_b, norm1_g, norm2_g, w_in, mla_q_norm, mla_kv_norm, mla_w_uq,
           mla_w_ukv, ret_decay, ret_norm, lru_conv_w, lru_conv_b, lru_w_a, lru_b_a, lru_w_x, lru_b_x,
           lru_lambda, gqa_q_norm, gqa_k_norm, w_branch, w_merge, b_merge, w_out, router_w, router_bias,
           moe_w1, moe_w3, moe_w2, final_norm):
    b, seq, d = x.shape
    n_ctx = ctx.shape[1]
    depth = ada_w.shape[0]
    s = n_ctx + seq
    assert n_ctx == TM and seq % TM == 0 and seq % GRID_W == 0

    r_pad = -(-(b + 1) // SUBLANE) * SUBLANE
    cc = jnp.zeros((r_pad, d), F32).at[:b].set(c).at[b].set(c_ctx)
    mods_all = _ada_mods(cc, ada_w, ada_b)

    tabs = jnp.asarray(np.stack(_rope_slot_tables(n_ctx, seq, MLA_ROPE, MLA_ROT_LANES)
                                + _rope_slot_tables(n_ctx, seq, GQA_HEAD_DIM, HEAD_LANES)))

    in_cols = _in_proj_columns()
    uq_cols, ukv_cols = _mla_up_columns()
    rw = jnp.concatenate([router_w.astype(F32), jnp.zeros((d, LANE - N_EXPERTS), F32)], axis=1)
    rw_hi = rw.astype(BF16)
    rw_lo = (rw - rw_hi.astype(F32)).astype(BF16)
    rb = jnp.concatenate([router_bias.astype(F32), jnp.zeros((LANE - N_EXPERTS,), F32)])[None]

    x_all = jnp.concatenate([ctx, x], axis=1)
    out = None
    for l in range(depth):
        last = l == depth - 1
        m = mods_all[l].reshape(r_pad, 6, d)
        mods = jnp.stack([jnp.broadcast_to(m[b], (b, 6, d)), m[:b]], axis=1)

        head_gains = jnp.zeros((2, LANE), F32).at[:, HEAD_LANES].set(
            jnp.stack([gqa_q_norm[l], gqa_k_norm[l]]).astype(F32))

        win_p = _take_cols(w_in[l], in_cols).astype(BF16)
        wuq_p = _take_cols(mla_w_uq[l], uq_cols).astype(BF16)
        wukv_p = _take_cols(mla_w_ukv[l], ukv_cols).astype(BF16)

        (mq, mk, mv, rq, rk, rv, rg, lu, lg, gq, gk, gv) = _inproj(
            x_all, mods, norm1_g[l][None], tabs, win_p, wuq_p, wukv_p,
            mla_q_norm[l][None], mla_kv_norm[l][None], head_gains)

        ya = _attention(mq, mk, mv, MLA_HEADS, MLA_HEADS, n_ctx, not last)
        yd = _attention(gq, gk, gv, GQA_HEADS, GQA_KV_HEADS, n_ctx, not last)

        log_g = -jax.nn.softplus(-ret_decay[l].astype(F32))
        yb = _retention(log_g, rq, rk, rv, rg, ret_norm[l][None], n_ctx)

        eye = jnp.eye(LRU_BLOCKS, dtype=F32)

        def block_diag(wblk):
            return jnp.einsum('ncd,nm->ncmd', wblk, eye).reshape(LRU_WIDTH, LRU_WIDTH)

        wcat = jnp.concatenate([block_diag(lru_w_a[l, 0]), block_diag(lru_w_x[l, 0]),
                                block_diag(lru_w_a[l, 1]), block_diag(lru_w_x[l, 1])], axis=1).astype(BF16)
        bcat = jnp.concatenate([lru_b_a[l, 0], lru_b_x[l, 0], lru_b_a[l, 1], lru_b_x[l, 1]])[None]
        yc = _rglru(lu, lg, wcat, bcat, lru_conv_w[l], lru_conv_b[l][None], lru_lambda[l], n_ctx)

        x_all, h2, probs = _merge(
            x_all, mods, norm1_g[l][None], norm2_g[l][None], (ya, yb, yc, yd),
            w_merge[l].astype(BF16), b_merge[l][None], w_branch[l].astype(BF16), w_out[l].astype(BF16),
            rw_hi, rw_lo, n_ctx, not last)

        t = b * (seq if last else s)
        assert t % ROUTE_ROWS == 0
        n_blocks = -(-t * TOP_K // EXP_ROWS) + N_EXPERTS
        route, counts = _route(probs.reshape(t, LANE), rb)
        slot_tiles = _rank(route, counts).reshape(t // TM, 1, TOP_K * TM)
        route = route.reshape(b, t // b, LANE)
        block_expert, n_used, cnt, pad_end = _block_experts(counts, n_blocks)
        xb = _dispatch(slot_tiles, cnt, pad_end, n_used, h2, n_blocks * EXP_ROWS)
        yblk = _expert_ffn(block_expert, n_used, xb, moe_w1, moe_w3, moe_w2, l)
        out = _moe_residual(slot_tiles, x_all, mods, route, final_norm[None], yblk, not last, last)
        x_all = out
    return out
```
